```python
import jax
import jax.numpy as jnp
from jax import lax
import numpy as np

D_MODEL = 1024
BATCH = 4
SEQ = 4096
DEPTH = 2

CTX_LEN = 256
GRID_W = 64
ROPE_BASE = 10000.0
EPS = 1e-6
HALF = 0.5
N_MOD = 9
D_FF = 2816

A_HEADS = 4
A_DK = 32
A_DV = 64
A_CHUNK = 64
B_HEADS = 6
B_Q_RANK = 256
B_KV_RANK = 128
B_NOPE = 64
B_ROPE = 32
B_DV = 64
B_DQK = B_NOPE + B_ROPE
B_SCALE = B_DQK ** -0.5
DENSE_QBLOCK = 128
C_HEADS = 6
C_KV_HEADS = 2
C_GROUP = C_HEADS // C_KV_HEADS
C_DH = 64
C_SCALE = C_DH ** -0.5
WINDOW = 128
C_BLOCK = 128

IN_SIZES = (A_HEADS * A_DK, A_HEADS * A_DK, A_HEADS * A_DV, A_HEADS * A_DV, 4 * A_HEADS,
            B_Q_RANK, B_KV_RANK, B_ROPE,
            C_HEADS * C_DH, C_KV_HEADS * C_DH, C_KV_HEADS * C_DH)
D_IN = sum(IN_SIZES)
MIX_WIDTH = A_HEADS * A_DV + B_HEADS * B_DV + C_HEADS * C_DH

kernel_name = "hybrid_mlstm_mla_swa_macaron_dit"


def rms_norm(x, g):
    xf = x.astype(jnp.float32)
    y = xf * lax.rsqrt(jnp.mean(xf * xf, axis=-1, keepdims=True) + EPS)
    return (y * g.astype(jnp.float32)).astype(x.dtype)


def modulate(h, g, shift, scale):
    return rms_norm(h, g) * (1 + scale) + shift


def swiglu(h, wi, wo):
    gt, up = jnp.split(h @ wi, 2, axis=-1)
    return (jax.nn.silu(gt) * up) @ wo


def rope_1d(x, pos):
    half = x.shape[-1] // 2
    inv = ROPE_BASE ** (-jnp.arange(half, dtype=jnp.float32) / half)
    ang = pos.astype(jnp.float32)[:, None] * inv
    cos = jnp.cos(ang)[:, None, :]
    sin = jnp.sin(ang)[:, None, :]
    xf = x.astype(jnp.float32)
    x1, x2 = xf[..., :half], xf[..., half:]
    return jnp.concatenate([x1 * cos - x2 * sin, x1 * sin + x2 * cos], axis=-1).astype(x.dtype)


def rope_2d(x, row, col):
    r = x.shape[-1] // 2
    return jnp.concatenate([rope_1d(x[..., :r], row), rope_1d(x[..., r:], col)], axis=-1)


def split_cols(p):
    return jnp.split(p, [int(i) for i in np.cumsum(IN_SIZES)[:-1]], axis=-1)


def mlstm_chunked(q, k, v, ig, lf, state):
    B, H, T, DK = q.shape
    L = A_CHUNK
    NC = T // L

    def to_chunks(a):
        return jnp.moveaxis(a.reshape((B, H, NC, L) + a.shape[3:]), 2, 0)

    mask = jnp.tril(jnp.ones((L, L), dtype=bool))

    def step(carry, inp):
        C, n, m = carry
        qq, kk, vv, ii, ff = inp
        b = jnp.cumsum(ff, axis=-1)
        dlog = jnp.where(mask, b[..., :, None] - b[..., None, :] + ii[..., None, :], -jnp.inf)
        inter = b + m[..., None]
        m_t = jnp.maximum(inter, jnp.max(dlog, axis=-1))
        w = jnp.exp(dlog - m_t[..., None])
        a_inter = jnp.exp(inter - m_t)
        s = jnp.einsum('bhtd,bhsd->bhts', qq, kk) * w
        num = a_inter[..., None] * jnp.einsum('bhtd,bhde->bhte', qq, C) + jnp.einsum('bhts,bhse->bhte', s, vv)
        den = a_inter * jnp.einsum('bhtd,bhd->bht', qq, n) + jnp.sum(s, axis=-1)
        h = num / jnp.maximum(jnp.abs(den), jnp.exp(-m_t))[..., None]
        b_last = b[..., -1]
        g = b_last[..., None] - b + ii
        m_new = jnp.maximum(b_last + m, jnp.max(g, axis=-1))
        decay = jnp.exp(b_last + m - m_new)
        wg = jnp.exp(g - m_new[..., None])
        C_new = decay[..., None, None] * C + jnp.einsum('bhs,bhsd,bhse->bhde', wg, kk, vv)
        n_new = decay[..., None] * n + jnp.einsum('bhs,bhsd->bhd', wg, kk)
        return (C_new, n_new, m_new), h

    state, hs = lax.scan(step, state, tuple(to_chunks(a) for a in (q, k, v, ig, lf)))
    h = jnp.moveaxis(hs, 0, 2).reshape(B, H, T, v.shape[-1])
    return h, state


def mlstm_branch(parts_l, parts_c, gate_b, out_norm, need_ctx):
    def prep(q, k, v, g):
        B, T, _ = q.shape

        def hd(a, d):
            return a.reshape(B, T, A_HEADS, d).transpose(0, 2, 1, 3).astype(jnp.float32)

        g = (g.astype(jnp.float32) + gate_b.astype(jnp.float32)).reshape(B, T, 4, A_HEADS).transpose(2, 0, 3, 1)
        return (hd(q, A_DK) * A_DK ** -0.5, hd(k, A_DK), hd(v, A_DV),
                g[0], jax.nn.log_sigmoid(g[1]), g[2], jax.nn.log_sigmoid(g[3]))

    def rev(a):
        return jnp.flip(a, axis=2)

    def both(q, k, v, ig_f, lf_f, ig_b, lf_b, st_f, st_b):
        h_f, st_f = mlstm_chunked(q, k, v, ig_f, lf_f, st_f)
        h_b, st_b = mlstm_chunked(rev(q), rev(k), rev(v), rev(ig_b), rev(lf_b), st_b)
        return h_f + rev(h_b), st_f, st_b

    def finish(h, o):
        B, _, T, _ = h.shape
        h = rms_norm(h.transpose(0, 2, 1, 3), out_norm.reshape(A_HEADS, A_DV))
        return (jax.nn.sigmoid(o.astype(jnp.float32)) * h.reshape(B, T, A_HEADS * A_DV)).astype(o.dtype)

    in_c = prep(parts_c[0], parts_c[1], parts_c[2], parts_c[4])
    in_l = prep(parts_l[0], parts_l[1], parts_l[2], parts_l[4])
    B = in_c[0].shape[0]
    zero = (jnp.zeros((B, A_HEADS, A_DK, A_DV), jnp.float32),
            jnp.zeros((B, A_HEADS, A_DK), jnp.float32),
            jnp.zeros((B, A_HEADS), jnp.float32))
    h_c, st_f, st_b = both(*in_c, zero, zero)
    h_l, _, _ = both(*in_l, st_f, st_b)
    y_l = finish(h_l, parts_l[3])
    y_c = finish(h_c, parts_c[3]) if need_ctx else None
    return y_l, y_c


def dense_attend(q, k, v, scale):
    s = jnp.einsum('bhqd,bhkd->bhqk', q, k).astype(jnp.float32) * scale
    p = jax.nn.softmax(s, axis=-1)
    return jnp.einsum('bhqk,bhkd->bhqd', p.astype(v.dtype), v)


def mla_branch(parts_l, parts_c, cq_norm, ckv_norm, w_uq, w_ukv, q_norm, k_norm, row, col, need_ctx):
    def heads(cq, ckv, kr, rotate):
        B, T, _ = cq.shape
        q = (rms_norm(cq, cq_norm) @ w_uq).reshape(B, T, B_HEADS, B_DQK)
        kv = (rms_norm(ckv, ckv_norm) @ w_ukv).reshape(B, T, B_HEADS, B_NOPE + B_DV)
        k = jnp.concatenate([kv[..., :B_NOPE], jnp.broadcast_to(kr[:, :, None, :], (B, T, B_HEADS, B_ROPE))], axis=-1)
        v = kv[..., B_NOPE:]
        q = rms_norm(q, q_norm)
        k = rms_norm(k, k_norm)
        if rotate:
            q = jnp.concatenate([q[..., :B_NOPE], rope_2d(q[..., B_NOPE:], row, col)], axis=-1)
            k = jnp.concatenate([k[..., :B_NOPE], rope_2d(k[..., B_NOPE:], row, col)], axis=-1)
        return q.transpose(0, 2, 1, 3), k.transpose(0, 2, 1, 3), v.transpose(0, 2, 1, 3)

    q_l, k_l, v_l = heads(parts_l[0], parts_l[1], parts_l[2], True)
    q_c, k_c, v_c = heads(parts_c[0], parts_c[1], parts_c[2], False)
    B, H, T, _ = q_l.shape
    k_all = jnp.concatenate([k_c, k_l], axis=2)
    v_all = jnp.concatenate([v_c, v_l], axis=2)
    nq = T // DENSE_QBLOCK
    qb = q_l.reshape(B, H, nq, DENSE_QBLOCK, B_DQK).transpose(2, 0, 1, 3, 4)
    ob = lax.map(lambda qq: dense_attend(qq, k_all, v_all, B_SCALE), qb)
    y_l = ob.transpose(1, 0, 3, 2, 4).reshape(B, T, H * B_DV)
    y_c = None
    if need_ctx:
        y_c = dense_attend(q_c, k_c, v_c, B_SCALE).transpose(0, 2, 1, 3).reshape(B, q_c.shape[2], H * B_DV)
    return y_l, y_c


def gqa_branch(parts_l, parts_c, q_norm, k_norm, sink, row, col, need_ctx):
    def heads(q, k, v, rotate):
        B, T, _ = q.shape
        q = rms_norm(q.reshape(B, T, C_HEADS, C_DH), q_norm)
        k = rms_norm(k.reshape(B, T, C_KV_HEADS, C_DH), k_norm)
        v = v.reshape(B, T, C_KV_HEADS, C_DH)
        if rotate:
            q = rope_2d(q, row, col)
            k = rope_2d(k, row, col)
        q = q.reshape(B, T, C_KV_HEADS, C_GROUP, C_DH).transpose(0, 2, 3, 1, 4)
        return q, k.transpose(0, 2, 1, 3), v.transpose(0, 2, 1, 3)

    q_l, k_l, v_l = heads(parts_l[0], parts_l[1], parts_l[2], True)
    q_c, k_c, v_c = heads(parts_c[0], parts_c[1], parts_c[2], False)
    B, KVH, G, T, dh = q_l.shape
    n_ctx = k_c.shape[2]
    nb = T // C_BLOCK
    sink_h = sink.astype(jnp.float32).reshape(C_KV_HEADS, C_GROUP)

    def band(a):
        ap = jnp.pad(a, ((0, 0), (0, 0), (C_BLOCK, C_BLOCK), (0, 0))).reshape(B, KVH, nb + 2, C_BLOCK, dh)
        return jnp.concatenate([ap[:, :, :-2], ap[:, :, 1:-1], ap[:, :, 2:]], axis=3)

    kb, vb = band(k_l), band(v_l)
    qb = q_l.reshape(B, KVH, G, nb, C_BLOCK, dh)
    s_band = jnp.einsum('bkgnqd,bknsd->bkgnqs', qb, kb).astype(jnp.float32) * C_SCALE
    blk = jnp.arange(nb)[:, None, None] * C_BLOCK
    qpos = blk + jnp.arange(C_BLOCK)[None, :, None]
    kpos = blk - C_BLOCK + jnp.arange(3 * C_BLOCK)[None, None, :]
    valid = (jnp.abs(qpos - kpos) <= WINDOW) & (kpos >= 0) & (kpos < T)
    s_band = jnp.where(valid, s_band, -jnp.inf)
    s_ctx = jnp.einsum('bkgnqd,bkcd->bkgnqc', qb, k_c).astype(jnp.float32) * C_SCALE
    s_sink = jnp.broadcast_to(sink_h[None, :, :, None, None, None], s_ctx.shape[:-1] + (1,))
    p = jax.nn.softmax(jnp.concatenate([s_sink, s_ctx, s_band], axis=-1), axis=-1)
    o = (jnp.einsum('bkgnqc,bkcd->bkgnqd', p[..., 1:1 + n_ctx].astype(v_c.dtype), v_c)
         + jnp.einsum('bkgnqs,bknsd->bkgnqd', p[..., 1 + n_ctx:].astype(vb.dtype), vb))
    y_l = o.reshape(B, KVH, G, T, dh).transpose(0, 3, 1, 2, 4).reshape(B, T, C_HEADS * C_DH)
    y_c = None
    if need_ctx:
        s = jnp.einsum('bkgtd,bkcd->bkgtc', q_c, k_c).astype(jnp.float32) * C_SCALE
        s0 = jnp.broadcast_to(sink_h[None, :, :, None, None], s.shape[:-1] + (1,))
        pc = jax.nn.softmax(jnp.concatenate([s0, s], axis=-1), axis=-1)[..., 1:]
        oc = jnp.einsum('bkgtc,bkcd->bkgtd', pc.astype(v_c.dtype), v_c)
        y_c = oc.transpose(0, 3, 1, 2, 4).reshape(B, n_ctx, C_HEADS * C_DH)
    return y_l, y_c


def token_mix(h_l, h_c, w_in, mlstm_gate_b, mlstm_out_norm, mla_cq_norm, mla_ckv_norm, mla_w_uq, mla_w_ukv,
              mla_q_norm, mla_k_norm, gqa_q_norm, gqa_k_norm, gqa_sink, row, col, need_ctx):
    p_l = split_cols(h_l @ w_in)
    p_c = split_cols(h_c @ w_in)
    ya_l, ya_c = mlstm_branch(p_l[0:5], p_c[0:5], mlstm_gate_b, mlstm_out_norm, need_ctx)
    yb_l, yb_c = mla_branch(p_l[5:8], p_c[5:8], mla_cq_norm, mla_ckv_norm, mla_w_uq, mla_w_ukv,
                            mla_q_norm, mla_k_norm, row, col, need_ctx)
    yc_l, yc_c = gqa_branch(p_l[8:11], p_c[8:11], gqa_q_norm, gqa_k_norm, gqa_sink, row, col, need_ctx)
    y_l = jnp.concatenate([ya_l, yb_l, yc_l], axis=-1)
    y_c = jnp.concatenate([ya_c, yb_c, yc_c], axis=-1) if need_ctx else None
    return y_l, y_c


def setup_inputs(seed: int = 0) -> dict:
    key = jax.random.key(seed)
    ks = iter(jax.random.split(key, 32))
    f32 = jnp.float32

    def nrm(shape, fan_in, scale=1.0):
        return jax.random.normal(next(ks), shape, f32) * (scale * fan_in ** -0.5)

    def gain(shape):
        return 1.0 + 0.05 * jax.random.normal(next(ks), shape, f32)

    fbias = jnp.linspace(3.0, 6.0, A_HEADS, dtype=f32)
    zb = jnp.zeros((A_HEADS,), f32)
    gate_base = jnp.concatenate([zb, fbias, zb, fbias])
    return {
        'x': jax.random.normal(next(ks), (BATCH, SEQ, D_MODEL), f32),
        'c': jax.random.normal(next(ks), (BATCH, D_MODEL), f32),
        'ctx': jax.random.normal(next(ks), (BATCH, CTX_LEN, D_MODEL), f32),
        'c_ctx': jax.random.normal(next(ks), (D_MODEL,), f32),
        'ada_w': nrm((DEPTH, D_MODEL, N_MOD * D_MODEL), D_MODEL, 0.5),
        'ada_b': 0.02 * jax.random.normal(next(ks), (DEPTH, N_MOD * D_MODEL), f32),
        'norm_g': gain((DEPTH, 3, D_MODEL)),
        'ffn1_wi': nrm((DEPTH, D_MODEL, 2 * D_FF), D_MODEL),
        'ffn1_wo': nrm((DEPTH, D_FF, D_MODEL), D_FF),
        'ffn2_wi': nrm((DEPTH, D_MODEL, 2 * D_FF), D_MODEL),
        'ffn2_wo': nrm((DEPTH, D_FF, D_MODEL), D_FF),
        'w_in': nrm((DEPTH, D_MODEL, D_IN), D_MODEL),
        'w_out': nrm((DEPTH, MIX_WIDTH, D_MODEL), MIX_WIDTH),
        'mlstm_gate_b': gate_base[None, :] + 0.1 * jax.random.normal(next(ks), (DEPTH, 4 * A_HEADS), f32),
        'mlstm_out_norm': gain((DEPTH, A_HEADS * A_DV)),
        'mla_cq_norm': gain((DEPTH, B_Q_RANK)),
        'mla_ckv_norm': gain((DEPTH, B_KV_RANK)),
        'mla_w_uq': nrm((DEPTH, B_Q_RANK, B_HEADS * B_DQK), B_Q_RANK),
        'mla_w_ukv': nrm((DEPTH, B_KV_RANK, B_HEADS * (B_NOPE + B_DV)), B_KV_RANK),
        'mla_q_norm': gain((DEPTH, B_DQK)),
        'mla_k_norm': gain((DEPTH, B_DQK)),
        'gqa_q_norm': gain((DEPTH, C_DH)),
        'gqa_k_norm': gain((DEPTH, C_DH)),
        'gqa_sink': 0.5 * jax.random.normal(next(ks), (DEPTH, C_HEADS), f32),
    }


def reference(x, c, ctx, c_ctx, ada_w, ada_b, norm_g, ffn1_wi, ffn1_wo, ffn2_wi, ffn2_wo, w_in, w_out,
              mlstm_gate_b, mlstm_out_norm, mla_cq_norm, mla_ckv_norm, mla_w_uq, mla_w_ukv, mla_q_norm, mla_k_norm,
              gqa_q_norm, gqa_k_norm, gqa_sink):
    B, T, D = x.shape
    ROWS = T // GRID_W
    row = jnp.repeat(jnp.arange(ROWS, dtype=jnp.int32), GRID_W)
    col = jnp.arange(ROWS * GRID_W, dtype=jnp.int32) % GRID_W
    xc = ctx
    for l in range(DEPTH):
        need_ctx = l < DEPTH - 1
        mod_l = (jax.nn.silu(c) @ ada_w[l] + ada_b[l]).reshape(B, N_MOD, D).transpose(1, 0, 2)[:, :, None, :]
        mod_c = (jax.nn.silu(c_ctx) @ ada_w[l] + ada_b[l]).reshape(N_MOD, 1, 1, D)
        x = x + HALF * mod_l[2] * swiglu(modulate(x, norm_g[l, 0], mod_l[0], mod_l[1]), ffn1_wi[l], ffn1_wo[l])
        xc = xc + HALF * mod_c[2] * swiglu(modulate(xc, norm_g[l, 0], mod_c[0], mod_c[1]), ffn1_wi[l], ffn1_wo[l])
        y_l, y_c = token_mix(modulate(x, norm_g[l, 1], mod_l[3], mod_l[4]),
                             modulate(xc, norm_g[l, 1], mod_c[3], mod_c[4]),
                             w_in[l], mlstm_gate_b[l], mlstm_out_norm[l], mla_cq_norm[l], mla_ckv_norm[l],
                             mla_w_uq[l], mla_w_ukv[l], mla_q_norm[l], mla_k_norm[l],
                             gqa_q_norm[l], gqa_k_norm[l], gqa_sink[l], row, col, need_ctx)
        x = x + mod_l[5] * (y_l @ w_out[l])
        x = x + HALF * mod_l[8] * swiglu(modulate(x, norm_g[l, 2], mod_l[6], mod_l[7]), ffn2_wi[l], ffn2_wo[l])
        if need_ctx:
            xc = xc + mod_c[5] * (y_c @ w_out[l])
            xc = xc + HALF * mod_c[8] * swiglu(modulate(xc, norm_g[l, 2], mod_c[6], mod_c[7]), ffn2_wi[l], ffn2_wo[l])
    return x
```

```python
import functools

import jax
import jax.numpy as jnp
import numpy as np
from jax import lax
from jax.experimental import pallas as pl
from jax.experimental.pallas import tpu as pltpu

f32 = jnp.float32
bf16 = jnp.bfloat16

D = 1024
B = 4
T = 4096
CTX = 256
DEPTH = 2
GRID_W = 64
ROPE_BASE = 10000.0
EPS = 1e-6
HALF = 0.5
N_MOD = 9
D_FF = 2816
A_HEADS, A_DK, A_DV, A_CHUNK = 4, 32, 64, 64
B_HEADS, B_Q_RANK, B_KV_RANK, B_NOPE, B_ROPE, B_DV = 6, 256, 128, 64, 32, 64
B_DQK = B_NOPE + B_ROPE
C_HEADS, C_KV_HEADS, C_DH, WINDOW = 6, 2, 64, 128
C_GROUP = C_HEADS // C_KV_HEADS

NL = B * T
NCX = B * CTX
N = NL + NCX

LANES = 128
MOD_ROWS = 8
VMEM_LIMIT = 56 * 1024 * 1024

TM = 256
TPB = T // TM
FF_CHUNK = 1408
ADA_TN = 1152
MLSTM_R = 256
MLA_TQ = 256
GQA_TQ = 128

COL_AQ, COL_AK, COL_AV, COL_AO, COL_AG = 0, 128, 256, 512, 768
COL_BCQ, COL_BCKV, COL_BKR = 896, 1152, 1280
COL_CQ, COL_CK, COL_CV = 1408, 2176, 2432
WP = 2560
PA_W = 1280


def _sigmoid(x):
    return 1.0 / (1.0 + jnp.exp(-x))


def _log_sigmoid(x):
    return jnp.minimum(x, 0.0) - jnp.log(1.0 + jnp.exp(-jnp.abs(x)))


def _rms(x, g):
    ms = jnp.mean(x * x, axis=-1, keepdims=True)
    return x * lax.rsqrt(ms + EPS) * g


def _dot(a, b):
    return jnp.dot(a, b, preferred_element_type=f32)


def _dot_nt(a, b):
    return lax.dot_general(a, b, (((1,), (1,)), ((), ())), preferred_element_type=f32)


def _ada_kernel(c_ref, w_ref, b_ref, o_ref):
    c = c_ref[...]
    s = c * _sigmoid(c)
    o_ref[0] = jnp.dot(s, w_ref[0], preferred_element_type=f32, precision=lax.Precision.HIGHEST) + b_ref[0]


def _ada(cc, ada_w, ada_b):
    nt = (N_MOD * D) // ADA_TN
    return pl.pallas_call(
        _ada_kernel,
        grid=(DEPTH, nt),
        in_specs=[
            pl.BlockSpec((MOD_ROWS, D), lambda l, j: (0, 0)),
            pl.BlockSpec((1, D, ADA_TN), lambda l, j: (l, 0, j)),
            pl.BlockSpec((1, 1, ADA_TN), lambda l, j: (l, 0, j)),
        ],
        out_specs=pl.BlockSpec((1, MOD_ROWS, ADA_TN), lambda l, j: (l, 0, j)),
        out_shape=jax.ShapeDtypeStruct((DEPTH, MOD_ROWS, N_MOD * D), f32),
        compiler_params=pltpu.CompilerParams(dimension_semantics=("arbitrary", "arbitrary"),
                                             vmem_limit_bytes=VMEM_LIMIT),
        name="ada_mod",
    )(cc, ada_w, ada_b.reshape(DEPTH, 1, N_MOD * D))


def _ffn(x, g, shift, scale, gate, wig_ref, wiu_ref, wo_ref):
    h = (_rms(x, g) * (1.0 + scale) + shift).astype(bf16)
    acc = None
    for c in range(D_FF // FF_CHUNK):
        sl = slice(c * FF_CHUNK, (c + 1) * FF_CHUNK)
        gt = _dot(h, wig_ref[:, sl])
        up = _dot(h, wiu_ref[:, sl])
        a = (gt * _sigmoid(gt) * up).astype(bf16)
        part = _dot(a, wo_ref[sl, :])
        acc = part if acc is None else acc + part
    return x + HALF * gate * acc


def _head_norm_rope(x, gain, n_real, cos, sin_next, sin_prev, shift):
    ss = jnp.sum(x * x, axis=-1, keepdims=True) * (1.0 / n_real)
    y = x * lax.rsqrt(ss + EPS) * gain
    return y * cos + pltpu.roll(y, LANES - shift, 1) * sin_next + pltpu.roll(y, shift, 1) * sin_prev


def _ffn_inproj_kernel(x_ref, mod_ref, ng_ref, wig_ref, wiu_ref, wo_ref, win_ref, bias_ref, tab_ref,
                       cqn_ref, ckvn_ref, wuq_ref, wukv_ref, hg_ref,
                       x1_ref, pa_ref, qb_ref, kb_ref, vb_ref, qc_ref, kc_ref, vc_ref):
    x = x_ref[...]
    mod = mod_ref[0]
    x1 = _ffn(x, ng_ref[0:1, :], mod[0:1, :], mod[1:2, :], mod[2:3, :], wig_ref, wiu_ref, wo_ref)
    x1_ref[...] = x1
    h = (_rms(x1, ng_ref[1:2, :]) * (1.0 + mod[4:5, :]) + mod[3:4, :]).astype(bf16)
    p = _dot(h, win_ref[...]) + bias_ref[...]

    pa_ref[:, COL_AQ:COL_AK] = p[:, COL_AQ:COL_AK] * (A_DK ** -0.5)
    pa_ref[:, COL_AK:COL_AG] = p[:, COL_AK:COL_AG]
    graw = p[:, COL_AG:COL_AG + LANES]
    lane = lax.broadcasted_iota(jnp.int32, (1, LANES), 1)
    for kk in range(4):
        gk = graw if kk == 0 else pltpu.roll(graw, LANES - A_HEADS * kk, 1)
        if kk % 2 == 1:
            gk = _log_sigmoid(gk)
        pa_ref[:, COL_AG + LANES * kk:COL_AG + LANES * (kk + 1)] = jnp.where(lane < A_HEADS, gk, 0.0)

    tab = tab_ref[...]
    tb = [tab[:, LANES * i:LANES * (i + 1)] for i in range(6)]
    hg = hg_ref[...]

    cq = _rms(p[:, COL_BCQ:COL_BCKV], cqn_ref[...]).astype(bf16)
    ckv = _rms(p[:, COL_BCKV:COL_BKR], ckvn_ref[...]).astype(bf16)
    kr = p[:, COL_BKR:COL_CQ]
    q = _dot(cq, wuq_ref[...])
    kv = _dot(ckv, wukv_ref[...])
    for hh in range(B_HEADS):
        sl = slice(LANES * hh, LANES * (hh + 1))
        qh = _head_norm_rope(q[:, sl], hg[0:1, :], B_DQK, tb[0], tb[1], tb[2], B_ROPE // 4)
        qb_ref[:, sl] = (qh * (B_DQK ** -0.5)).astype(bf16)
        kh = _head_norm_rope(kv[:, sl] + kr, hg[1:2, :], B_DQK, tb[0], tb[1], tb[2], B_ROPE // 4)
        kb_ref[:, sl] = kh.astype(bf16)
    vb_ref[...] = kv[:, B_HEADS * LANES:].astype(bf16)

    for hh in range(C_HEADS):
        sl = slice(LANES * hh, LANES * (hh + 1))
        qh = _head_norm_rope(p[:, COL_CQ + LANES * hh:COL_CQ + LANES * (hh + 1)], hg[2:3, :], C_DH,
                             tb[3], tb[4], tb[5], C_DH // 4)
        qc_ref[:, sl] = (qh * (C_DH ** -0.5)).astype(bf16)
    for hh in range(C_KV_HEADS):
        sl = slice(LANES * hh, LANES * (hh + 1))
        kh = _head_norm_rope(p[:, COL_CK + LANES * hh:COL_CK + LANES * (hh + 1)], hg[3:4, :], C_DH,
                             tb[3], tb[4], tb[5], C_DH // 4)
        kc_ref[:, sl] = kh.astype(bf16)
    vc_ref[...] = p[:, COL_CV:COL_CV + LANES].astype(bf16)


def _resident(shape):
    nd = len(shape)
    return pl.BlockSpec(shape, lambda i: (0,) * nd, pipeline_mode=pl.Buffered(1))


def _ffn_inproj(x, mod, ng, wig, wiu, wo, win, bias, tab, cqn, ckvn, wuq, wukv, hg):
    nt = N // TM
    row = lambda w: pl.BlockSpec((TM, w), lambda i: (i, 0))
    out_w = [(D, f32), (PA_W, f32), (B_HEADS * LANES, bf16), (B_HEADS * LANES, bf16), (B_HEADS * B_DV, bf16),
             (C_HEADS * LANES, bf16), (C_KV_HEADS * LANES, bf16), (C_KV_HEADS * C_DH, bf16)]
    return pl.pallas_call(
        _ffn_inproj_kernel,
        grid=(nt,),
        in_specs=[
            row(D),
            pl.BlockSpec((1, N_MOD, D), lambda i: (i // TPB, 0, 0)),
            _resident(ng.shape), _resident(wig.shape), _resident(wiu.shape), _resident(wo.shape),
            _resident(win.shape), _resident(bias.shape),
            pl.BlockSpec((TM, 6 * LANES), lambda i: (jnp.where(i < NL // TM, i % TPB, TPB), 0)),
            _resident(cqn.shape), _resident(ckvn.shape), _resident(wuq.shape), _resident(wukv.shape),
            _resident(hg.shape),
        ],
        out_specs=[row(w) for w, _ in out_w],
        out_shape=[jax.ShapeDtypeStruct((N, w), dt) for w, dt in out_w],
        compiler_params=pltpu.CompilerParams(dimension_semantics=("arbitrary",), vmem_limit_bytes=VMEM_LIMIT),
        name="ffn1_inproj",
    )(x, mod, ng, wig, wiu, wo, win, bias, tab, cqn, ckvn, wuq, wukv, hg)


def _outproj_ffn_kernel(x_ref, mod_ref, ng_ref, h_ref, o_ref, yb_ref, yc_ref, on_ref, wout_ref,
                        wig_ref, wiu_ref, wo_ref, out_ref):
    x = x_ref[...]
    mod = mod_ref[0]
    hs = h_ref[0] + h_ref[1]
    sq = hs * hs
    head = lax.broadcasted_iota(jnp.int32, (1, A_HEADS * A_DV), 1) // A_DV
    ms = jnp.zeros_like(hs)
    for hh in range(A_HEADS):
        sel = head == hh
        ssh = jnp.sum(jnp.where(sel, sq, 0.0), axis=-1, keepdims=True) * (1.0 / A_DV)
        ms = jnp.where(sel, ssh, ms)
    ya = _sigmoid(o_ref[...]) * (hs * lax.rsqrt(ms + EPS) * on_ref[...])
    y = jnp.concatenate([ya.astype(bf16), yb_ref[...], yc_ref[...]], axis=-1)
    x2 = x + mod[5:6, :] * _dot(y, wout_ref[...])
    out_ref[...] = _ffn(x2, ng_ref[2:3, :], mod[6:7, :], mod[7:8, :], mod[8:9, :], wig_ref, wiu_ref, wo_ref)


def _outproj_ffn(x1, mod, ng, hA, pa, yb, yc, onorm, wout, wig, wiu, wo, rows):
    nt = rows // TM
    return pl.pallas_call(
        _outproj_ffn_kernel,
        grid=(nt,),
        in_specs=[
            pl.BlockSpec((TM, D), lambda i: (i, 0)),
            pl.BlockSpec((1, N_MOD, D), lambda i: (i // TPB, 0, 0)),
            _resident(ng.shape),
            pl.BlockSpec((2, TM, A_HEADS * A_DV), lambda i: (0, i, 0)),
            pl.BlockSpec((TM, A_HEADS * A_DV), lambda i: (i, COL_AO // (A_HEADS * A_DV))),
            pl.BlockSpec((TM, B_HEADS * B_DV), lambda i: (i, 0)),
            pl.BlockSpec((TM, C_HEADS * C_DH), lambda i: (i, 0)),
            _resident(onorm.shape), _resident(wout.shape),
            _resident(wig.shape), _resident(wiu.shape), _resident(wo.shape),
        ],
        out_specs=pl.BlockSpec((TM, D), lambda i: (i, 0)),
        out_shape=jax.ShapeDtypeStruct((rows, D), f32),
        compiler_params=pltpu.CompilerParams(dimension_semantics=("arbitrary",), vmem_limit_bytes=VMEM_LIMIT),
        name="outproj_ffn2",
    )(x1, mod, ng, hA, pa, yb, yc, onorm, wout, wig, wiu, wo)


def _mlstm_kernel(q_ref, k_ref, v_ref, ig_ref, lf_ref, gt_ref, h_ref, c_ref, m_ref):
    d = pl.program_id(1)
    j = pl.program_id(2)

    @pl.when(j == 0)
    def _():
        c_ref[...] = jnp.zeros_like(c_ref)
        m_ref[...] = jnp.zeros_like(m_ref)

    L = A_CHUNK
    rev = d == 1
    ti = lax.broadcasted_iota(jnp.int32, (L, L), 0)
    si = lax.broadcasted_iota(jnp.int32, (L, L), 1)
    dd = (si - ti) * jnp.where(rev, -1, 1)
    attend = dd <= 0
    attend_f = attend.astype(f32)
    cum_rows = (dd >= 0).astype(f32)
    lane = lax.broadcasted_iota(jnp.int32, (1, LANES), 1)
    sub = lax.broadcasted_iota(jnp.int32, (LANES, 1), 0)
    n_chunks = MLSTM_R // L

    def chunk(ci, carry):
        cc = jnp.where(rev, n_chunks - 1 - ci, ci)
        r0 = pl.multiple_of(cc * L, L)
        q = q_ref[pl.ds(r0, L), :]
        k = k_ref[pl.ds(r0, L), :]
        v = v_ref[pl.ds(r0, L), :]
        ig = ig_ref[pl.ds(r0, L), :]
        lf = lf_ref[pl.ds(r0, L), :]
        gt = gt_ref[cc]
        hp = lax.Precision.HIGHEST
        b_rows = jnp.dot(gt, cum_rows, preferred_element_type=f32, precision=hp)
        b_col = jnp.dot(attend_f, lf, preferred_element_type=f32, precision=hp)
        b_last = jnp.sum(lf, axis=0, keepdims=True)
        m_old = m_ref[0:1, :]
        g_col = b_last - b_col + ig
        m_new = jnp.maximum(b_last + m_old, jnp.max(g_col, axis=0, keepdims=True))
        decay = jnp.exp(b_last + m_old - m_new)
        wg = jnp.exp(g_col - m_new)
        qb = q.astype(bf16)
        kb = k.astype(bf16)
        outs = []
        for hh in range(A_HEADS):
            in_head = (lane >= hh * A_DK) & (lane < (hh + 1) * A_DK)
            bc = b_col[:, hh:hh + 1]
            dlog = jnp.where(attend, bc - b_rows[A_HEADS + hh:A_HEADS + hh + 1, :] + gt[hh:hh + 1, :], -jnp.inf)
            inter = bc + m_old[:, hh:hh + 1]
            mt = jnp.maximum(inter, jnp.max(dlog, axis=-1, keepdims=True))
            w = jnp.exp(dlog - mt)
            a_inter = jnp.exp(inter - mt)
            s = _dot_nt(jnp.where(in_head, q, 0.0).astype(bf16), kb) * w
            vpair = v[:, LANES * (hh // 2):LANES * (hh // 2 + 1)]
            if hh % 2 == 0:
                vext = jnp.where(lane < A_DV, vpair, jnp.where(lane == A_DV, 1.0, 0.0))
                den_lane = A_DV
            else:
                vext = jnp.where(lane >= A_DV, vpair, jnp.where(lane == 0, 1.0, 0.0))
                den_lane = 0
            vext = vext.astype(bf16)
            c_h = c_ref[hh]
            out = a_inter * _dot(qb, c_h.astype(bf16)) + _dot(s.astype(bf16), vext)
            den = out[:, den_lane:den_lane + 1]
            outs.append(out / jnp.maximum(jnp.abs(den), jnp.exp(-mt)))
            kw_t = (wg[:, hh:hh + 1] * k).T.astype(bf16)
            kv = _dot(kw_t, vext)
            row_in_head = (sub >= hh * A_DK) & (sub < (hh + 1) * A_DK)
            c_ref[hh] = decay[:, hh:hh + 1] * c_h + jnp.where(row_in_head, kv, 0.0)
        m_ref[0:1, :] = m_new
        h_ref[0, pl.ds(r0, L), 0:LANES] = jnp.where(lane < A_DV, outs[0], outs[1])
        h_ref[0, pl.ds(r0, L), LANES:2 * LANES] = jnp.where(lane < A_DV, outs[2], outs[3])
        return carry

    lax.fori_loop(0, n_chunks, chunk, 0)


def _mlstm(pa, gt):
    nb = T // MLSTM_R
    assert CTX == MLSTM_R

    def rb(b, d, j):
        jj = j - 1
        return jnp.where(j == 0, NL // MLSTM_R + b, b * nb + jnp.where(d == 0, jj, nb - 1 - jj))

    def col(w, c):
        return pl.BlockSpec((MLSTM_R, w), lambda b, d, j: (rb(b, d, j), c))

    return pl.pallas_call(
        _mlstm_kernel,
        grid=(B, 2, nb + 1),
        in_specs=[
            col(LANES, COL_AQ // LANES), col(LANES, COL_AK // LANES), col(2 * LANES, COL_AV // (2 * LANES)),
            pl.BlockSpec((MLSTM_R, LANES), lambda b, d, j: (rb(b, d, j), COL_AG // LANES + 2 * d)),
            pl.BlockSpec((MLSTM_R, LANES), lambda b, d, j: (rb(b, d, j), COL_AG // LANES + 2 * d + 1)),
            pl.BlockSpec((MLSTM_R // A_CHUNK, 2 * A_HEADS, A_CHUNK), lambda b, d, j: (rb(b, d, j), d, 0)),
        ],
        out_specs=pl.BlockSpec((1, MLSTM_R, A_HEADS * A_DV), lambda b, d, j: (d, rb(b, d, j), 0)),
        out_shape=jax.ShapeDtypeStruct((2, N, A_HEADS * A_DV), f32),
        scratch_shapes=[pltpu.VMEM((A_HEADS, LANES, LANES), f32), pltpu.VMEM((8, LANES), f32)],
        compiler_params=pltpu.CompilerParams(dimension_semantics=("arbitrary", "arbitrary", "arbitrary"),
                                             vmem_limit_bytes=VMEM_LIMIT),
        name="mlstm",
    )(pa, pa, pa, pa, pa, gt)


def _mla_kernel(*refs, latent):
    if latent:
        q_ref, kc_ref, vc_ref, kl_ref, vl_ref, o_ref = refs
    else:
        q_ref, kc_ref, vc_ref, _, o_ref = refs
    q = q_ref[...]
    outs = []
    for hh in range(2):
        sl = slice(LANES * hh, LANES * (hh + 1))
        qh = q[:, sl]
        s_c = _dot_nt(qh, kc_ref[:, sl])
        m = jnp.max(s_c, axis=-1, keepdims=True)
        if latent:
            s_l = _dot_nt(qh, kl_ref[:, sl])
            m = jnp.maximum(m, jnp.max(s_l, axis=-1, keepdims=True))
        p_c = jnp.exp(s_c - m)
        l = jnp.sum(p_c, axis=-1, keepdims=True)
        o = _dot(p_c.astype(bf16), vc_ref[...])
        if latent:
            p_l = jnp.exp(s_l - m)
            l = l + jnp.sum(p_l, axis=-1, keepdims=True)
            o = o + _dot(p_l.astype(bf16), vl_ref[...])
        outs.append(o / l)
    lane = lax.broadcasted_iota(jnp.int32, (1, LANES), 1)
    o_ref[...] = jnp.where(lane < B_DV, outs[0], outs[1]).astype(o_ref.dtype)


def _mla(qb, kb, vb, prev=None):
    latent = prev is None
    npair = B_HEADS // 2
    ctx_blk = NL // CTX
    kv_specs = [
        pl.BlockSpec((CTX, 2 * LANES), lambda b, p, i: (ctx_blk + b, p)),
        pl.BlockSpec((CTX, LANES), lambda b, p, i: (ctx_blk + b, p)),
    ]
    if latent:
        tq = MLA_TQ
        nq = T // tq
        qmap = lambda b, p, i: (b * nq + i, p)
        kv_specs += [
            pl.BlockSpec((T, 2 * LANES), lambda b, p, i: (b, p)),
            pl.BlockSpec((T, LANES), lambda b, p, i: (b, p)),
        ]
        args = (qb, kb, vb, kb, vb)
        aliases = {}
    else:
        tq = CTX
        nq = 1
        qmap = lambda b, p, i: (ctx_blk + b, p)
        kv_specs += [pl.BlockSpec(memory_space=pl.ANY)]
        args = (qb, kb, vb, prev)
        aliases = {3: 0}
    return pl.pallas_call(
        functools.partial(_mla_kernel, latent=latent),
        grid=(B, npair, nq),
        in_specs=[pl.BlockSpec((tq, 2 * LANES), qmap)] + kv_specs,
        out_specs=pl.BlockSpec((tq, LANES), qmap),
        out_shape=jax.ShapeDtypeStruct((N, B_HEADS * B_DV), bf16),
        input_output_aliases=aliases,
        compiler_params=pltpu.CompilerParams(dimension_semantics=("arbitrary", "arbitrary", "arbitrary"),
                                             vmem_limit_bytes=VMEM_LIMIT),
        name="mla_latent" if latent else "mla_context",
    )(*args)


def _gqa_kernel(sink_ref, *refs, latent):
    if latent:
        q_ref, kc_ref, vc_ref, kl_ref, vl_ref, o_ref = refs
    else:
        q_ref, kc_ref, vc_ref, _, o_ref = refs
    q = q_ref[...]
    tq = q.shape[0]
    vc = vc_ref[...]
    if latent:
        n = pl.program_id(1)
        band = 3 * GQA_TQ
        start = pl.multiple_of(jnp.clip((n - 1) * GQA_TQ, 0, T - band), GQA_TQ)
        kband = kl_ref[pl.ds(start, band), :]
        vband = vl_ref[pl.ds(start, band), :]
        qpos1 = n * GQA_TQ + lax.broadcasted_iota(jnp.int32, (tq, 1), 0)
        qpos = jnp.concatenate([qpos1] * C_GROUP, axis=0)
        kpos = start + lax.broadcasted_iota(jnp.int32, (1, band), 1)
        valid = jnp.abs(qpos - kpos) <= WINDOW
    heads = []
    for kvh in range(C_KV_HEADS):
        sl = slice(LANES * kvh, LANES * (kvh + 1))
        qs = jnp.concatenate([q[:, LANES * (C_GROUP * kvh + g):LANES * (C_GROUP * kvh + g + 1)]
                              for g in range(C_GROUP)], axis=0)
        sink = jnp.concatenate([jnp.full((tq, 1), sink_ref[C_GROUP * kvh + g], f32) for g in range(C_GROUP)],
                               axis=0)
        s_c = _dot_nt(qs, kc_ref[:, sl])
        m = jnp.maximum(sink, jnp.max(s_c, axis=-1, keepdims=True))
        if latent:
            s_b = jnp.where(valid, _dot_nt(qs, kband[:, sl]), -jnp.inf)
            m = jnp.maximum(m, jnp.max(s_b, axis=-1, keepdims=True))
        p_c = jnp.exp(s_c - m)
        l = jnp.exp(sink - m) + jnp.sum(p_c, axis=-1, keepdims=True)
        o = _dot(p_c.astype(bf16), vc)
        if latent:
            p_b = jnp.exp(s_b - m)
            l = l + jnp.sum(p_b, axis=-1, keepdims=True)
            o = o + _dot(p_b.astype(bf16), vband)
        o = o / l
        heads += [o[g * tq:(g + 1) * tq, :] for g in range(C_GROUP)]
    lane = lax.broadcasted_iota(jnp.int32, (1, LANES), 1)
    lo = lane < C_DH
    o_ref[:, 0:LANES] = jnp.where(lo, heads[0], pltpu.roll(heads[1], C_DH, 1)).astype(o_ref.dtype)
    o_ref[:, LANES:2 * LANES] = jnp.where(lo, heads[2], heads[3]).astype(o_ref.dtype)
    o_ref[:, 2 * LANES:3 * LANES] = jnp.where(lo, pltpu.roll(heads[4], C_DH, 1), heads[5]).astype(o_ref.dtype)


def _gqa(sink, qc, kc, vc, prev=None):
    latent = prev is None
    ctx_blk = NL // CTX
    kv_specs = [
        pl.BlockSpec((CTX, C_KV_HEADS * LANES), lambda b, i: (ctx_blk + b, 0)),
        pl.BlockSpec((CTX, LANES), lambda b, i: (ctx_blk + b, 0)),
    ]
    if latent:
        tq = GQA_TQ
        nq = T // tq
        qmap = lambda b, i: (b * nq + i, 0)
        kv_specs += [
            pl.BlockSpec((T, C_KV_HEADS * LANES), lambda b, i: (b, 0)),
            pl.BlockSpec((T, LANES), lambda b, i: (b, 0)),
        ]
        args = (sink, qc, kc, vc, kc, vc)
        aliases = {}
    else:
        tq = CTX
        nq = 1
        qmap = lambda b, i: (ctx_blk + b, 0)
        kv_specs += [pl.BlockSpec(memory_space=pl.ANY)]
        args = (sink, qc, kc, vc, prev)
        aliases = {4: 0}
    return pl.pallas_call(
        functools.partial(_gqa_kernel, latent=latent),
        grid=(B, nq),
        in_specs=[pl.BlockSpec(memory_space=pltpu.SMEM), pl.BlockSpec((tq, C_HEADS * LANES), qmap)] + kv_specs,
        out_specs=pl.BlockSpec((tq, C_HEADS * C_DH), qmap),
        out_shape=jax.ShapeDtypeStruct((N, C_HEADS * C_DH), bf16),
        input_output_aliases=aliases,
        compiler_params=pltpu.CompilerParams(dimension_semantics=("arbitrary", "arbitrary"),
                                             vmem_limit_bytes=VMEM_LIMIT),
        name="gqa_latent" if latent else "gqa_context",
    )(*args)


def _pad_cols(w, width):
    return jnp.pad(w, ((0, 0), (0, width - w.shape[1])))


def _pad_heads(w, heads, dh):
    r = w.shape[0]
    return jnp.pad(w.reshape(r, heads, dh), ((0, 0), (0, 0), (0, LANES - dh))).reshape(r, heads * LANES)


def _arrange_w_in(w):
    o = np.cumsum((0, 128, 128, 256, 256, 16, 256, 128, 32, 384, 128, 128))
    part = lambda i: w[:, int(o[i]):int(o[i + 1])]
    kr = jnp.pad(part(7), ((0, 0), (B_NOPE, LANES - B_NOPE - B_ROPE)))
    return jnp.concatenate([
        part(0), part(1), part(2), part(3), _pad_cols(part(4), LANES),
        part(5), part(6), kr,
        _pad_heads(part(8), C_HEADS, C_DH), _pad_heads(part(9), C_KV_HEADS, C_DH), part(10),
    ], axis=1)


def _rope_tables():
    t = jnp.arange(T, dtype=jnp.int32)
    row = (t // GRID_W).astype(f32)[:, None]
    col = (t % GRID_W).astype(f32)[:, None]

    def tables(first_lane, half):
        inv = ROPE_BASE ** (-jnp.arange(half, dtype=f32) / half)
        ang_r = row * inv
        ang_c = col * inv
        cos = jnp.ones((T, LANES), f32)
        s_next = jnp.zeros((T, LANES), f32)
        s_prev = jnp.zeros((T, LANES), f32)
        for base, ang in ((first_lane, ang_r), (first_lane + 2 * half, ang_c)):
            c, s = jnp.cos(ang), jnp.sin(ang)
            cos = cos.at[:, base:base + half].set(c).at[:, base + half:base + 2 * half].set(c)
            s_next = s_next.at[:, base:base + half].set(-s)
            s_prev = s_prev.at[:, base + half:base + 2 * half].set(s)
        return [cos, s_next, s_prev]

    tab = jnp.concatenate(tables(B_NOPE, B_ROPE // 4) + tables(0, C_DH // 4), axis=1)
    ident = jnp.concatenate([jnp.ones((TM, LANES), f32), jnp.zeros((TM, 2 * LANES), f32)] * 2, axis=1)
    return jnp.concatenate([tab, ident], axis=0)


def kernel(x, c, ctx, c_ctx, ada_w, ada_b, norm_g, ffn1_wi, ffn1_wo, ffn2_wi, ffn2_wo, w_in, w_out,
           mlstm_gate_b, mlstm_out_norm, mla_cq_norm, mla_ckv_norm, mla_w_uq, mla_w_ukv, mla_q_norm, mla_k_norm,
           gqa_q_norm, gqa_k_norm, gqa_sink):
    xs = jnp.concatenate([x.reshape(NL, D), ctx.reshape(NCX, D)], axis=0)
    cc = jnp.concatenate([c, c_ctx[None, :], jnp.zeros((MOD_ROWS - B - 1, D), f32)], axis=0)
    mod_all = _ada(cc, ada_w, ada_b).reshape(DEPTH, MOD_ROWS, N_MOD, D)
    tab = _rope_tables()

    for l in range(DEPTH):
        need_ctx = l < DEPTH - 1
        mod = mod_all[l]
        ng = norm_g[l]
        wig1, wiu1 = ffn1_wi[l, :, :D_FF].astype(bf16), ffn1_wi[l, :, D_FF:].astype(bf16)
        wig2, wiu2 = ffn2_wi[l, :, :D_FF].astype(bf16), ffn2_wi[l, :, D_FF:].astype(bf16)
        wo1, wo2 = ffn1_wo[l].astype(bf16), ffn2_wo[l].astype(bf16)
        win = _arrange_w_in(w_in[l]).astype(bf16)
        bias = jnp.zeros((1, WP), f32).at[0, COL_AG:COL_AG + 4 * A_HEADS].set(mlstm_gate_b[l])
        wuq = _pad_heads(mla_w_uq[l], B_HEADS, B_DQK).astype(bf16)
        ukv = mla_w_ukv[l].reshape(B_KV_RANK, B_HEADS, B_NOPE + B_DV)
        wukv = jnp.concatenate([_pad_heads(ukv[:, :, :B_NOPE].reshape(B_KV_RANK, -1), B_HEADS, B_NOPE),
                                ukv[:, :, B_NOPE:].reshape(B_KV_RANK, -1)], axis=1).astype(bf16)
        hg = jnp.stack([_pad_cols(mla_q_norm[l][None], LANES)[0], _pad_cols(mla_k_norm[l][None], LANES)[0],
                        _pad_cols(gqa_q_norm[l][None], LANES)[0], _pad_cols(gqa_k_norm[l][None], LANES)[0]])

        x1, pa, qb, kb, vb, qc, kc, vc = _ffn_inproj(
            xs, mod, ng, wig1, wiu1, wo1, win, bias, tab,
            mla_cq_norm[l][None], mla_ckv_norm[l][None], wuq, wukv, hg)

        gates = jnp.concatenate([pa[:, COL_AG + LANES * kk:COL_AG + LANES * kk + A_HEADS] for kk in range(4)], axis=1)
        gt = gates.reshape(N // A_CHUNK, A_CHUNK, 4 * A_HEADS).transpose(0, 2, 1)
        hA = _mlstm(pa, gt)

        yb = _mla(qb, kb, vb)
        yc = _gqa(gqa_sink[l], qc, kc, vc)
        if need_ctx:
            yb = _mla(qb, kb, vb, prev=yb)
            yc = _gqa(gqa_sink[l], qc, kc, vc, prev=yc)

        rows = N if need_ctx else NL
        xs = _outproj_ffn(x1, mod, ng, hA, pa, yb, yc, mlstm_out_norm[l][None], w_out[l].astype(bf16),
                          wig2, wiu2, wo2, rows)
    return xs.reshape(B, T, D)
```

```python
import functools
import math

import jax
import jax.numpy as jnp
import numpy as np
from jax import lax
from jax.experimental import pallas as pl
from jax.experimental.pallas import tpu as pltpu

f32 = jnp.float32
bf16 = jnp.bfloat16

D = 1024
B = 4
T = 4096
CTX = 256
DEPTH = 2
GRID_W = 64
ROPE_BASE = 10000.0
EPS = 1e-6
HALF = 0.5
N_MOD = 9
D_FF = 2816
A_HEADS, A_DK, A_DV, A_CHUNK = 4, 32, 64, 64
B_HEADS, B_Q_RANK, B_KV_RANK, B_NOPE, B_ROPE, B_DV = 6, 256, 128, 64, 32, 64
B_DQK = B_NOPE + B_ROPE
C_HEADS, C_KV_HEADS, C_DH, WINDOW = 6, 2, 64, 128
C_GROUP = C_HEADS // C_KV_HEADS

NL = B * T
NCX = B * CTX
N = NL + NCX

LANES = 128
MOD_ROWS = 8
VMEM_LIMIT = 56 * 1024 * 1024
LOG2E = math.log2(math.e)

TM_FFN = 512
FF_CHUNK = 1408
TM_IN = 512
ADA_TN = 1152
MLSTM_R = 256
MLA_TQ = 256
MLA_KC = 512
GQA_TQ = 256
GQA_BAND = GQA_TQ + 2 * WINDOW

COL_AQ, COL_AK, COL_AV, COL_AO, COL_AG = 0, 128, 256, 512, 768
COL_BCQ, COL_BCKV, COL_BKR = 896, 1152, 1280
COL_CQ, COL_CK, COL_CV = 1408, 2176, 2432
WP = 2560
PA_W = 1280


def _sigmoid(x):
    return 1.0 / (1.0 + jnp.exp(-x))


def _log_sigmoid(x):
    return jnp.minimum(x, 0.0) - jnp.log(1.0 + jnp.exp(-jnp.abs(x)))


def _rms(x, g):
    ms = jnp.mean(x * x, axis=-1, keepdims=True)
    return x * lax.rsqrt(ms + EPS) * g


def _dot(a, b):
    return jnp.dot(a, b, preferred_element_type=f32)


def _dot_nt(a, b):
    return lax.dot_general(a, b, (((1,), (1,)), ((), ())), preferred_element_type=f32)


def _resident(shape):
    nd = len(shape)
    return pl.BlockSpec(shape, lambda *_: (0,) * nd, pipeline_mode=pl.Buffered(1))


def _params(n_axes):
    return pltpu.CompilerParams(dimension_semantics=("arbitrary",) * n_axes, vmem_limit_bytes=VMEM_LIMIT)


def _ada_kernel(c_ref, w_ref, b_ref, o_ref):
    c = c_ref[...]
    s = c * _sigmoid(c)
    o_ref[0] = jnp.dot(s, w_ref[0], preferred_element_type=f32, precision=lax.Precision.HIGHEST) + b_ref[0]


def _ada(cc, ada_w, ada_b):
    nt = (N_MOD * D) // ADA_TN
    return pl.pallas_call(
        _ada_kernel,
        grid=(DEPTH, nt),
        in_specs=[
            pl.BlockSpec((MOD_ROWS, D), lambda l, j: (0, 0)),
            pl.BlockSpec((1, D, ADA_TN), lambda l, j: (l, 0, j)),
            pl.BlockSpec((1, 1, ADA_TN), lambda l, j: (l, 0, j)),
        ],
        out_specs=pl.BlockSpec((1, MOD_ROWS, ADA_TN), lambda l, j: (l, 0, j)),
        out_shape=jax.ShapeDtypeStruct((DEPTH, MOD_ROWS, N_MOD * D), f32),
        compiler_params=_params(2),
        name="ada_mod",
    )(cc, ada_w, ada_b.reshape(DEPTH, 1, N_MOD * D))


def _ffn(x, g, shift, scale, gate, wig_ref, wiu_ref, wo_ref):
    h = (_rms(x, g) * (1.0 + scale) + shift).astype(bf16)
    acc = None
    for c in range(D_FF // FF_CHUNK):
        sl = slice(c * FF_CHUNK, (c + 1) * FF_CHUNK)
        gt = _dot(h, wig_ref[:, sl])
        up = _dot(h, wiu_ref[:, sl])
        a = (gt * _sigmoid(gt) * up).astype(bf16)
        part = _dot(a, wo_ref[sl, :])
        acc = part if acc is None else acc + part
    return x + HALF * gate * acc


def _ffn1_kernel(*refs, split_input):
    if split_input:
        xl_ref, xc_ref, mod_ref, ng_ref, wig_ref, wiu_ref, wo_ref, x1_ref, h_ref = refs
        x = jnp.where(pl.program_id(0) < NL // TM_FFN, xl_ref[...], xc_ref[...])
    else:
        x_ref, mod_ref, ng_ref, wig_ref, wiu_ref, wo_ref, x1_ref, h_ref = refs
        x = x_ref[...]
    mod = mod_ref[0]
    x1 = _ffn(x, ng_ref[0:1, :], mod[0:1, :], mod[1:2, :], mod[2:3, :], wig_ref, wiu_ref, wo_ref)
    x1_ref[...] = x1
    h_ref[...] = (_rms(x1, ng_ref[1:2, :]) * (1.0 + mod[4:5, :]) + mod[3:4, :]).astype(bf16)


def _ffn1(xs, mod, ng, wig, wiu, wo):
    split_input = isinstance(xs, tuple)
    tpb = T // TM_FFN
    nlt = NL // TM_FFN
    if split_input:
        assert NCX % TM_FFN == 0
        x_specs = [pl.BlockSpec((TM_FFN, D), lambda i: (jnp.minimum(i, nlt - 1), 0)),
                   pl.BlockSpec((TM_FFN, D), lambda i: (jnp.maximum(i - nlt, 0), 0))]
    else:
        xs = (xs,)
        x_specs = [pl.BlockSpec((TM_FFN, D), lambda i: (i, 0))]
    return pl.pallas_call(
        functools.partial(_ffn1_kernel, split_input=split_input),
        grid=(N // TM_FFN,),
        in_specs=x_specs + [
            pl.BlockSpec((1, N_MOD, D), lambda i: (i // tpb, 0, 0)),
            _resident(ng.shape), _resident(wig.shape), _resident(wiu.shape), _resident(wo.shape),
        ],
        out_specs=[pl.BlockSpec((TM_FFN, D), lambda i: (i, 0))] * 2,
        out_shape=[jax.ShapeDtypeStruct((N, D), f32), jax.ShapeDtypeStruct((N, D), bf16)],
        compiler_params=_params(1),
        name="ffn1",
    )(*xs, mod, ng, wig, wiu, wo)


def _outproj_ffn_kernel(x_ref, mod_ref, ng_ref, hf_ref, hb_ref, o_ref, yb_ref, yc_ref, on_ref, wout_ref,
                        wig_ref, wiu_ref, wo_ref, out_ref):
    x = x_ref[...]
    mod = mod_ref[0]
    hs = hf_ref[...] + hb_ref[...]
    sq = hs * hs
    head = lax.broadcasted_iota(jnp.int32, (1, A_HEADS * A_DV), 1) // A_DV
    ms = jnp.zeros_like(hs)
    for hh in range(A_HEADS):
        sel = head == hh
        ssh = jnp.sum(jnp.where(sel, sq, 0.0), axis=-1, keepdims=True) * (1.0 / A_DV)
        ms = jnp.where(sel, ssh, ms)
    ya = _sigmoid(o_ref[...]) * (hs * lax.rsqrt(ms + EPS) * on_ref[...])
    y = jnp.concatenate([ya.astype(bf16), yb_ref[...], yc_ref[...]], axis=-1)
    x2 = x + mod[5:6, :] * _dot(y, wout_ref[...])
    out_ref[...] = _ffn(x2, ng_ref[2:3, :], mod[6:7, :], mod[7:8, :], mod[8:9, :], wig_ref, wiu_ref, wo_ref)


def _outproj_ffn(x1, mod, ng, hf, hb, pa, yb, yc, onorm, wout, wig, wiu, wo, rows):
    tpb = T // TM_FFN
    row = lambda w, c=0: pl.BlockSpec((TM_FFN, w), lambda i: (i, c))
    return pl.pallas_call(
        _outproj_ffn_kernel,
        grid=(rows // TM_FFN,),
        in_specs=[
            row(D),
            pl.BlockSpec((1, N_MOD, D), lambda i: (i // tpb, 0, 0)),
            _resident(ng.shape),
            row(A_HEADS * A_DV), row(A_HEADS * A_DV), row(A_HEADS * A_DV, COL_AO // (A_HEADS * A_DV)),
            row(B_HEADS * B_DV), row(C_HEADS * C_DH),
            _resident(onorm.shape), _resident(wout.shape),
            _resident(wig.shape), _resident(wiu.shape), _resident(wo.shape),
        ],
        out_specs=row(D),
        out_shape=jax.ShapeDtypeStruct((rows, D), f32),
        compiler_params=_params(1),
        name="outproj_ffn2",
    )(x1, mod, ng, hf, hb, pa, yb, yc, onorm, wout, wig, wiu, wo)


def _head_norm_rope(x, gain, n_real, cos, sin_next, sin_prev, shift):
    ss = jnp.sum(x * x, axis=-1, keepdims=True) * (1.0 / n_real)
    y = x * lax.rsqrt(ss + EPS) * gain
    return y * cos + pltpu.roll(y, LANES - shift, 1) * sin_next + pltpu.roll(y, shift, 1) * sin_prev


def _inproj_kernel(h_ref, win_ref, bias_ref, tab_ref, cqn_ref, ckvn_ref, wuq_ref, wukv_ref, hg_ref,
                   pa_ref, qb_ref, kb_ref, vb_ref, qc_ref, kc_ref, vc_ref):
    p = _dot(h_ref[...], win_ref[...]) + bias_ref[...]

    pa_ref[:, COL_AQ:COL_AK] = p[:, COL_AQ:COL_AK] * (A_DK ** -0.5)
    pa_ref[:, COL_AK:COL_AG] = p[:, COL_AK:COL_AG]
    graw = p[:, COL_AG:COL_AG + LANES]
    lane = lax.broadcasted_iota(jnp.int32, (1, LANES), 1)
    for kk in range(4):
        gk = graw if kk == 0 else pltpu.roll(graw, LANES - A_HEADS * kk, 1)
        if kk % 2 == 1:
            gk = _log_sigmoid(gk)
        pa_ref[:, COL_AG + LANES * kk:COL_AG + LANES * (kk + 1)] = jnp.where(lane < A_HEADS, gk, 0.0)

    tab = tab_ref[...]
    tb = [tab[:, LANES * i:LANES * (i + 1)] for i in range(6)]
    hg = hg_ref[...]

    cq = _rms(p[:, COL_BCQ:COL_BCKV], cqn_ref[...]).astype(bf16)
    ckv = _rms(p[:, COL_BCKV:COL_BKR], ckvn_ref[...]).astype(bf16)
    kr = p[:, COL_BKR:COL_CQ]
    q = _dot(cq, wuq_ref[...])
    kv = _dot(ckv, wukv_ref[...])
    for hh in range(B_HEADS):
        sl = slice(LANES * hh, LANES * (hh + 1))
        qh = _head_norm_rope(q[:, sl], hg[0:1, :], B_DQK, tb[0], tb[1], tb[2], B_ROPE // 4)
        qb_ref[:, sl] = (qh * (B_DQK ** -0.5 * LOG2E)).astype(bf16)
        kh = _head_norm_rope(kv[:, sl] + kr, hg[1:2, :], B_DQK, tb[0], tb[1], tb[2], B_ROPE // 4)
        kb_ref[:, sl] = kh.astype(bf16)
    vb_ref[...] = kv[:, B_HEADS * LANES:].astype(bf16)

    for hh in range(C_HEADS):
        sl = slice(LANES * hh, LANES * (hh + 1))
        qh = _head_norm_rope(p[:, COL_CQ + LANES * hh:COL_CQ + LANES * (hh + 1)], hg[2:3, :], C_DH,
                             tb[3], tb[4], tb[5], C_DH // 4)
        qc_ref[:, sl] = (qh * (C_DH ** -0.5 * LOG2E)).astype(bf16)
    for hh in range(C_KV_HEADS):
        sl = slice(LANES * hh, LANES * (hh + 1))
        kh = _head_norm_rope(p[:, COL_CK + LANES * hh:COL_CK + LANES * (hh + 1)], hg[3:4, :], C_DH,
                             tb[3], tb[4], tb[5], C_DH // 4)
        kc_ref[:, sl] = kh.astype(bf16)
    vc_ref[...] = p[:, COL_CV:COL_CV + LANES].astype(bf16)


def _inproj(h, win, bias, tab, cqn, ckvn, wuq, wukv, hg):
    tpb = T // TM_IN
    row = lambda w: pl.BlockSpec((TM_IN, w), lambda i: (i, 0))
    out_w = [(PA_W, f32), (B_HEADS * LANES, bf16), (B_HEADS * LANES, bf16), (B_HEADS * B_DV, bf16),
             (C_HEADS * LANES, bf16), (C_KV_HEADS * LANES, bf16), (C_KV_HEADS * C_DH, bf16)]
    return pl.pallas_call(
        _inproj_kernel,
        grid=(N // TM_IN,),
        in_specs=[
            row(D), _resident(win.shape), _resident(bias.shape),
            pl.BlockSpec((TM_IN, 6 * LANES), lambda i: (jnp.where(i < NL // TM_IN, i % tpb, tpb), 0)),
            _resident(cqn.shape), _resident(ckvn.shape), _resident(wuq.shape), _resident(wukv.shape),
            _resident(hg.shape),
        ],
        out_specs=[row(w) for w, _ in out_w],
        out_shape=[jax.ShapeDtypeStruct((N, w), dt) for w, dt in out_w],
        compiler_params=_params(1),
        name="inproj",
    )(h, win, bias, tab, cqn, ckvn, wuq, wukv, hg)


def _mlstm_stream(refs, c_ref, m_ref, sidx, rev):
    q_ref, k_ref, v_ref, ig_ref, lf_ref, gt_ref, h_ref = refs
    L = A_CHUNK
    ti = lax.broadcasted_iota(jnp.int32, (L, L), 0)
    si = lax.broadcasted_iota(jnp.int32, (L, L), 1)
    attend = (si >= ti) if rev else (si <= ti)
    attend_f = attend.astype(f32)
    cum_rows = ((ti >= si) if rev else (ti <= si)).astype(f32)
    lane = lax.broadcasted_iota(jnp.int32, (1, LANES), 1)
    sub = lax.broadcasted_iota(jnp.int32, (LANES, 1), 0)
    n_chunks = MLSTM_R // L
    hp = lax.Precision.HIGHEST

    m_old = m_ref[sidx, 0:1, :]
    c_state = [c_ref[sidx, hh] for hh in range(A_HEADS)]
    order = range(n_chunks - 1, -1, -1) if rev else range(n_chunks)
    for cc in order:
        rows = slice(cc * L, (cc + 1) * L)
        q = q_ref[rows, :]
        k = k_ref[rows, :]
        v = v_ref[rows, :]
        ig = ig_ref[rows, :]
        lf = lf_ref[rows, :]
        gt = gt_ref[cc]
        b_rows = jnp.dot(gt, cum_rows, preferred_element_type=f32, precision=hp)
        b_col = jnp.dot(attend_f, lf, preferred_element_type=f32, precision=hp)
        b_last = jnp.sum(lf, axis=0, keepdims=True)
        g_col = b_last - b_col + ig
        g_max = jnp.max(g_col, axis=0, keepdims=True)
        wg = jnp.exp(g_col - g_max)
        m_new = jnp.maximum(b_last + m_old, g_max)
        decay = jnp.exp(b_last + m_old - m_new)
        kv_scale = jnp.exp(g_max - m_new)
        qb = q.astype(bf16)
        kb = k.astype(bf16)
        outs = []
        for hh in range(A_HEADS):
            in_head = (lane >= hh * A_DK) & (lane < (hh + 1) * A_DK)
            bc = b_col[:, hh:hh + 1]
            dlog = jnp.where(attend, bc - b_rows[A_HEADS + hh:A_HEADS + hh + 1, :] + gt[hh:hh + 1, :], -jnp.inf)
            inter = bc + m_old[:, hh:hh + 1]
            mt = jnp.maximum(inter, jnp.max(dlog, axis=-1, keepdims=True))
            w = jnp.exp(dlog - mt)
            a_inter = jnp.exp(inter - mt)
            s = _dot_nt(jnp.where(in_head, q, 0.0).astype(bf16), kb) * w
            vpair = v[:, LANES * (hh // 2):LANES * (hh // 2 + 1)]
            if hh % 2 == 0:
                vext = jnp.where(lane < A_DV, vpair, jnp.where(lane == A_DV, 1.0, 0.0))
                den_lane = A_DV
            else:
                vext = jnp.where(lane >= A_DV, vpair, jnp.where(lane == 0, 1.0, 0.0))
                den_lane = 0
            vext = vext.astype(bf16)
            c_h = c_state[hh]
            out = a_inter * _dot(qb, c_h.astype(bf16)) + _dot(s.astype(bf16), vext)
            den = out[:, den_lane:den_lane + 1]
            outs.append(out / jnp.maximum(jnp.abs(den), jnp.exp(-mt)))
            kw_t = (wg[:, hh:hh + 1] * k).T.astype(bf16)
            kv = _dot(kw_t, vext)
            row_in_head = (sub >= hh * A_DK) & (sub < (hh + 1) * A_DK)
            c_state[hh] = decay[:, hh:hh + 1] * c_h + jnp.where(row_in_head, kv_scale[:, hh:hh + 1] * kv, 0.0)
        m_old = m_new
        h_ref[rows, 0:LANES] = jnp.where(lane < A_DV, outs[0], outs[1])
        h_ref[rows, LANES:2 * LANES] = jnp.where(lane < A_DV, outs[2], outs[3])
    m_ref[sidx, 0:1, :] = m_old
    for hh in range(A_HEADS):
        c_ref[sidx, hh] = c_state[hh]


def _mlstm_kernel(*refs):
    fwd = refs[0:6] + refs[12:13]
    bwd = refs[6:12] + refs[13:14]
    c_ref, m_ref = refs[14:16]

    @pl.when(pl.program_id(1) == 0)
    def _():
        c_ref[...] = jnp.zeros_like(c_ref)
        m_ref[...] = jnp.zeros_like(m_ref)

    _mlstm_stream(fwd, c_ref, m_ref, 0, False)
    _mlstm_stream(bwd, c_ref, m_ref, 1, True)


def _mlstm(pa, gt):
    nb = T // MLSTM_R
    assert CTX == MLSTM_R

    def rb(rev):
        def f(b, j):
            jj = j - 1
            return jnp.where(j == 0, NL // MLSTM_R + b, b * nb + (nb - 1 - jj if rev else jj))
        return f

    def stream_specs(rev):
        r = rb(rev)
        d = 2 if rev else 0
        col = lambda w, c: pl.BlockSpec((MLSTM_R, w), lambda b, j: (r(b, j), c))
        return [
            col(LANES, COL_AQ // LANES), col(LANES, COL_AK // LANES), col(2 * LANES, COL_AV // (2 * LANES)),
            col(LANES, COL_AG // LANES + d), col(LANES, COL_AG // LANES + d + 1),
            pl.BlockSpec((MLSTM_R // A_CHUNK, 2 * A_HEADS, A_CHUNK), lambda b, j: (r(b, j), d // 2, 0)),
        ]

    out_spec = lambda rev: pl.BlockSpec((MLSTM_R, A_HEADS * A_DV), lambda b, j: (rb(rev)(b, j), 0))
    return pl.pallas_call(
        _mlstm_kernel,
        grid=(B, nb + 1),
        in_specs=stream_specs(False) + stream_specs(True),
        out_specs=[out_spec(False), out_spec(True)],
        out_shape=[jax.ShapeDtypeStruct((N, A_HEADS * A_DV), f32)] * 2,
        scratch_shapes=[pltpu.VMEM((2, A_HEADS, LANES, LANES), f32), pltpu.VMEM((2, 8, LANES), f32)],
        compiler_params=_params(2),
        name="mlstm",
    )(*([pa] * 5 + [gt]) * 2)


def _tile_max(s, m128):
    for t in range(s.shape[1] // LANES):
        blk = s[:, LANES * t:LANES * (t + 1)]
        m128 = blk if m128 is None else jnp.maximum(m128, blk)
    return m128


def _mla_kernel(*refs, latent):
    if latent:
        q_ref, kc_ref, vc_ref, kl_ref, vl_ref, o_ref, s_ref = refs
        sources = [(kc_ref, vc_ref, 0, CTX)] + [(kl_ref, vl_ref, c, MLA_KC) for c in range(0, T, MLA_KC)]
    else:
        q_ref, kc_ref, vc_ref, _, o_ref, s_ref = refs
        sources = [(kc_ref, vc_ref, 0, CTX)]
    q = q_ref[...]
    lane = lax.broadcasted_iota(jnp.int32, (1, LANES), 1)
    outs = []
    for hh in range(2):
        sl = slice(LANES * hh, LANES * (hh + 1))
        qh = q[:, sl]
        den_lane = B_DV if hh == 0 else 0
        m128 = None
        off = 0
        for k_ref, _, r0, n in sources:
            s = _dot_nt(qh, k_ref[r0:r0 + n, sl])
            s_ref[:, off:off + n] = s
            m128 = _tile_max(s, m128)
            off += n
        m = jnp.max(m128, axis=-1, keepdims=True)
        acc = None
        off = 0
        for _, v_ref, r0, n in sources:
            p = jnp.exp2(s_ref[:, off:off + n] - m).astype(bf16)
            vext = jnp.where(lane == den_lane, 1.0, v_ref[r0:r0 + n, :]).astype(bf16)
            part = _dot(p, vext)
            acc = part if acc is None else acc + part
            off += n
        outs.append(acc / acc[:, den_lane:den_lane + 1])
    o_ref[...] = jnp.where(lane < B_DV, outs[0], outs[1]).astype(o_ref.dtype)


def _mla(qb, kb, vb, prev=None):
    latent = prev is None
    npair = B_HEADS // 2
    ctx_blk = NL // CTX
    kv_specs = [
        pl.BlockSpec((CTX, 2 * LANES), lambda b, p, i: (ctx_blk + b, p)),
        pl.BlockSpec((CTX, LANES), lambda b, p, i: (ctx_blk + b, p)),
    ]
    if latent:
        tq = MLA_TQ
        nq = T // tq
        qmap = lambda b, p, i: (b * nq + i, p)
        kv_specs += [
            pl.BlockSpec((T, 2 * LANES), lambda b, p, i: (b, p)),
            pl.BlockSpec((T, LANES), lambda b, p, i: (b, p)),
        ]
        args = (qb, kb, vb, kb, vb)
        aliases = {}
        nkeys = CTX + T
    else:
        tq = CTX
        nq = 1
        qmap = lambda b, p, i: (ctx_blk + b, p)
        kv_specs += [pl.BlockSpec(memory_space=pl.ANY)]
        args = (qb, kb, vb, prev)
        aliases = {3: 0}
        nkeys = CTX
    return pl.pallas_call(
        functools.partial(_mla_kernel, latent=latent),
        grid=(B, npair, nq),
        in_specs=[pl.BlockSpec((tq, 2 * LANES), qmap)] + kv_specs,
        out_specs=pl.BlockSpec((tq, LANES), qmap),
        out_shape=jax.ShapeDtypeStruct((N, B_HEADS * B_DV), bf16),
        scratch_shapes=[pltpu.VMEM((tq, nkeys), f32)],
        input_output_aliases=aliases,
        compiler_params=_params(3),
        name="mla_latent" if latent else "mla_context",
    )(*args)


def _gqa_kernel(sink_ref, *refs, latent):
    if latent:
        q_ref, kc_ref, vc_ref, kl_ref, vl_ref, o_ref = refs
    else:
        q_ref, kc_ref, vc_ref, _, o_ref = refs
    q = q_ref[...]
    tq = q.shape[0]
    lane = lax.broadcasted_iota(jnp.int32, (1, LANES), 1)
    vc = vc_ref[...]
    if latent:
        n = pl.program_id(1)
        start = pl.multiple_of(jnp.clip(n * GQA_TQ - WINDOW, 0, T - GQA_BAND), WINDOW)
        kband = kl_ref[pl.ds(start, GQA_BAND), :]
        vband = vl_ref[pl.ds(start, GQA_BAND), :]
        qpos1 = n * GQA_TQ + lax.broadcasted_iota(jnp.int32, (tq, 1), 0)
        qpos = jnp.concatenate([qpos1] * C_GROUP, axis=0)
        kpos = start + lax.broadcasted_iota(jnp.int32, (1, GQA_BAND), 1)
        valid = jnp.abs(qpos - kpos) <= WINDOW
    heads = []
    for kvh in range(C_KV_HEADS):
        sl = slice(LANES * kvh, LANES * (kvh + 1))
        den_lane = C_DH * (1 - kvh)
        qs = jnp.concatenate([q[:, LANES * (C_GROUP * kvh + g):LANES * (C_GROUP * kvh + g + 1)]
                              for g in range(C_GROUP)], axis=0)
        sink = jnp.concatenate([jnp.full((tq, 1), sink_ref[C_GROUP * kvh + g] * LOG2E, f32)
                                for g in range(C_GROUP)], axis=0)
        s_c = _dot_nt(qs, kc_ref[:, sl])
        m128 = _tile_max(s_c, None)
        if latent:
            s_b = jnp.where(valid, _dot_nt(qs, kband[:, sl]), -jnp.inf)
            m128 = _tile_max(s_b, m128)
        m = jnp.maximum(sink, jnp.max(m128, axis=-1, keepdims=True))
        acc = _dot(jnp.exp2(s_c - m).astype(bf16), jnp.where(lane == den_lane, 1.0, vc).astype(bf16))
        if latent:
            acc = acc + _dot(jnp.exp2(s_b - m).astype(bf16), jnp.where(lane == den_lane, 1.0, vband).astype(bf16))
        o = acc / (jnp.exp2(sink - m) + acc[:, den_lane:den_lane + 1])
        heads += [o[g * tq:(g + 1) * tq, :] for g in range(C_GROUP)]
    lo = lane < C_DH
    o_ref[:, 0:LANES] = jnp.where(lo, heads[0], pltpu.roll(heads[1], C_DH, 1)).astype(o_ref.dtype)
    o_ref[:, LANES:2 * LANES] = jnp.where(lo, heads[2], heads[3]).astype(o_ref.dtype)
    o_ref[:, 2 * LANES:3 * LANES] = jnp.where(lo, pltpu.roll(heads[4], C_DH, 1), heads[5]).astype(o_ref.dtype)


def _gqa(sink, qc, kc, vc, prev=None):
    latent = prev is None
    ctx_blk = NL // CTX
    kv_specs = [
        pl.BlockSpec((CTX, C_KV_HEADS * LANES), lambda b, i: (ctx_blk + b, 0)),
        pl.BlockSpec((CTX, LANES), lambda b, i: (ctx_blk + b, 0)),
    ]
    if latent:
        tq = GQA_TQ
        nq = T // tq
        qmap = lambda b, i: (b * nq + i, 0)
        kv_specs += [
            pl.BlockSpec((T, C_KV_HEADS * LANES), lambda b, i: (b, 0)),
            pl.BlockSpec((T, LANES), lambda b, i: (b, 0)),
        ]
        args = (sink, qc, kc, vc, kc, vc)
        aliases = {}
    else:
        tq = CTX
        nq = 1
        qmap = lambda b, i: (ctx_blk + b, 0)
        kv_specs += [pl.BlockSpec(memory_space=pl.ANY)]
        args = (sink, qc, kc, vc, prev)
        aliases = {4: 0}
    return pl.pallas_call(
        functools.partial(_gqa_kernel, latent=latent),
        grid=(B, nq),
        in_specs=[pl.BlockSpec(memory_space=pltpu.SMEM), pl.BlockSpec((tq, C_HEADS * LANES), qmap)] + kv_specs,
        out_specs=pl.BlockSpec((tq, C_HEADS * C_DH), qmap),
        out_shape=jax.ShapeDtypeStruct((N, C_HEADS * C_DH), bf16),
        input_output_aliases=aliases,
        compiler_params=_params(2),
        name="gqa_latent" if latent else "gqa_context",
    )(*args)


def _pad_cols(w, width):
    return jnp.pad(w, ((0, 0), (0, width - w.shape[1])))


def _pad_heads(w, heads, dh):
    r = w.shape[0]
    return jnp.pad(w.reshape(r, heads, dh), ((0, 0), (0, 0), (0, LANES - dh))).reshape(r, heads * LANES)


def _arrange_w_in(w):
    o = np.cumsum((0, 128, 128, 256, 256, 16, 256, 128, 32, 384, 128, 128))
    part = lambda i: w[:, int(o[i]):int(o[i + 1])]
    kr = jnp.pad(part(7), ((0, 0), (B_NOPE, LANES - B_NOPE - B_ROPE)))
    return jnp.concatenate([
        part(0), part(1), part(2), part(3), _pad_cols(part(4), LANES),
        part(5), part(6), kr,
        _pad_heads(part(8), C_HEADS, C_DH), _pad_heads(part(9), C_KV_HEADS, C_DH), part(10),
    ], axis=1)


def _rope_tables():
    t = jnp.arange(T + TM_IN, dtype=jnp.int32)[:, None]
    live = t < T
    row = (t // GRID_W).astype(f32)
    col = (t % GRID_W).astype(f32)
    lane = jnp.arange(LANES, dtype=jnp.int32)[None, :]

    def tables(first_lane, half):
        rel = lane - first_lane
        in_rope = (rel >= 0) & (rel < 4 * half) & live
        freq = ROPE_BASE ** (-(rel % half).astype(f32) / half)
        ang = jnp.where(rel < 2 * half, row, col) * freq
        second = (rel // half) % 2 == 1
        cos = jnp.where(in_rope, jnp.cos(ang), 1.0)
        sin = jnp.sin(ang)
        s_next = jnp.where(in_rope & ~second, -sin, 0.0)
        s_prev = jnp.where(in_rope & second, sin, 0.0)
        return [cos, s_next, s_prev]

    return jnp.concatenate(tables(B_NOPE, B_ROPE // 4) + tables(0, C_DH // 4), axis=1)


def kernel(x, c, ctx, c_ctx, ada_w, ada_b, norm_g, ffn1_wi, ffn1_wo, ffn2_wi, ffn2_wo, w_in, w_out,
           mlstm_gate_b, mlstm_out_norm, mla_cq_norm, mla_ckv_norm, mla_w_uq, mla_w_ukv, mla_q_norm, mla_k_norm,
           gqa_q_norm, gqa_k_norm, gqa_sink):
    cc = jnp.concatenate([c, c_ctx[None, :], jnp.zeros((MOD_ROWS - B - 1, D), f32)], axis=0)
    mod_all = _ada(cc, ada_w, ada_b).reshape(DEPTH, MOD_ROWS, N_MOD, D)
    tab = _rope_tables()
    xs = (x.reshape(NL, D), ctx.reshape(NCX, D))

    for l in range(DEPTH):
        need_ctx = l < DEPTH - 1
        mod = mod_all[l]
        ng = norm_g[l]
        wig1, wiu1 = ffn1_wi[l, :, :D_FF].astype(bf16), ffn1_wi[l, :, D_FF:].astype(bf16)
        wig2, wiu2 = ffn2_wi[l, :, :D_FF].astype(bf16), ffn2_wi[l, :, D_FF:].astype(bf16)
        wo1, wo2 = ffn1_wo[l].astype(bf16), ffn2_wo[l].astype(bf16)
        win = _arrange_w_in(w_in[l]).astype(bf16)
        bias = _pad_cols(jnp.pad(mlstm_gate_b[l][None], ((0, 0), (COL_AG, 0))), WP)
        wuq = _pad_heads(mla_w_uq[l], B_HEADS, B_DQK).astype(bf16)
        ukv = mla_w_ukv[l].reshape(B_KV_RANK, B_HEADS, B_NOPE + B_DV)
        wukv = jnp.concatenate([_pad_heads(ukv[:, :, :B_NOPE].reshape(B_KV_RANK, -1), B_HEADS, B_NOPE),
                                ukv[:, :, B_NOPE:].reshape(B_KV_RANK, -1)], axis=1).astype(bf16)
        hg = jnp.concatenate([_pad_cols(g[l][None], LANES) for g in (mla_q_norm, mla_k_norm, gqa_q_norm, gqa_k_norm)])

        x1, h = _ffn1(xs, mod, ng, wig1, wiu1, wo1)
        pa, qb, kb, vb, qc, kc, vc = _inproj(h, win, bias, tab, mla_cq_norm[l][None], mla_ckv_norm[l][None],
                                             wuq, wukv, hg)

        gates = jnp.concatenate([pa[:, COL_AG + LANES * kk:COL_AG + LANES * kk + A_HEADS] for kk in range(4)], axis=1)
        gt = gates.reshape(N // A_CHUNK, A_CHUNK, 4 * A_HEADS).transpose(0, 2, 1)
        hf, hb = _mlstm(pa, gt)

        yb = _mla(qb, kb, vb)
        yc = _gqa(gqa_sink[l], qc, kc, vc)
        if need_ctx:
            yb = _mla(qb, kb, vb, prev=yb)
            yc = _gqa(gqa_sink[l], qc, kc, vc, prev=yc)

        rows = N if need_ctx else NL
        xs = _outproj_ffn(x1, mod, ng, hf, hb, pa, yb, yc, mlstm_out_norm[l][None], w_out[l].astype(bf16),
                          wig2, wiu2, wo2, rows)
    return xs.reshape(B, T, D)
```

```python
import functools
import math

import jax
import jax.numpy as jnp
import numpy as np
from jax import lax
from jax.experimental import pallas as pl
from jax.experimental.pallas import tpu as pltpu

f32 = jnp.float32
bf16 = jnp.bfloat16

D = 1024
B = 4
T = 4096
CTX = 256
DEPTH = 2
GRID_W = 64
ROPE_BASE = 10000.0
EPS = 1e-6
HALF = 0.5
N_MOD = 9
D_FF = 2816
A_HEADS, A_DK, A_DV, A_CHUNK = 4, 32, 64, 64
B_HEADS, B_Q_RANK, B_KV_RANK, B_NOPE, B_ROPE, B_DV = 6, 256, 128, 64, 32, 64
B_DQK = B_NOPE + B_ROPE
C_HEADS, C_KV_HEADS, C_DH, WINDOW = 6, 2, 64, 128
C_GROUP = C_HEADS // C_KV_HEADS

NL = B * T
NCX = B * CTX
N = NL + NCX

LANES = 128
HALF_LANES = LANES // 2
MOD_ROWS = 8
VMEM_LIMIT = 56 * 1024 * 1024
LOG2E = math.log2(math.e)

TM_FFN = 512
FF_CHUNK = 1408
TM_IN = 512
ADA_TN = 1152
MLSTM_R = 256
MLA_TQ = 256
MLA_KC = 512
GQA_TQ = 256
GQA_BAND = GQA_TQ + 2 * WINDOW

COL_AQ, COL_AV, COL_AO, COL_AG = 0, 128, 384, 640
COL_BCQ, COL_BCKV, COL_BKR = 768, 1024, 1152
COL_CQ, COL_CK, COL_CV = 1280, 2048, 2304
WP = 2432
PA_V, PA_O, PA_Q, PA_G = 0, 256, 512, 640
PA_W = PA_G + 4 * LANES
N_GATES = 4 * A_HEADS


def _head_lane_map_b():
    m = -np.ones(LANES, np.int64)
    m[0:8], m[8:16], m[16:64] = np.arange(64, 72), np.arange(80, 88), np.arange(0, 48)
    m[64:72], m[72:80], m[80:96] = np.arange(72, 80), np.arange(88, 96), np.arange(48, 64)
    return m


def _head_lane_map_c():
    m = -np.ones(LANES, np.int64)
    m[0:16], m[16:32] = np.arange(0, 16), np.arange(32, 48)
    m[64:80], m[80:96] = np.arange(16, 32), np.arange(48, 64)
    return m


def _sigmoid(x):
    return 1.0 / (1.0 + jnp.exp(-x))


def _log_sigmoid(x):
    return jnp.minimum(x, 0.0) - jnp.log(1.0 + jnp.exp(-jnp.abs(x)))


def _rms(x, g):
    ms = jnp.mean(x * x, axis=-1, keepdims=True)
    return x * lax.rsqrt(ms + EPS) * g


def _dot(a, b):
    return jnp.dot(a, b, preferred_element_type=f32)


def _dot_nt(a, b):
    return lax.dot_general(a, b, (((1,), (1,)), ((), ())), preferred_element_type=f32)


def _split3(x):
    hi = x.astype(bf16)
    r1 = x - hi.astype(f32)
    mid = r1.astype(bf16)
    return hi, mid, (r1 - mid.astype(f32)).astype(bf16)


def _resident(shape):
    nd = len(shape)
    return pl.BlockSpec(shape, lambda *_: (0,) * nd, pipeline_mode=pl.Buffered(1))


def _params(n_axes):
    return pltpu.CompilerParams(dimension_semantics=("arbitrary",) * n_axes, vmem_limit_bytes=VMEM_LIMIT)


def _ada_kernel(c_ref, w_ref, b_ref, o_ref):
    c = c_ref[...]
    s = c * _sigmoid(c)
    o_ref[0] = jnp.dot(s, w_ref[0], preferred_element_type=f32, precision=lax.Precision.HIGHEST) + b_ref[0]


def _ada(cc, ada_w, ada_b):
    nt = (N_MOD * D) // ADA_TN
    return pl.pallas_call(
        _ada_kernel,
        grid=(DEPTH, nt),
        in_specs=[
            pl.BlockSpec((MOD_ROWS, D), lambda l, j: (0, 0)),
            pl.BlockSpec((1, D, ADA_TN), lambda l, j: (l, 0, j)),
            pl.BlockSpec((1, 1, ADA_TN), lambda l, j: (l, 0, j)),
        ],
        out_specs=pl.BlockSpec((1, MOD_ROWS, ADA_TN), lambda l, j: (l, 0, j)),
        out_shape=jax.ShapeDtypeStruct((DEPTH, MOD_ROWS, N_MOD * D), f32),
        compiler_params=_params(2),
        name="ada_mod",
    )(cc, ada_w, ada_b.reshape(DEPTH, 1, N_MOD * D))


def _ffn(x, g, shift, scale, gate, wig_ref, wiu_ref, wo_ref):
    h = (_rms(x, g) * (1.0 + scale) + shift).astype(bf16)
    acc = None
    for c in range(D_FF // FF_CHUNK):
        sl = slice(c * FF_CHUNK, (c + 1) * FF_CHUNK)
        gt = _dot(h, wig_ref[:, sl])
        up = _dot(h, wiu_ref[:, sl])
        a = (gt * _sigmoid(gt) * up).astype(bf16)
        part = _dot(a, wo_ref[sl, :])
        acc = part if acc is None else acc + part
    return x + HALF * gate * acc


def _ffn1_kernel(*refs, split_input):
    if split_input:
        xl_ref, xc_ref, mod_ref, ng_ref, wig_ref, wiu_ref, wo_ref, x1_ref, h_ref = refs
        x = jnp.where(pl.program_id(0) < NL // TM_FFN, xl_ref[...], xc_ref[...])
    else:
        x_ref, mod_ref, ng_ref, wig_ref, wiu_ref, wo_ref, x1_ref, h_ref = refs
        x = x_ref[...]
    mod = mod_ref[0]
    x1 = _ffn(x, ng_ref[0:1, :], mod[0:1, :], mod[1:2, :], mod[2:3, :], wig_ref, wiu_ref, wo_ref)
    x1_ref[...] = x1
    h_ref[...] = (_rms(x1, ng_ref[1:2, :]) * (1.0 + mod[4:5, :]) + mod[3:4, :]).astype(bf16)


def _ffn1(xs, mod, ng, wig, wiu, wo):
    split_input = isinstance(xs, tuple)
    tpb = T // TM_FFN
    nlt = NL // TM_FFN
    if split_input:
        assert NCX % TM_FFN == 0
        x_specs = [pl.BlockSpec((TM_FFN, D), lambda i: (jnp.minimum(i, nlt - 1), 0)),
                   pl.BlockSpec((TM_FFN, D), lambda i: (jnp.maximum(i - nlt, 0), 0))]
    else:
        xs = (xs,)
        x_specs = [pl.BlockSpec((TM_FFN, D), lambda i: (i, 0))]
    return pl.pallas_call(
        functools.partial(_ffn1_kernel, split_input=split_input),
        grid=(N // TM_FFN,),
        in_specs=x_specs + [
            pl.BlockSpec((1, N_MOD, D), lambda i: (i // tpb, 0, 0)),
            _resident(ng.shape), _resident(wig.shape), _resident(wiu.shape), _resident(wo.shape),
        ],
        out_specs=[pl.BlockSpec((TM_FFN, D), lambda i: (i, 0))] * 2,
        out_shape=[jax.ShapeDtypeStruct((N, D), f32), jax.ShapeDtypeStruct((N, D), bf16)],
        compiler_params=_params(1),
        name="ffn1",
    )(*xs, mod, ng, wig, wiu, wo)


def _outproj_ffn_kernel(x_ref, mod_ref, ng_ref, hf_ref, hb_ref, o_ref, yb_ref, yc_ref, on_ref, wout_ref,
                        wig_ref, wiu_ref, wo_ref, out_ref):
    x = x_ref[...]
    mod = mod_ref[0]
    hs = hf_ref[...] + hb_ref[...]
    sq = hs * hs
    head = lax.broadcasted_iota(jnp.int32, (1, A_HEADS * A_DV), 1) // A_DV
    ms = jnp.zeros_like(hs)
    for hh in range(A_HEADS):
        sel = head == hh
        ssh = jnp.sum(jnp.where(sel, sq, 0.0), axis=-1, keepdims=True) * (1.0 / A_DV)
        ms = jnp.where(sel, ssh, ms)
    ya = _sigmoid(o_ref[...]) * (hs * lax.rsqrt(ms + EPS) * on_ref[...])
    y = jnp.concatenate([ya.astype(bf16), yb_ref[...], yc_ref[...]], axis=-1)
    x2 = x + mod[5:6, :] * _dot(y, wout_ref[...])
    out_ref[...] = _ffn(x2, ng_ref[2:3, :], mod[6:7, :], mod[7:8, :], mod[8:9, :], wig_ref, wiu_ref, wo_ref)


def _outproj_ffn(x1, mod, ng, hf, hb, pa, yb, yc, onorm, wout, wig, wiu, wo, rows):
    tpb = T // TM_FFN
    row = lambda w, c=0: pl.BlockSpec((TM_FFN, w), lambda i: (i, c))
    return pl.pallas_call(
        _outproj_ffn_kernel,
        grid=(rows // TM_FFN,),
        in_specs=[
            row(D),
            pl.BlockSpec((1, N_MOD, D), lambda i: (i // tpb, 0, 0)),
            _resident(ng.shape),
            row(A_HEADS * A_DV), row(A_HEADS * A_DV), row(A_HEADS * A_DV, PA_O // (A_HEADS * A_DV)),
            row(B_HEADS * B_DV), row(C_HEADS * C_DH),
            _resident(onorm.shape), _resident(wout.shape),
            _resident(wig.shape), _resident(wiu.shape), _resident(wo.shape),
        ],
        out_specs=row(D),
        out_shape=jax.ShapeDtypeStruct((rows, D), f32),
        compiler_params=_params(1),
        name="outproj_ffn2",
    )(x1, mod, ng, hf, hb, pa, yb, yc, onorm, wout, wig, wiu, wo)


def _inproj_kernel(h_ref, win_ref, bias_ref, tab_ref, cqn_ref, ckvn_ref, wuq_ref, wukv_ref, hg_ref,
                   wkg_ref, gb_ref,
                   pa_ref, kt_ref, gt_ref, qb_ref, kb_ref, vb_ref, qc_ref, kc_ref, vc_ref):
    h = h_ref[...]
    p = _dot(h, win_ref[...]) + bias_ref[...]

    pa_ref[:, PA_V:PA_Q] = p[:, COL_AV:COL_AG]
    pa_ref[:, PA_Q:PA_G] = p[:, COL_AQ:COL_AV] * (A_DK ** -0.5)
    graw = p[:, COL_AG:COL_AG + LANES]
    lane = lax.broadcasted_iota(jnp.int32, (1, LANES), 1)
    for kk in range(4):
        gk = graw if kk == 0 else pltpu.roll(graw, LANES - A_HEADS * kk, 1)
        if kk % 2 == 1:
            gk = _log_sigmoid(gk)
        pa_ref[:, PA_G + LANES * kk:PA_G + LANES * (kk + 1)] = jnp.where(lane < A_HEADS, gk, 0.0)

    grow = lax.broadcasted_iota(jnp.int32, (2 * N_GATES, 1), 0)
    is_forget = ((grow // A_HEADS) % 2 == 1) == (grow < N_GATES)
    for c in range(TM_IN // A_CHUNK):
        t = _dot_nt(wkg_ref[...], h[c * A_CHUNK:(c + 1) * A_CHUNK, :])
        kt_ref[c] = t[0:LANES, :]
        g = t[LANES:LANES + 2 * N_GATES, :] + gb_ref[...]
        gt_ref[c] = jnp.where(is_forget, _log_sigmoid(g), g)

    tab = tab_ref[...]
    tb = [tab[:, LANES * i:LANES * (i + 1)] for i in range(4)]
    hg = hg_ref[...]

    cq = _rms(p[:, COL_BCQ:COL_BCKV], cqn_ref[...]).astype(bf16)
    ckv = _rms(p[:, COL_BCKV:COL_BKR], ckvn_ref[...]).astype(bf16)
    kr = p[:, COL_BKR:COL_CQ]
    q = _dot(cq, wuq_ref[...])
    kv = _dot(ckv, wukv_ref[...])
    vb_ref[...] = kv[:, B_HEADS * LANES:].astype(bf16)
    vc_ref[...] = p[:, COL_CV:COL_CV + LANES].astype(bf16)

    two = lambda a: jnp.concatenate([a, a], axis=1)
    pair = lambda a, j, first=0: a[:, first + 2 * LANES * j:first + 2 * LANES * (j + 1)]
    gb_q, gb_k, gc_q, gc_k = (two(hg[i:i + 1, :]) for i in range(4))
    cos_b, sin_b, cos_c, sin_c = (two(t) for t in tb)
    kr2 = two(kr)
    jobs = []
    for j in range(B_HEADS // 2):
        jobs.append((pair(q, j), gb_q, B_DQK, cos_b, sin_b, B_DQK ** -0.5 * LOG2E, qb_ref, j))
        jobs.append((pair(kv, j) + kr2, gb_k, B_DQK, cos_b, sin_b, None, kb_ref, j))
    for j in range(C_HEADS // 2):
        jobs.append((pair(p, j, COL_CQ), gc_q, C_DH, cos_c, sin_c, C_DH ** -0.5 * LOG2E, qc_ref, j))
    jobs.append((pair(p, 0, COL_CK), gc_k, C_DH, cos_c, sin_c, None, kc_ref, 0))
    ri = lax.broadcasted_iota(jnp.int32, (2 * LANES, 2 * LANES), 0)
    ci = lax.broadcasted_iota(jnp.int32, (2 * LANES, 2 * LANES), 1)
    swap_halves = ((ri // LANES == ci // LANES) & (ri % LANES == (ci + HALF_LANES) % LANES)).astype(bf16)
    lane2 = lax.broadcasted_iota(jnp.int32, (1, 2 * LANES), 1)
    sums = []
    for x, *_ in jobs:
        sq = x * x
        s0 = jnp.sum(sq[:, 0:LANES], axis=-1, keepdims=True)
        s1 = jnp.sum(sq[:, LANES:], axis=-1, keepdims=True)
        sums.append(jnp.where(lane2 < LANES, s0, s1))
    normed = [x * lax.rsqrt(ss * (1.0 / n_real) + EPS) * gain for (x, gain, n_real, *_), ss in zip(jobs, sums)]
    rolled = [_dot(y.astype(bf16), swap_halves) for y in normed]
    for (_, _, _, cos, sin, scale, dst_ref, j), y, yr in zip(jobs, normed, rolled):
        out = y * cos + yr * sin
        if scale is not None:
            out = out * scale
        dst_ref[:, 2 * LANES * j:2 * LANES * (j + 1)] = out.astype(bf16)


def _inproj(h, win, bias, tab, cqn, ckvn, wuq, wukv, hg, wkg, gb):
    tpb = T // TM_IN
    row = lambda w: pl.BlockSpec((TM_IN, w), lambda i: (i, 0))
    chunked = lambda r: pl.BlockSpec((TM_IN // A_CHUNK, r, A_CHUNK), lambda i: (i, 0, 0))
    out_w = [(B_HEADS * LANES, bf16), (B_HEADS * LANES, bf16), (B_HEADS * B_DV, bf16),
             (C_HEADS * LANES, bf16), (C_KV_HEADS * LANES, bf16), (C_KV_HEADS * C_DH, bf16)]
    return pl.pallas_call(
        _inproj_kernel,
        grid=(N // TM_IN,),
        in_specs=[
            row(D), _resident(win.shape), _resident(bias.shape),
            pl.BlockSpec((TM_IN, 4 * LANES), lambda i: (jnp.where(i < NL // TM_IN, i % tpb, tpb), 0)),
            _resident(cqn.shape), _resident(ckvn.shape), _resident(wuq.shape), _resident(wukv.shape),
            _resident(hg.shape), _resident(wkg.shape), _resident(gb.shape),
        ],
        out_specs=[row(PA_W), chunked(LANES), chunked(2 * N_GATES)] + [row(w) for w, _ in out_w],
        out_shape=[jax.ShapeDtypeStruct((N, PA_W), f32),
                   jax.ShapeDtypeStruct((N // A_CHUNK, LANES, A_CHUNK), f32),
                   jax.ShapeDtypeStruct((N // A_CHUNK, 2 * N_GATES, A_CHUNK), f32)]
        + [jax.ShapeDtypeStruct((N, w), dt) for w, dt in out_w],
        compiler_params=_params(1),
        name="inproj",
    )(h, win, bias, tab, cqn, ckvn, wuq, wukv, hg, wkg, gb)


def _cummax_rows(x, rev):
    n = x.shape[0]
    row = lax.broadcasted_iota(jnp.int32, (n, 1), 0)
    sh = 1
    while sh < n:
        if rev:
            x = jnp.maximum(x, jnp.where(row < n - sh, pltpu.roll(x, n - sh, 0), -jnp.inf))
        else:
            x = jnp.maximum(x, jnp.where(row >= sh, pltpu.roll(x, sh, 0), -jnp.inf))
        sh *= 2
    return x


def _stack_heads(pieces):
    return jnp.concatenate(pieces, axis=0)


def _mlstm_kernel(*refs):
    streams = [(refs[0:7] + refs[14:15], False), (refs[7:14] + refs[15:16], True)]
    s_ref, ml_ref, ms_ref = refs[16:19]

    @pl.when(pl.program_id(1) == 0)
    def _():
        s_ref[...] = jnp.zeros_like(s_ref)
        ml_ref[...] = jnp.zeros_like(ml_ref)
        ms_ref[...] = jnp.zeros_like(ms_ref)

    L = A_CHUNK
    n_chunks = MLSTM_R // L
    heads = range(A_HEADS)
    ti = lax.broadcasted_iota(jnp.int32, (L, L), 0)
    si = lax.broadcasted_iota(jnp.int32, (L, L), 1)
    lane = lax.broadcasted_iota(jnp.int32, (1, LANES), 1)
    row8 = lax.broadcasted_iota(jnp.int32, (2 * A_HEADS, 1), 0)
    in_head = [(lane >= hh * A_DK) & (lane < (hh + 1) * A_DK) for hh in heads]
    ones_blk = jnp.ones((L, LANES), bf16)

    items = []
    for sidx, (srefs, rev) in enumerate(streams):
        q_ref, kt_ref, v_ref, ig_ref, lf_ref, gt_ref, gts_ref, h_ref = srefs
        attend = (si >= ti) if rev else (si <= ti)
        attend4 = _stack_heads([attend] * A_HEADS)
        cum_cols = attend.astype(bf16)
        cum_rows = ((ti >= si) if rev else (ti <= si)).astype(bf16)
        m_lane = ml_ref[sidx, 0:1, :]
        m_sub = ms_ref[sidx, :, 0:1]
        for cc in (range(n_chunks - 1, -1, -1) if rev else range(n_chunks)):
            rows = slice(cc * L, (cc + 1) * L)
            ig = ig_ref[rows, :]
            lf = lf_ref[rows, :]
            gt = gt_ref[cc]
            gts = gts_ref[cc]
            b_col = sum(_dot(cum_cols, piece) for piece in _split3(lf))
            r_col = ig - b_col
            big_m = jnp.maximum(m_lane, _cummax_rows(r_col, rev))
            mt_col = b_col + big_m
            b_last_l = jnp.sum(lf, axis=0, keepdims=True)
            m_new_l = jnp.maximum(m_lane, jnp.max(r_col, axis=0, keepdims=True)) + b_last_l
            b_rows = sum(_dot(piece, cum_rows) for piece in _split3(gts))
            live = row8 < A_HEADS
            r8 = jnp.where(live, gt - b_rows, 0.0)
            b_last_s = jnp.where(live, jnp.sum(gts, axis=-1, keepdims=True), 0.0)
            r_max_s = jnp.max(r8, axis=-1, keepdims=True)
            wg8 = jnp.exp(r8 - r_max_s)
            m_new_s = jnp.maximum(m_sub, r_max_s) + b_last_s
            decay_s = jnp.exp(b_last_s + m_sub - m_new_s)
            scale_s = jnp.exp(b_last_s + r_max_s - m_new_s)
            expand = lambda a, n: _stack_heads([jnp.broadcast_to(a[hh:hh + 1, :], (n, a.shape[1])) for hh in heads])
            q = q_ref[rows, :]
            big_m_b = _stack_heads([jnp.broadcast_to(big_m[:, hh:hh + 1], (L, LANES)) for hh in heads])
            mt_b = _stack_heads([jnp.broadcast_to(mt_col[:, hh:hh + 1], (L, LANES)) for hh in heads])
            m_old_b = _stack_heads([jnp.broadcast_to(m_lane[:, hh:hh + 1], (L, LANES)) for hh in heads])
            items.append(dict(
                sidx=sidx, rows=rows, h_ref=h_ref,
                qst=_stack_heads([jnp.where(in_head[hh], q, 0.0) for hh in heads]).astype(bf16),
                kt=kt_ref[cc].astype(bf16),
                kw=(kt_ref[cc] * expand(wg8, A_DK)).astype(bf16),
                vo=jnp.concatenate([v_ref[rows, :].astype(bf16), ones_blk], axis=1),
                w=jnp.exp(jnp.where(attend4, expand(r8, L) - big_m_b[:, 0:L], -jnp.inf)),
                a_inter=jnp.exp(m_old_b - big_m_b), floor=jnp.exp(-mt_b),
                decay=jnp.broadcast_to(expand(decay_s, A_DK), (LANES, LANES)),
                kv_scale=jnp.broadcast_to(expand(scale_s, A_DK), (LANES, LANES))))
            m_lane, m_sub = m_new_l, m_new_s
        ml_ref[sidx, 0:1, :] = m_lane
        ms_ref[sidx, :, 0:1] = m_sub

    for it in items:
        it["s"] = _dot(it["qst"], it["kt"])

    for it in items:
        it["kv"] = _dot(it["kw"], it["vo"])

    for it in items:
        it["p"] = (it["s"] * it["w"]).astype(bf16)

    state = [s_ref[0], s_ref[1]]
    tile3 = lambda a: jnp.concatenate([a] * 3, axis=1)
    for it in items:
        st = state[it["sidx"]]
        it["c_in"] = st.astype(bf16)
        state[it["sidx"]] = tile3(it["decay"]) * st + tile3(it["kv_scale"]) * it["kv"]
    s_ref[0] = state[0]
    s_ref[1] = state[1]

    nv = A_HEADS * A_DV
    for it in items:
        out = tile3(it["a_inter"]) * _dot(it["qst"], it["c_in"]) + _dot(it["p"], it["vo"])
        res = out[:, 0:nv] / tile3(jnp.maximum(jnp.abs(out[:, nv:]), it["floor"]))[:, 0:nv]
        for pair in range(A_HEADS // 2):
            sl = slice(LANES * pair, LANES * (pair + 1))
            even = res[L * 2 * pair:L * (2 * pair + 1), sl]
            odd = res[L * (2 * pair + 1):L * (2 * pair + 2), sl]
            it["h_ref"][it["rows"], sl] = jnp.where(lane < A_DV, even, odd)


def _mlstm(pa, kt, gt):
    nb = T // MLSTM_R
    nc = MLSTM_R // A_CHUNK
    assert CTX == MLSTM_R

    def rb(rev):
        def f(b, j):
            jj = j - 1
            return jnp.where(j == 0, NL // MLSTM_R + b, b * nb + (nb - 1 - jj if rev else jj))
        return f

    def stream_specs(rev):
        r = rb(rev)
        d = 2 if rev else 0
        col = lambda w, c: pl.BlockSpec((MLSTM_R, w), lambda b, j: (r(b, j), c))
        return [
            col(LANES, PA_Q // LANES),
            pl.BlockSpec((nc, LANES, A_CHUNK), lambda b, j: (r(b, j), 0, 0)),
            col(2 * LANES, PA_V // (2 * LANES)),
            col(LANES, PA_G // LANES + d), col(LANES, PA_G // LANES + d + 1),
            pl.BlockSpec((nc, 2 * A_HEADS, A_CHUNK), lambda b, j: (r(b, j), d // 2, 0)),
            pl.BlockSpec((nc, 2 * A_HEADS, A_CHUNK), lambda b, j: (r(b, j), 2 + d // 2, 0)),
        ]

    out_spec = lambda rev: pl.BlockSpec((MLSTM_R, A_HEADS * A_DV), lambda b, j: (rb(rev)(b, j), 0))
    return pl.pallas_call(
        _mlstm_kernel,
        grid=(B, nb + 1),
        in_specs=stream_specs(False) + stream_specs(True),
        out_specs=[out_spec(False), out_spec(True)],
        out_shape=[jax.ShapeDtypeStruct((N, A_HEADS * A_DV), f32)] * 2,
        scratch_shapes=[pltpu.VMEM((2, LANES, 3 * LANES), f32), pltpu.VMEM((2, 8, LANES), f32),
                        pltpu.VMEM((2, 8, LANES), f32)],
        compiler_params=_params(2),
        name="mlstm",
    )(*[pa, kt, pa, pa, pa, gt, gt] * 2)


def _tile_max(s, m128):
    for t in range(s.shape[1] // LANES):
        blk = s[:, LANES * t:LANES * (t + 1)]
        m128 = blk if m128 is None else jnp.maximum(m128, blk)
    return m128


def _mla_kernel(*refs, latent):
    if latent:
        q_ref, kc_ref, vc_ref, kl_ref, vl_ref, o_ref, s_ref = refs
        sources = [(kc_ref, vc_ref, 0, CTX)] + [(kl_ref, vl_ref, c, MLA_KC) for c in range(0, T, MLA_KC)]
    else:
        q_ref, kc_ref, vc_ref, _, o_ref, s_ref = refs
        sources = [(kc_ref, vc_ref, 0, CTX)]
    q = q_ref[...]
    lane = lax.broadcasted_iota(jnp.int32, (1, LANES), 1)
    row_max = []
    for hh in range(2):
        sl = slice(LANES * hh, LANES * (hh + 1))
        qh = q[:, sl]
        m128 = None
        off = 0
        for k_ref, _, r0, n in sources:
            s = _dot_nt(qh, k_ref[r0:r0 + n, sl])
            s_ref[hh, :, off:off + n] = s
            m128 = _tile_max(s, m128)
            off += n
        row_max.append(jnp.max(m128, axis=-1, keepdims=True))
    outs = []
    for hh in range(2):
        den_lane = B_DV if hh == 0 else 0
        acc = None
        off = 0
        for _, v_ref, r0, n in sources:
            p = jnp.exp2(s_ref[hh, :, off:off + n] - row_max[hh]).astype(bf16)
            vext = jnp.where(lane == den_lane, 1.0, v_ref[r0:r0 + n, :]).astype(bf16)
            part = _dot(p, vext)
            acc = part if acc is None else acc + part
            off += n
        outs.append(acc / acc[:, den_lane:den_lane + 1])
    o_ref[...] = jnp.where(lane < B_DV, outs[0], outs[1]).astype(o_ref.dtype)


def _mla(qb, kb, vb, prev=None):
    latent = prev is None
    npair = B_HEADS // 2
    ctx_blk = NL // CTX
    kv_specs = [
        pl.BlockSpec((CTX, 2 * LANES), lambda b, p, i: (ctx_blk + b, p)),
        pl.BlockSpec((CTX, LANES), lambda b, p, i: (ctx_blk + b, p)),
    ]
    if latent:
        tq = MLA_TQ
        nq = T // tq
        qmap = lambda b, p, i: (b * nq + i, p)
        kv_specs += [
            pl.BlockSpec((T, 2 * LANES), lambda b, p, i: (b, p)),
            pl.BlockSpec((T, LANES), lambda b, p, i: (b, p)),
        ]
        args = (qb, kb, vb, kb, vb)
        aliases = {}
        nkeys = CTX + T
    else:
        tq = CTX
        nq = 1
        qmap = lambda b, p, i: (ctx_blk + b, p)
        kv_specs += [pl.BlockSpec(memory_space=pl.ANY)]
        args = (qb, kb, vb, prev)
        aliases = {3: 0}
        nkeys = CTX
    return pl.pallas_call(
        functools.partial(_mla_kernel, latent=latent),
        grid=(B, npair, nq),
        in_specs=[pl.BlockSpec((tq, 2 * LANES), qmap)] + kv_specs,
        out_specs=pl.BlockSpec((tq, LANES), qmap),
        out_shape=jax.ShapeDtypeStruct((N, B_HEADS * B_DV), bf16),
        scratch_shapes=[pltpu.VMEM((2, tq, nkeys), f32)],
        input_output_aliases=aliases,
        compiler_params=_params(3),
        name="mla_latent" if latent else "mla_context",
    )(*args)


def _gqa_kernel(sink_ref, *refs, latent):
    if latent:
        q_ref, kc_ref, vc_ref, kl_ref, vl_ref, o_ref = refs
    else:
        q_ref, kc_ref, vc_ref, _, o_ref = refs
    q = q_ref[...]
    tq = q.shape[0]
    lane = lax.broadcasted_iota(jnp.int32, (1, LANES), 1)
    vc = vc_ref[...]
    if latent:
        n = pl.program_id(1)
        start = pl.multiple_of(jnp.clip(n * GQA_TQ - WINDOW, 0, T - GQA_BAND), WINDOW)
        kband = kl_ref[pl.ds(start, GQA_BAND), :]
        vband = vl_ref[pl.ds(start, GQA_BAND), :]
        qpos1 = n * GQA_TQ + lax.broadcasted_iota(jnp.int32, (tq, 1), 0)
        qpos = jnp.concatenate([qpos1] * C_GROUP, axis=0)
        kpos = start + lax.broadcasted_iota(jnp.int32, (1, GQA_BAND), 1)
        valid = jnp.abs(qpos - kpos) <= WINDOW
    heads = []

    def finish(kvh, sink, s_c, s_b, m):
        den_lane = C_DH * (1 - kvh)
        acc = _dot(jnp.exp2(s_c - m).astype(bf16), jnp.where(lane == den_lane, 1.0, vc).astype(bf16))
        if latent:
            acc = acc + _dot(jnp.exp2(s_b - m).astype(bf16), jnp.where(lane == den_lane, 1.0, vband).astype(bf16))
        o = acc / (jnp.exp2(sink - m) + acc[:, den_lane:den_lane + 1])
        heads.extend(o[g * tq:(g + 1) * tq, :] for g in range(C_GROUP))

    scored = []
    for kvh in range(C_KV_HEADS):
        sl = slice(LANES * kvh, LANES * (kvh + 1))
        qs = jnp.concatenate([q[:, LANES * (C_GROUP * kvh + g):LANES * (C_GROUP * kvh + g + 1)]
                              for g in range(C_GROUP)], axis=0)
        sink = jnp.concatenate([jnp.full((tq, 1), sink_ref[C_GROUP * kvh + g] * LOG2E, f32)
                                for g in range(C_GROUP)], axis=0)
        s_c = _dot_nt(qs, kc_ref[:, sl])
        m128 = _tile_max(s_c, None)
        s_b = None
        if latent:
            s_b = jnp.where(valid, _dot_nt(qs, kband[:, sl]), -jnp.inf)
            m128 = _tile_max(s_b, m128)
        m = jnp.maximum(sink, jnp.max(m128, axis=-1, keepdims=True))
        if latent:
            finish(kvh, sink, s_c, s_b, m)
        else:
            scored.append((kvh, sink, s_c, s_b, m))
    for job in scored:
        finish(*job)
    lo = lane < C_DH
    o_ref[:, 0:LANES] = jnp.where(lo, heads[0], pltpu.roll(heads[1], C_DH, 1)).astype(o_ref.dtype)
    o_ref[:, LANES:2 * LANES] = jnp.where(lo, heads[2], heads[3]).astype(o_ref.dtype)
    o_ref[:, 2 * LANES:3 * LANES] = jnp.where(lo, pltpu.roll(heads[4], C_DH, 1), heads[5]).astype(o_ref.dtype)


def _gqa(sink, qc, kc, vc, prev=None):
    latent = prev is None
    ctx_blk = NL // CTX
    kv_specs = [
        pl.BlockSpec((CTX, C_KV_HEADS * LANES), lambda b, i: (ctx_blk + b, 0)),
        pl.BlockSpec((CTX, LANES), lambda b, i: (ctx_blk + b, 0)),
    ]
    if latent:
        tq = GQA_TQ
        nq = T // tq
        qmap = lambda b, i: (b * nq + i, 0)
        kv_specs += [
            pl.BlockSpec((T, C_KV_HEADS * LANES), lambda b, i: (b, 0)),
            pl.BlockSpec((T, LANES), lambda b, i: (b, 0)),
        ]
        args = (sink, qc, kc, vc, kc, vc)
        aliases = {}
    else:
        tq = CTX
        nq = 1
        qmap = lambda b, i: (ctx_blk + b, 0)
        kv_specs += [pl.BlockSpec(memory_space=pl.ANY)]
        args = (sink, qc, kc, vc, prev)
        aliases = {4: 0}
    return pl.pallas_call(
        functools.partial(_gqa_kernel, latent=latent),
        grid=(B, nq),
        in_specs=[pl.BlockSpec(memory_space=pltpu.SMEM), pl.BlockSpec((tq, C_HEADS * LANES), qmap)] + kv_specs,
        out_specs=pl.BlockSpec((tq, C_HEADS * C_DH), qmap),
        out_shape=jax.ShapeDtypeStruct((N, C_HEADS * C_DH), bf16),
        input_output_aliases=aliases,
        compiler_params=_params(2),
        name="gqa_latent" if latent else "gqa_context",
    )(*args)


def _pad_cols(w, width):
    return jnp.pad(w, ((0, 0), (0, width - w.shape[1])))


def _place(w, lane_map):
    return jnp.where(lane_map[None, :] >= 0, w[:, np.maximum(lane_map, 0)], 0.0)


def _place_heads(w, heads, lane_map):
    r = w.shape[0]
    w = w.reshape(r, heads, -1)
    out = jnp.where(lane_map[None, None, :] >= 0, w[:, :, np.maximum(lane_map, 0)], 0.0)
    return out.reshape(r, heads * LANES)


def _arrange_w_in(w, map_b, map_c):
    o = np.cumsum((0, 128, 128, 256, 256, 16, 256, 128, 32, 384, 128, 128))
    part = lambda i: w[:, int(o[i]):int(o[i + 1])]
    kr = _place(part(7), np.where(map_b >= B_NOPE, map_b - B_NOPE, -1))
    return jnp.concatenate([
        part(0), part(2), part(3), _pad_cols(part(4), LANES),
        part(5), part(6), kr,
        _place_heads(part(8), C_HEADS, map_c), _place_heads(part(9), C_KV_HEADS, map_c), part(10),
    ], axis=1)


def _rope_tables(map_b, map_c):
    t = jnp.arange(T + TM_IN, dtype=jnp.int32)[:, None]
    live = t < T
    row = (t // GRID_W).astype(f32)
    col = (t % GRID_W).astype(f32)

    def tables(lane_map, rope_start, half):
        rel = lane_map - rope_start
        in_rope = (lane_map >= 0) & (rel >= 0) & (rel < 4 * half)
        rel = np.where(in_rope, rel, 0)
        use_col = jnp.asarray(rel >= 2 * half)[None, :]
        second = jnp.asarray((rel // half) % 2 == 1)[None, :]
        freq = ROPE_BASE ** (-jnp.asarray(rel % half, f32) / half)
        rot = jnp.asarray(in_rope)[None, :] & live
        ang = jnp.where(use_col, col, row) * freq[None, :]
        sin = jnp.sin(ang)
        return [jnp.where(rot, jnp.cos(ang), 1.0), jnp.where(rot, jnp.where(second, sin, -sin), 0.0)]

    return jnp.concatenate(tables(map_b, B_NOPE, B_ROPE // 4) + tables(map_c, 0, C_DH // 4), axis=1)


def kernel(x, c, ctx, c_ctx, ada_w, ada_b, norm_g, ffn1_wi, ffn1_wo, ffn2_wi, ffn2_wo, w_in, w_out,
           mlstm_gate_b, mlstm_out_norm, mla_cq_norm, mla_ckv_norm, mla_w_uq, mla_w_ukv, mla_q_norm, mla_k_norm,
           gqa_q_norm, gqa_k_norm, gqa_sink):
    map_b, map_c = _head_lane_map_b(), _head_lane_map_c()
    cc = jnp.concatenate([c, c_ctx[None, :], jnp.zeros((MOD_ROWS - B - 1, D), f32)], axis=0)
    mod_all = _ada(cc, ada_w, ada_b).reshape(DEPTH, MOD_ROWS, N_MOD, D)
    tab = _rope_tables(map_b, map_c)
    xs = (x.reshape(NL, D), ctx.reshape(NCX, D))

    for l in range(DEPTH):
        need_ctx = l < DEPTH - 1
        mod = mod_all[l]
        ng = norm_g[l]
        wig1, wiu1 = ffn1_wi[l, :, :D_FF].astype(bf16), ffn1_wi[l, :, D_FF:].astype(bf16)
        wig2, wiu2 = ffn2_wi[l, :, :D_FF].astype(bf16), ffn2_wi[l, :, D_FF:].astype(bf16)
        wo1, wo2 = ffn1_wo[l].astype(bf16), ffn2_wo[l].astype(bf16)
        win = _arrange_w_in(w_in[l], map_b, map_c).astype(bf16)
        bias = _pad_cols(jnp.pad(mlstm_gate_b[l][None], ((0, 0), (COL_AG, 0))), WP)
        gate_order = np.concatenate([np.arange(N_GATES), np.arange(N_GATES).reshape(4, A_HEADS)[[1, 0, 3, 2]].ravel()])
        wkg = jnp.concatenate([w_in[l][:, 128:256], w_in[l][:, 768 + gate_order]], axis=1).T.astype(bf16)
        wuq = _place_heads(mla_w_uq[l], B_HEADS, map_b).astype(bf16)
        ukv = mla_w_ukv[l].reshape(B_KV_RANK, B_HEADS, B_NOPE + B_DV)
        nope_map = np.where(map_b < B_NOPE, map_b, -1)
        wukv = jnp.concatenate([_place_heads(ukv[:, :, :B_NOPE].reshape(B_KV_RANK, -1), B_HEADS, nope_map),
                                ukv[:, :, B_NOPE:].reshape(B_KV_RANK, -1)], axis=1).astype(bf16)
        hg = jnp.concatenate([_place(mla_q_norm[l][None], map_b), _place(mla_k_norm[l][None], map_b),
                              _place(gqa_q_norm[l][None], map_c), _place(gqa_k_norm[l][None], map_c)])

        x1, h = _ffn1(xs, mod, ng, wig1, wiu1, wo1)
        pa, kt, gt, qb, kb, vb, qc, kc, vc = _inproj(
            h, win, bias, tab, mla_cq_norm[l][None], mla_ckv_norm[l][None], wuq, wukv, hg,
            wkg, mlstm_gate_b[l][gate_order][:, None])

        hf, hb = _mlstm(pa, kt, gt)

        yb = _mla(qb, kb, vb)
        yc = _gqa(gqa_sink[l], qc, kc, vc)
        if need_ctx:
            yb = _mla(qb, kb, vb, prev=yb)
            yc = _gqa(gqa_sink[l], qc, kc, vc, prev=yc)

        rows = N if need_ctx else NL
        xs = _outproj_ffn(x1, mod, ng, hf, hb, pa, yb, yc, mlstm_out_norm[l][None], w_out[l].astype(bf16),
                          wig2, wiu2, wo2, rows)
    return xs.reshape(B, T, D)
```

```python
import functools
import math

import jax
import jax.numpy as jnp
import numpy as np
from jax import lax
from jax.experimental import pallas as pl
from jax.experimental.pallas import tpu as pltpu

f32 = jnp.float32
bf16 = jnp.bfloat16

D = 1024
B = 4
T = 4096
CTX = 256
DEPTH = 2
GRID_W = 64
ROPE_BASE = 10000.0
EPS = 1e-6
HALF = 0.5
N_MOD = 9
D_FF = 2816
A_HEADS, A_DK, A_DV, A_CHUNK = 4, 32, 64, 64
B_HEADS, B_Q_RANK, B_KV_RANK, B_NOPE, B_ROPE, B_DV = 6, 256, 128, 64, 32, 64
B_DQK = B_NOPE + B_ROPE
C_HEADS, C_KV_HEADS, C_DH, WINDOW = 6, 2, 64, 128
C_GROUP = C_HEADS // C_KV_HEADS

NL = B * T
NCX = B * CTX
N = NL + NCX

LANES = 128
HALF_LANES = LANES // 2
MOD_ROWS = 8
VMEM_LIMIT = 56 * 1024 * 1024
LOG2E = math.log2(math.e)

TM_FFN = 512
FF_CHUNK = 256
TM_IN = 512
ADA_TN = 1152
MLSTM_R = 256
MLA_TQ = 512
MLA_KC = 512
GQA_TQ = 256
GQA_BAND = GQA_TQ + 2 * WINDOW

COL_AQ, COL_AV, COL_AO, COL_AG = 0, 128, 384, 640
COL_BCQ, COL_BCKV, COL_BKR = 768, 1024, 1152
COL_CQ, COL_CK, COL_CV = 1280, 2048, 2304
WP = 2432
PA_V, PA_O, PA_Q, PA_G = 0, 256, 512, 640
PA_W = PA_G + 4 * LANES
N_GATES = 4 * A_HEADS


def _head_lane_map_b():
    m = -np.ones(LANES, np.int64)
    m[0:8], m[8:16], m[16:64] = np.arange(64, 72), np.arange(80, 88), np.arange(0, 48)
    m[64:72], m[72:80], m[80:96] = np.arange(72, 80), np.arange(88, 96), np.arange(48, 64)
    return m


def _head_lane_map_c():
    m = -np.ones(LANES, np.int64)
    m[0:16], m[16:32] = np.arange(0, 16), np.arange(32, 48)
    m[64:80], m[80:96] = np.arange(16, 32), np.arange(48, 64)
    return m


def _sigmoid(x):
    return 1.0 / (1.0 + jnp.exp(-x))


def _log_sigmoid(x):
    return jnp.minimum(x, 0.0) - jnp.log(1.0 + jnp.exp(-jnp.abs(x)))


def _rms(x, g):
    ms = jnp.mean(x * x, axis=-1, keepdims=True)
    return x * lax.rsqrt(ms + EPS) * g


def _dot(a, b):
    return jnp.dot(a, b, preferred_element_type=f32)


def _dot_nt(a, b):
    return lax.dot_general(a, b, (((1,), (1,)), ((), ())), preferred_element_type=f32)


def _split3(x):
    hi = x.astype(bf16)
    r1 = x - hi.astype(f32)
    mid = r1.astype(bf16)
    return hi, mid, (r1 - mid.astype(f32)).astype(bf16)


def _layer(arr, l):
    rest = (0,) * (arr.ndim - 1)
    return pl.BlockSpec((None,) + arr.shape[1:], lambda *_: (l,) + rest, pipeline_mode=pl.Buffered(1))


def _mod_spec(l, tm):
    tpb = T // tm
    return pl.BlockSpec((None, 1, N_MOD, D), lambda i: (l, i // tpb, 0, 0))


def _params(n_axes):
    return pltpu.CompilerParams(dimension_semantics=("arbitrary",) * n_axes, vmem_limit_bytes=VMEM_LIMIT)


def _ada_kernel(c_ref, w_ref, b_ref, o_ref):
    c = c_ref[...]
    s = c * _sigmoid(c)
    o_ref[0] = jnp.dot(s, w_ref[0], preferred_element_type=f32, precision=lax.Precision.HIGHEST) + b_ref[0]


def _ada(cc, ada_w, ada_b):
    nt = (N_MOD * D) // ADA_TN
    return pl.pallas_call(
        _ada_kernel,
        grid=(DEPTH, nt),
        in_specs=[
            pl.BlockSpec((MOD_ROWS, D), lambda l, j: (0, 0)),
            pl.BlockSpec((1, D, ADA_TN), lambda l, j: (l, 0, j)),
            pl.BlockSpec((1, 1, ADA_TN), lambda l, j: (l, 0, j)),
        ],
        out_specs=pl.BlockSpec((1, MOD_ROWS, ADA_TN), lambda l, j: (l, 0, j)),
        out_shape=jax.ShapeDtypeStruct((DEPTH, MOD_ROWS, N_MOD * D), f32),
        compiler_params=_params(2),
        name="ada_mod",
    )(cc, ada_w, ada_b.reshape(DEPTH, 1, N_MOD * D))


def _ffn(x, g, shift, scale, gate, wi_ref, wo_ref):
    h = (_rms(x, g) * (1.0 + scale) + shift).astype(bf16)
    acc = None
    for c in range(D_FF // FF_CHUNK):
        lo, hi = c * FF_CHUNK, (c + 1) * FF_CHUNK
        gt = _dot(h, wi_ref[:, lo:hi])
        up = _dot(h, wi_ref[:, D_FF + lo:D_FF + hi])
        a = (gt * _sigmoid(gt) * up).astype(bf16)
        part = _dot(a, wo_ref[lo:hi, :])
        acc = part if acc is None else acc + part
    return x + HALF * gate * acc


def _ffn1_kernel(*refs, split_input):
    if split_input:
        xl_ref, xc_ref, mod_ref, ng_ref, wi_ref, wo_ref, x1_ref, h_ref = refs
        x = jnp.where(pl.program_id(0) < NL // TM_FFN, xl_ref[...], xc_ref[...])
    else:
        x_ref, mod_ref, ng_ref, wi_ref, wo_ref, x1_ref, h_ref = refs
        x = x_ref[...]
    mod = mod_ref[0]
    x1 = _ffn(x, ng_ref[0:1, :], mod[0:1, :], mod[1:2, :], mod[2:3, :], wi_ref, wo_ref)
    x1_ref[...] = x1
    h_ref[...] = (_rms(x1, ng_ref[1:2, :]) * (1.0 + mod[4:5, :]) + mod[3:4, :]).astype(bf16)


def _ffn1(l, xs, mod, ng, wi, wo):
    split_input = isinstance(xs, tuple)
    nlt = NL // TM_FFN
    if split_input:
        assert NCX % TM_FFN == 0
        x_specs = [pl.BlockSpec((TM_FFN, D), lambda i: (jnp.minimum(i, nlt - 1), 0)),
                   pl.BlockSpec((TM_FFN, D), lambda i: (jnp.maximum(i - nlt, 0), 0))]
    else:
        xs = (xs,)
        x_specs = [pl.BlockSpec((TM_FFN, D), lambda i: (i, 0))]
    return pl.pallas_call(
        functools.partial(_ffn1_kernel, split_input=split_input),
        grid=(N // TM_FFN,),
        in_specs=x_specs + [_mod_spec(l, TM_FFN), _layer(ng, l), _layer(wi, l), _layer(wo, l)],
        out_specs=[pl.BlockSpec((TM_FFN, D), lambda i: (i, 0))] * 2,
        out_shape=[jax.ShapeDtypeStruct((N, D), f32), jax.ShapeDtypeStruct((N, D), bf16)],
        compiler_params=_params(1),
        name="ffn1",
    )(*xs, mod, ng, wi, wo)


def _outproj_ffn_kernel(*refs, with_ctx):
    if with_ctx:
        (x_ref, mod_ref, ng_ref, hf_ref, hb_ref, o_ref, ybl_ref, ycl_ref, ybc_ref, ycc_ref, on_ref, wout_ref,
         wi_ref, wo_ref, out_ref) = refs
        is_latent = pl.program_id(0) < NL // TM_FFN
        yb = jnp.where(is_latent, ybl_ref[...], ybc_ref[...])
        yc = jnp.where(is_latent, ycl_ref[...], ycc_ref[...])
    else:
        (x_ref, mod_ref, ng_ref, hf_ref, hb_ref, o_ref, yb_ref, yc_ref, on_ref, wout_ref,
         wi_ref, wo_ref, out_ref) = refs
        yb, yc = yb_ref[...], yc_ref[...]
    x = x_ref[...]
    mod = mod_ref[0]
    hs = hf_ref[...] + hb_ref[...]
    sq = hs * hs
    head = lax.broadcasted_iota(jnp.int32, (1, A_HEADS * A_DV), 1) // A_DV
    ms = jnp.zeros_like(hs)
    for hh in range(A_HEADS):
        sel = head == hh
        ssh = jnp.sum(jnp.where(sel, sq, 0.0), axis=-1, keepdims=True) * (1.0 / A_DV)
        ms = jnp.where(sel, ssh, ms)
    ya = _sigmoid(o_ref[...]) * (hs * lax.rsqrt(ms + EPS) * on_ref[...])
    y = jnp.concatenate([ya.astype(bf16), yb, yc], axis=-1)
    x2 = x + mod[5:6, :] * _dot(y, wout_ref[...])
    out_ref[...] = _ffn(x2, ng_ref[2:3, :], mod[6:7, :], mod[7:8, :], mod[8:9, :], wi_ref, wo_ref)


def _outproj_ffn(l, x1, mod, ng, hf, hb, pa, y_lat, y_ctx, onorm, wout, wi, wo):
    with_ctx = y_ctx is not None
    rows = N if with_ctx else NL
    nlt = NL // TM_FFN
    row = lambda w, c=0: pl.BlockSpec((TM_FFN, w), lambda i: (i, c))
    lat = lambda w: pl.BlockSpec((TM_FFN, w), lambda i: (jnp.minimum(i, nlt - 1), 0))
    ctx = lambda w: pl.BlockSpec((TM_FFN, w), lambda i: (jnp.maximum(i - nlt, 0), 0))
    y_specs = [lat(B_HEADS * B_DV), lat(C_HEADS * C_DH)]
    if with_ctx:
        y_specs += [ctx(B_HEADS * B_DV), ctx(C_HEADS * C_DH)]
    return pl.pallas_call(
        functools.partial(_outproj_ffn_kernel, with_ctx=with_ctx),
        grid=(rows // TM_FFN,),
        in_specs=[
            row(D), _mod_spec(l, TM_FFN), _layer(ng, l),
            row(A_HEADS * A_DV), row(A_HEADS * A_DV), row(A_HEADS * A_DV, PA_O // (A_HEADS * A_DV)),
        ] + y_specs + [_layer(onorm, l), _layer(wout, l), _layer(wi, l), _layer(wo, l)],
        out_specs=row(D),
        out_shape=jax.ShapeDtypeStruct((rows, D), f32),
        compiler_params=_params(1),
        name="outproj_ffn2",
    )(x1, mod, ng, hf, hb, pa, *y_lat, *(y_ctx or ()), onorm, wout, wi, wo)


def _inproj_kernel(h_ref, win_ref, bias_ref, tab_ref, cqn_ref, ckvn_ref, wuq_ref, wukv_ref, hg_ref,
                   wkg_ref, gb_ref,
                   pa_ref, kt_ref, gt_ref, qb_ref, kb_ref, vb_ref, qc_ref, kc_ref, vc_ref):
    h = h_ref[...]
    p = _dot(h, win_ref[...]) + bias_ref[...]

    pa_ref[:, PA_V:PA_Q] = p[:, COL_AV:COL_AG]
    pa_ref[:, PA_Q:PA_G] = p[:, COL_AQ:COL_AV] * (A_DK ** -0.5)
    graw = p[:, COL_AG:COL_AG + LANES]
    lane = lax.broadcasted_iota(jnp.int32, (1, LANES), 1)
    for kk in range(4):
        gk = graw if kk == 0 else pltpu.roll(graw, LANES - A_HEADS * kk, 1)
        if kk % 2 == 1:
            gk = _log_sigmoid(gk)
        pa_ref[:, PA_G + LANES * kk:PA_G + LANES * (kk + 1)] = jnp.where(lane < A_HEADS, gk, 0.0)

    grow = lax.broadcasted_iota(jnp.int32, (2 * N_GATES, 1), 0)
    is_forget = ((grow // A_HEADS) % 2 == 1) == (grow < N_GATES)
    for c in range(TM_IN // A_CHUNK):
        t = _dot_nt(wkg_ref[...], h[c * A_CHUNK:(c + 1) * A_CHUNK, :])
        kt_ref[c] = t[0:LANES, :]
        g = t[LANES:LANES + 2 * N_GATES, :] + gb_ref[...]
        gt_ref[c] = jnp.where(is_forget, _log_sigmoid(g), g)

    tab = tab_ref[...]
    tb = [tab[:, LANES * i:LANES * (i + 1)] for i in range(4)]
    hg = hg_ref[...]

    cq = _rms(p[:, COL_BCQ:COL_BCKV], cqn_ref[...]).astype(bf16)
    ckv = _rms(p[:, COL_BCKV:COL_BKR], ckvn_ref[...]).astype(bf16)
    kr = p[:, COL_BKR:COL_CQ]
    q = _dot(cq, wuq_ref[...])
    kv = _dot(ckv, wukv_ref[...])
    vb_ref[...] = kv[:, B_HEADS * LANES:].astype(bf16)
    vc_ref[...] = p[:, COL_CV:COL_CV + LANES].astype(bf16)

    two = lambda a: jnp.concatenate([a, a], axis=1)
    pair = lambda a, j, first=0: a[:, first + 2 * LANES * j:first + 2 * LANES * (j + 1)]
    gb_q, gb_k, gc_q, gc_k = (two(hg[i:i + 1, :]) for i in range(4))
    cos_b, sin_b, cos_c, sin_c = (two(t) for t in tb)
    kr2 = two(kr)
    jobs = []
    for j in range(B_HEADS // 2):
        jobs.append((pair(q, j), gb_q, B_DQK, cos_b, sin_b, B_DQK ** -0.5 * LOG2E, qb_ref, j))
        jobs.append((pair(kv, j) + kr2, gb_k, B_DQK, cos_b, sin_b, None, kb_ref, j))
    for j in range(C_HEADS // 2):
        jobs.append((pair(p, j, COL_CQ), gc_q, C_DH, cos_c, sin_c, C_DH ** -0.5 * LOG2E, qc_ref, j))
    jobs.append((pair(p, 0, COL_CK), gc_k, C_DH, cos_c, sin_c, None, kc_ref, 0))
    ri = lax.broadcasted_iota(jnp.int32, (2 * LANES, 2 * LANES), 0)
    ci = lax.broadcasted_iota(jnp.int32, (2 * LANES, 2 * LANES), 1)
    swap_halves = ((ri // LANES == ci // LANES) & (ri % LANES == (ci + HALF_LANES) % LANES)).astype(bf16)
    lane2 = lax.broadcasted_iota(jnp.int32, (1, 2 * LANES), 1)
    sums = []
    for x, *_ in jobs:
        sq = x * x
        s0 = jnp.sum(sq[:, 0:LANES], axis=-1, keepdims=True)
        s1 = jnp.sum(sq[:, LANES:], axis=-1, keepdims=True)
        sums.append(jnp.where(lane2 < LANES, s0, s1))
    normed = [x * lax.rsqrt(ss * (1.0 / n_real) + EPS) * gain for (x, gain, n_real, *_), ss in zip(jobs, sums)]
    rolled = [_dot(y.astype(bf16), swap_halves) for y in normed]
    for (_, _, _, cos, sin, scale, dst_ref, j), y, yr in zip(jobs, normed, rolled):
        out = y * cos + yr * sin
        if scale is not None:
            out = out * scale
        dst_ref[:, 2 * LANES * j:2 * LANES * (j + 1)] = out.astype(bf16)


def _inproj(l, h, tab, win, bias, cqn, ckvn, wuq, wukv, hg, wkg, gb):
    tpb = T // TM_IN
    row = lambda w: pl.BlockSpec((TM_IN, w), lambda i: (i, 0))
    chunked = lambda r: pl.BlockSpec((TM_IN // A_CHUNK, r, A_CHUNK), lambda i: (i, 0, 0))
    out_w = [(B_HEADS * LANES, bf16), (B_HEADS * LANES, bf16), (B_HEADS * B_DV, bf16),
             (C_HEADS * LANES, bf16), (C_KV_HEADS * LANES, bf16), (C_KV_HEADS * C_DH, bf16)]
    return pl.pallas_call(
        _inproj_kernel,
        grid=(N // TM_IN,),
        in_specs=[
            row(D), _layer(win, l), _layer(bias, l),
            pl.BlockSpec((TM_IN, 4 * LANES), lambda i: (jnp.where(i < NL // TM_IN, i % tpb, tpb), 0)),
            _layer(cqn, l), _layer(ckvn, l), _layer(wuq, l), _layer(wukv, l),
            _layer(hg, l), _layer(wkg, l), _layer(gb, l),
        ],
        out_specs=[row(PA_W), chunked(LANES), chunked(2 * N_GATES)] + [row(w) for w, _ in out_w],
        out_shape=[jax.ShapeDtypeStruct((N, PA_W), f32),
                   jax.ShapeDtypeStruct((N // A_CHUNK, LANES, A_CHUNK), f32),
                   jax.ShapeDtypeStruct((N // A_CHUNK, 2 * N_GATES, A_CHUNK), f32)]
        + [jax.ShapeDtypeStruct((N, w), dt) for w, dt in out_w],
        compiler_params=_params(1),
        name="inproj",
    )(h, win, bias, tab, cqn, ckvn, wuq, wukv, hg, wkg, gb)


def _cummax_rows(x, rev):
    n = x.shape[0]
    row = lax.broadcasted_iota(jnp.int32, (n, 1), 0)
    sh = 1
    while sh < n:
        if rev:
            x = jnp.maximum(x, jnp.where(row < n - sh, pltpu.roll(x, n - sh, 0), -jnp.inf))
        else:
            x = jnp.maximum(x, jnp.where(row >= sh, pltpu.roll(x, sh, 0), -jnp.inf))
        sh *= 2
    return x


def _stack_heads(pieces):
    return jnp.concatenate(pieces, axis=0)


def _mlstm_kernel(*refs):
    streams = [(refs[0:7] + refs[14:15], False), (refs[7:14] + refs[15:16], True)]
    s_ref, ml_ref, ms_ref = refs[16:19]

    @pl.when(pl.program_id(1) == 0)
    def _():
        s_ref[...] = jnp.zeros_like(s_ref)
        ml_ref[...] = jnp.zeros_like(ml_ref)
        ms_ref[...] = jnp.zeros_like(ms_ref)

    L = A_CHUNK
    n_chunks = MLSTM_R // L
    heads = range(A_HEADS)
    ti = lax.broadcasted_iota(jnp.int32, (L, L), 0)
    si = lax.broadcasted_iota(jnp.int32, (L, L), 1)
    lane = lax.broadcasted_iota(jnp.int32, (1, LANES), 1)
    row8 = lax.broadcasted_iota(jnp.int32, (2 * A_HEADS, 1), 0)
    in_head = [(lane >= hh * A_DK) & (lane < (hh + 1) * A_DK) for hh in heads]
    ones_blk = jnp.ones((L, LANES), bf16)

    items = []
    for sidx, (srefs, rev) in enumerate(streams):
        q_ref, kt_ref, v_ref, ig_ref, lf_ref, gt_ref, gts_ref, h_ref = srefs
        attend = (si >= ti) if rev else (si <= ti)
        attend4 = _stack_heads([attend] * A_HEADS)
        cum_cols = attend.astype(bf16)
        cum_rows = ((ti >= si) if rev else (ti <= si)).astype(bf16)
        m_lane = ml_ref[sidx, 0:1, :]
        m_sub = ms_ref[sidx, :, 0:1]
        for cc in (range(n_chunks - 1, -1, -1) if rev else range(n_chunks)):
            rows = slice(cc * L, (cc + 1) * L)
            ig = ig_ref[rows, :]
            lf = lf_ref[rows, :]
            gt = gt_ref[cc]
            gts = gts_ref[cc]
            b_col = sum(_dot(cum_cols, piece) for piece in _split3(lf))
            r_col = ig - b_col
            big_m = jnp.maximum(m_lane, _cummax_rows(r_col, rev))
            mt_col = b_col + big_m
            b_last_l = jnp.sum(lf, axis=0, keepdims=True)
            m_new_l = jnp.maximum(m_lane, jnp.max(r_col, axis=0, keepdims=True)) + b_last_l
            b_rows = sum(_dot(piece, cum_rows) for piece in _split3(gts))
            live = row8 < A_HEADS
            r8 = jnp.where(live, gt - b_rows, 0.0)
            b_last_s = jnp.where(live, jnp.sum(gts, axis=-1, keepdims=True), 0.0)
            r_max_s = jnp.max(r8, axis=-1, keepdims=True)
            wg8 = jnp.exp(r8 - r_max_s)
            m_new_s = jnp.maximum(m_sub, r_max_s) + b_last_s
            decay_s = jnp.exp(b_last_s + m_sub - m_new_s)
            scale_s = jnp.exp(b_last_s + r_max_s - m_new_s)
            expand = lambda a, n: _stack_heads([jnp.broadcast_to(a[hh:hh + 1, :], (n, a.shape[1])) for hh in heads])
            q = q_ref[rows, :]
            big_m_b = _stack_heads([jnp.broadcast_to(big_m[:, hh:hh + 1], (L, LANES)) for hh in heads])
            mt_b = _stack_heads([jnp.broadcast_to(mt_col[:, hh:hh + 1], (L, LANES)) for hh in heads])
            m_old_b = _stack_heads([jnp.broadcast_to(m_lane[:, hh:hh + 1], (L, LANES)) for hh in heads])
            items.append(dict(
                sidx=sidx, rows=rows, h_ref=h_ref,
                qst=_stack_heads([jnp.where(in_head[hh], q, 0.0) for hh in heads]).astype(bf16),
                kt=kt_ref[cc].astype(bf16),
                kw=(kt_ref[cc] * expand(wg8, A_DK)).astype(bf16),
                vo=jnp.concatenate([v_ref[rows, :].astype(bf16), ones_blk], axis=1),
                w=jnp.exp(jnp.where(attend4, expand(r8, L) - big_m_b[:, 0:L], -jnp.inf)),
                a_inter=jnp.exp(m_old_b - big_m_b), floor=jnp.exp(-mt_b),
                decay=jnp.broadcast_to(expand(decay_s, A_DK), (LANES, LANES)),
                kv_scale=jnp.broadcast_to(expand(scale_s, A_DK), (LANES, LANES))))
            m_lane, m_sub = m_new_l, m_new_s
        ml_ref[sidx, 0:1, :] = m_lane
        ms_ref[sidx, :, 0:1] = m_sub

    for it in items:
        it["s"] = _dot(it["qst"], it["kt"])

    for it in items:
        it["kv"] = _dot(it["kw"], it["vo"])

    for it in items:
        it["p"] = (it["s"] * it["w"]).astype(bf16)

    state = [s_ref[0], s_ref[1]]
    tile3 = lambda a: jnp.concatenate([a] * 3, axis=1)
    for it in items:
        st = state[it["sidx"]]
        it["c_in"] = st.astype(bf16)
        state[it["sidx"]] = tile3(it["decay"]) * st + tile3(it["kv_scale"]) * it["kv"]
    s_ref[0] = state[0]
    s_ref[1] = state[1]

    nv = A_HEADS * A_DV
    for it in items:
        out = tile3(it["a_inter"]) * _dot(it["qst"], it["c_in"]) + _dot(it["p"], it["vo"])
        res = out[:, 0:nv] / tile3(jnp.maximum(jnp.abs(out[:, nv:]), it["floor"]))[:, 0:nv]
        for pair in range(A_HEADS // 2):
            sl = slice(LANES * pair, LANES * (pair + 1))
            even = res[L * 2 * pair:L * (2 * pair + 1), sl]
            odd = res[L * (2 * pair + 1):L * (2 * pair + 2), sl]
            it["h_ref"][it["rows"], sl] = jnp.where(lane < A_DV, even, odd)


def _mlstm(pa, kt, gt):
    nb = T // MLSTM_R
    nc = MLSTM_R // A_CHUNK
    assert CTX == MLSTM_R

    def rb(rev):
        def f(b, j):
            jj = j - 1
            return jnp.where(j == 0, NL // MLSTM_R + b, b * nb + (nb - 1 - jj if rev else jj))
        return f

    def stream_specs(rev):
        r = rb(rev)
        d = 2 if rev else 0
        col = lambda w, c: pl.BlockSpec((MLSTM_R, w), lambda b, j: (r(b, j), c))
        return [
            col(LANES, PA_Q // LANES),
            pl.BlockSpec((nc, LANES, A_CHUNK), lambda b, j: (r(b, j), 0, 0)),
            col(2 * LANES, PA_V // (2 * LANES)),
            col(LANES, PA_G // LANES + d), col(LANES, PA_G // LANES + d + 1),
            pl.BlockSpec((nc, 2 * A_HEADS, A_CHUNK), lambda b, j: (r(b, j), d // 2, 0)),
            pl.BlockSpec((nc, 2 * A_HEADS, A_CHUNK), lambda b, j: (r(b, j), 2 + d // 2, 0)),
        ]

    out_spec = lambda rev: pl.BlockSpec((MLSTM_R, A_HEADS * A_DV), lambda b, j: (rb(rev)(b, j), 0))
    return pl.pallas_call(
        _mlstm_kernel,
        grid=(B, nb + 1),
        in_specs=stream_specs(False) + stream_specs(True),
        out_specs=[out_spec(False), out_spec(True)],
        out_shape=[jax.ShapeDtypeStruct((N, A_HEADS * A_DV), f32)] * 2,
        scratch_shapes=[pltpu.VMEM((2, LANES, 3 * LANES), f32), pltpu.VMEM((2, 8, LANES), f32),
                        pltpu.VMEM((2, 8, LANES), f32)],
        compiler_params=_params(2),
        name="mlstm",
    )(*[pa, kt, pa, pa, pa, gt, gt] * 2)


def _tile_max(s, m128):
    for t in range(s.shape[1] // LANES):
        blk = s[:, LANES * t:LANES * (t + 1)]
        m128 = blk if m128 is None else jnp.maximum(m128, blk)
    return m128


def _mla_kernel(*refs, latent):
    if latent:
        q_ref, kc_ref, vc_ref, kl_ref, vl_ref, o_ref, s_ref = refs
        sources = [(kc_ref, vc_ref, 0, CTX)] + [(kl_ref, vl_ref, c, MLA_KC) for c in range(0, T, MLA_KC)]
    else:
        q_ref, kc_ref, vc_ref, o_ref, s_ref = refs
        sources = [(kc_ref, vc_ref, 0, CTX)]
    q = q_ref[...]
    lane = lax.broadcasted_iota(jnp.int32, (1, LANES), 1)
    row_max = []
    for hh in range(2):
        sl = slice(LANES * hh, LANES * (hh + 1))
        qh = q[:, sl]
        m128 = None
        off = 0
        for k_ref, _, r0, n in sources:
            s = _dot_nt(qh, k_ref[r0:r0 + n, sl])
            s_ref[hh, :, off:off + n] = s
            m128 = _tile_max(s, m128)
            off += n
        row_max.append(jnp.max(m128, axis=-1, keepdims=True))
    outs = []
    for hh in range(2):
        den_lane = B_DV if hh == 0 else 0
        acc = None
        off = 0
        for _, v_ref, r0, n in sources:
            p = jnp.exp2(s_ref[hh, :, off:off + n] - row_max[hh]).astype(bf16)
            vext = jnp.where(lane == den_lane, 1.0, v_ref[r0:r0 + n, :]).astype(bf16)
            part = _dot(p, vext)
            acc = part if acc is None else acc + part
            off += n
        outs.append(acc / acc[:, den_lane:den_lane + 1])
    o_ref[...] = jnp.where(lane < B_DV, outs[0], outs[1]).astype(o_ref.dtype)


def _mla(qb, kb, vb, latent):
    npair = B_HEADS // 2
    ctx_blk = NL // CTX
    kv_specs = [
        pl.BlockSpec((CTX, 2 * LANES), lambda b, p, i: (ctx_blk + b, p)),
        pl.BlockSpec((CTX, LANES), lambda b, p, i: (ctx_blk + b, p)),
    ]
    if latent:
        tq = MLA_TQ
        nq = T // tq
        qmap = omap = lambda b, p, i: (b * nq + i, p)
        kv_specs += [
            pl.BlockSpec((T, 2 * LANES), lambda b, p, i: (b, p)),
            pl.BlockSpec((T, LANES), lambda b, p, i: (b, p)),
        ]
        args = (qb, kb, vb, kb, vb)
        nkeys = CTX + T
    else:
        tq = CTX
        nq = 1
        qmap = lambda b, p, i: (ctx_blk + b, p)
        omap = lambda b, p, i: (b, p)
        args = (qb, kb, vb)
        nkeys = CTX
    return pl.pallas_call(
        functools.partial(_mla_kernel, latent=latent),
        grid=(B, npair, nq),
        in_specs=[pl.BlockSpec((tq, 2 * LANES), qmap)] + kv_specs,
        out_specs=pl.BlockSpec((tq, LANES), omap),
        out_shape=jax.ShapeDtypeStruct((NL if latent else NCX, B_HEADS * B_DV), bf16),
        scratch_shapes=[pltpu.VMEM((2, tq, nkeys), f32)],
        compiler_params=_params(3),
        name="mla_latent" if latent else "mla_context",
    )(*args)


def _gqa_kernel(sink_ref, *refs, latent):
    if latent:
        q_ref, kc_ref, vc_ref, kl_ref, vl_ref, o_ref = refs
    else:
        q_ref, kc_ref, vc_ref, o_ref = refs
    q = q_ref[...]
    tq = q.shape[0]
    lane = lax.broadcasted_iota(jnp.int32, (1, LANES), 1)
    vc = vc_ref[...]
    if latent:
        n = pl.program_id(1)
        start = pl.multiple_of(jnp.clip(n * GQA_TQ - WINDOW, 0, T - GQA_BAND), WINDOW)
        kband = kl_ref[pl.ds(start, GQA_BAND), :]
        vband = vl_ref[pl.ds(start, GQA_BAND), :]
        qpos1 = n * GQA_TQ + lax.broadcasted_iota(jnp.int32, (tq, 1), 0)
        qpos = jnp.concatenate([qpos1] * C_GROUP, axis=0)
        kpos = start + lax.broadcasted_iota(jnp.int32, (1, GQA_BAND), 1)
        valid = jnp.abs(qpos - kpos) <= WINDOW
    heads = []

    def finish(kvh, sink, s_c, s_b, m):
        den_lane = C_DH * (1 - kvh)
        acc = _dot(jnp.exp2(s_c - m).astype(bf16), jnp.where(lane == den_lane, 1.0, vc).astype(bf16))
        if latent:
            acc = acc + _dot(jnp.exp2(s_b - m).astype(bf16), jnp.where(lane == den_lane, 1.0, vband).astype(bf16))
        o = acc / (jnp.exp2(sink - m) + acc[:, den_lane:den_lane + 1])
        heads.extend(o[g * tq:(g + 1) * tq, :] for g in range(C_GROUP))

    scored = []
    for kvh in range(C_KV_HEADS):
        sl = slice(LANES * kvh, LANES * (kvh + 1))
        qs = jnp.concatenate([q[:, LANES * (C_GROUP * kvh + g):LANES * (C_GROUP * kvh + g + 1)]
                              for g in range(C_GROUP)], axis=0)
        sink = jnp.concatenate([jnp.full((tq, 1), sink_ref[C_GROUP * kvh + g] * LOG2E, f32)
                                for g in range(C_GROUP)], axis=0)
        s_c = _dot_nt(qs, kc_ref[:, sl])
        m128 = _tile_max(s_c, None)
        s_b = None
        if latent:
            s_b = jnp.where(valid, _dot_nt(qs, kband[:, sl]), -jnp.inf)
            m128 = _tile_max(s_b, m128)
        m = jnp.maximum(sink, jnp.max(m128, axis=-1, keepdims=True))
        if latent:
            finish(kvh, sink, s_c, s_b, m)
        else:
            scored.append((kvh, sink, s_c, s_b, m))
    for job in scored:
        finish(*job)
    lo = lane < C_DH
    o_ref[:, 0:LANES] = jnp.where(lo, heads[0], pltpu.roll(heads[1], C_DH, 1)).astype(o_ref.dtype)
    o_ref[:, LANES:2 * LANES] = jnp.where(lo, heads[2], heads[3]).astype(o_ref.dtype)
    o_ref[:, 2 * LANES:3 * LANES] = jnp.where(lo, pltpu.roll(heads[4], C_DH, 1), heads[5]).astype(o_ref.dtype)


def _gqa(sink, qc, kc, vc, latent):
    ctx_blk = NL // CTX
    kv_specs = [
        pl.BlockSpec((CTX, C_KV_HEADS * LANES), lambda b, i: (ctx_blk + b, 0)),
        pl.BlockSpec((CTX, LANES), lambda b, i: (ctx_blk + b, 0)),
    ]
    if latent:
        tq = GQA_TQ
        nq = T // tq
        qmap = omap = lambda b, i: (b * nq + i, 0)
        kv_specs += [
            pl.BlockSpec((T, C_KV_HEADS * LANES), lambda b, i: (b, 0)),
            pl.BlockSpec((T, LANES), lambda b, i: (b, 0)),
        ]
        args = (sink, qc, kc, vc, kc, vc)
    else:
        tq = CTX
        nq = 1
        qmap = lambda b, i: (ctx_blk + b, 0)
        omap = lambda b, i: (b, 0)
        args = (sink, qc, kc, vc)
    return pl.pallas_call(
        functools.partial(_gqa_kernel, latent=latent),
        grid=(B, nq),
        in_specs=[pl.BlockSpec(memory_space=pltpu.SMEM), pl.BlockSpec((tq, C_HEADS * LANES), qmap)] + kv_specs,
        out_specs=pl.BlockSpec((tq, C_HEADS * C_DH), omap),
        out_shape=jax.ShapeDtypeStruct((NL if latent else NCX, C_HEADS * C_DH), bf16),        compiler_params=_params(2),
        name="gqa_latent" if latent else "gqa_context",
    )(*args)


def _pad_cols(w, width):
    return jnp.pad(w, ((0, 0), (0, width - w.shape[1])))


def _place(w, lane_map):
    return jnp.where(lane_map[None, :] >= 0, w[:, np.maximum(lane_map, 0)], 0.0)


def _place_heads(w, heads, lane_map):
    r = w.shape[0]
    w = w.reshape(r, heads, -1)
    out = jnp.where(lane_map[None, None, :] >= 0, w[:, :, np.maximum(lane_map, 0)], 0.0)
    return out.reshape(r, heads * LANES)


def _arrange_w_in(w, map_b, map_c):
    o = np.cumsum((0, 128, 128, 256, 256, 16, 256, 128, 32, 384, 128, 128))
    part = lambda i: w[:, int(o[i]):int(o[i + 1])]
    kr = _place(part(7), np.where(map_b >= B_NOPE, map_b - B_NOPE, -1))
    return jnp.concatenate([
        part(0), part(2), part(3), _pad_cols(part(4), LANES),
        part(5), part(6), kr,
        _place_heads(part(8), C_HEADS, map_c), _place_heads(part(9), C_KV_HEADS, map_c), part(10),
    ], axis=1)


def _rope_tables(map_b, map_c):
    t = jnp.arange(T + TM_IN, dtype=jnp.int32)[:, None]
    live = t < T
    row = (t // GRID_W).astype(f32)
    col = (t % GRID_W).astype(f32)

    def tables(lane_map, rope_start, half):
        rel = lane_map - rope_start
        in_rope = (lane_map >= 0) & (rel >= 0) & (rel < 4 * half)
        rel = np.where(in_rope, rel, 0)
        use_col = jnp.asarray(rel >= 2 * half)[None, :]
        second = jnp.asarray((rel // half) % 2 == 1)[None, :]
        freq = ROPE_BASE ** (-jnp.asarray(rel % half, f32) / half)
        rot = jnp.asarray(in_rope)[None, :] & live
        ang = jnp.where(use_col, col, row) * freq[None, :]
        sin = jnp.sin(ang)
        return [jnp.where(rot, jnp.cos(ang), 1.0), jnp.where(rot, jnp.where(second, sin, -sin), 0.0)]

    return jnp.concatenate(tables(map_b, B_NOPE, B_ROPE // 4) + tables(map_c, 0, C_DH // 4), axis=1)


def kernel(x, c, ctx, c_ctx, ada_w, ada_b, norm_g, ffn1_wi, ffn1_wo, ffn2_wi, ffn2_wo, w_in, w_out,
           mlstm_gate_b, mlstm_out_norm, mla_cq_norm, mla_ckv_norm, mla_w_uq, mla_w_ukv, mla_q_norm, mla_k_norm,
           gqa_q_norm, gqa_k_norm, gqa_sink):
    map_b, map_c = _head_lane_map_b(), _head_lane_map_c()
    nope_map = np.where(map_b < B_NOPE, map_b, -1)
    gate_order = np.concatenate([np.arange(N_GATES), np.arange(N_GATES).reshape(4, A_HEADS)[[1, 0, 3, 2]].ravel()])

    def arrange(w_in_l, gate_b_l, w_uq_l, w_ukv_l, bq_l, bk_l, cq_l, ck_l):
        ukv = w_ukv_l.reshape(B_KV_RANK, B_HEADS, B_NOPE + B_DV)
        return dict(
            win=_arrange_w_in(w_in_l, map_b, map_c).astype(bf16),
            bias=_pad_cols(jnp.pad(gate_b_l[None], ((0, 0), (COL_AG, 0))), WP),
            wkg=jnp.concatenate([w_in_l[:, 128:256], w_in_l[:, 768 + gate_order]], axis=1).T.astype(bf16),
            gb=gate_b_l[gate_order][:, None],
            wuq=_place_heads(w_uq_l, B_HEADS, map_b).astype(bf16),
            wukv=jnp.concatenate([_place_heads(ukv[:, :, :B_NOPE].reshape(B_KV_RANK, -1), B_HEADS, nope_map),
                                  ukv[:, :, B_NOPE:].reshape(B_KV_RANK, -1)], axis=1).astype(bf16),
            hg=jnp.concatenate([_place(bq_l[None], map_b), _place(bk_l[None], map_b),
                                _place(cq_l[None], map_c), _place(ck_l[None], map_c)]))

    pw = jax.vmap(arrange)(w_in, mlstm_gate_b, mla_w_uq, mla_w_ukv, mla_q_norm, mla_k_norm, gqa_q_norm, gqa_k_norm)
    wi1, wo1, wi2, wo2 = (w.astype(bf16) for w in (ffn1_wi, ffn1_wo, ffn2_wi, ffn2_wo))
    wout = w_out.astype(bf16)
    cqn, ckvn, onorm = mla_cq_norm[:, None, :], mla_ckv_norm[:, None, :], mlstm_out_norm[:, None, :]

    cc = jnp.concatenate([c, c_ctx[None, :], jnp.zeros((MOD_ROWS - B - 1, D), f32)], axis=0)
    mod = _ada(cc, ada_w, ada_b).reshape(DEPTH, MOD_ROWS, N_MOD, D)
    tab = _rope_tables(map_b, map_c)
    xs = (x.reshape(NL, D), ctx.reshape(NCX, D))

    for l in range(DEPTH):
        need_ctx = l < DEPTH - 1
        x1, h = _ffn1(l, xs, mod, norm_g, wi1, wo1)
        pa, kt, gt, qb, kb, vb, qc, kc, vc = _inproj(
            l, h, tab, pw["win"], pw["bias"], cqn, ckvn, pw["wuq"], pw["wukv"], pw["hg"], pw["wkg"], pw["gb"])
        hf, hb = _mlstm(pa, kt, gt)
        y_lat = (_mla(qb, kb, vb, True), _gqa(gqa_sink[l], qc, kc, vc, True))
        y_ctx = (_mla(qb, kb, vb, False), _gqa(gqa_sink[l], qc, kc, vc, False)) if need_ctx else None
        xs = _outproj_ffn(l, x1, mod, norm_g, hf, hb, pa, y_lat, y_ctx, onorm, wout, wi2, wo2)
    return xs.reshape(B, T, D)
```

```python
import functools
import math

import jax
import jax.numpy as jnp
import numpy as np
from jax import lax
from jax.experimental import pallas as pl
from jax.experimental.pallas import tpu as pltpu

f32 = jnp.float32
bf16 = jnp.bfloat16

D = 1024
B = 4
T = 4096
CTX = 256
DEPTH = 2
GRID_W = 64
ROPE_BASE = 10000.0
EPS = 1e-6
HALF = 0.5
N_MOD = 9
D_FF = 2816
A_HEADS, A_DK, A_DV, A_CHUNK = 4, 32, 64, 64
B_HEADS, B_Q_RANK, B_KV_RANK, B_NOPE, B_ROPE, B_DV = 6, 256, 128, 64, 32, 64
B_DQK = B_NOPE + B_ROPE
C_HEADS, C_KV_HEADS, C_DH, WINDOW = 6, 2, 64, 128
C_GROUP = C_HEADS // C_KV_HEADS

NL = B * T
NCX = B * CTX
N = NL + NCX

LANES = 128
HALF_LANES = LANES // 2
MOD_ROWS = 8
VMEM_LIMIT = 56 * 1024 * 1024
LOG2E = math.log2(math.e)

TM_FFN = 512
FF_CHUNK = 256
TM_IN = 512
ADA_TN = 1152
MLSTM_R = 256
MLA_TQ = 512
MLA_KC = 512
GQA_TQ = 256
GQA_BAND = GQA_TQ + 2 * WINDOW

COL_AQ, COL_AV, COL_AO, COL_AG = 0, 128, 384, 640
COL_BCQ, COL_BCKV, COL_BKR = 768, 1024, 1152
COL_CQ, COL_CK, COL_CV = 1280, 1664, 1792
WP = 1920
C_HEAD_ORDER = tuple(h for g in range(C_GROUP) for h in (g, C_GROUP + g))
PA_V, PA_O, PA_Q, PA_G = 0, 256, 512, 640
PA_W = PA_G + 4 * LANES
N_GATES = 4 * A_HEADS


def _head_lane_map_b():
    m = -np.ones(LANES, np.int64)
    m[0:8], m[8:16], m[16:64] = np.arange(64, 72), np.arange(80, 88), np.arange(0, 48)
    m[64:72], m[72:80], m[80:96] = np.arange(72, 80), np.arange(88, 96), np.arange(48, 64)
    return m


def _head_lane_map_c():
    return np.concatenate([np.arange(C_DH), np.arange(C_DH)])


def _sigmoid(x):
    return 1.0 / (1.0 + jnp.exp(-x))


def _log_sigmoid(x):
    return jnp.minimum(x, 0.0) - jnp.log(1.0 + jnp.exp(-jnp.abs(x)))


def _rms(x, g):
    ms = jnp.mean(x * x, axis=-1, keepdims=True)
    return x * lax.rsqrt(ms + EPS) * g


def _dot(a, b):
    return jnp.dot(a, b, preferred_element_type=f32)


def _dot_nt(a, b):
    return lax.dot_general(a, b, (((1,), (1,)), ((), ())), preferred_element_type=f32)


def _split3(x):
    hi = x.astype(bf16)
    r1 = x - hi.astype(f32)
    mid = r1.astype(bf16)
    return hi, mid, (r1 - mid.astype(f32)).astype(bf16)


def _layer(arr, l):
    rest = (0,) * (arr.ndim - 1)
    return pl.BlockSpec((None,) + arr.shape[1:], lambda *_: (l,) + rest, pipeline_mode=pl.Buffered(1))


def _mod_spec(l, tm):
    tpb = T // tm
    return pl.BlockSpec((None, 1, N_MOD, D), lambda i: (l, i // tpb, 0, 0))


def _params(n_axes):
    return pltpu.CompilerParams(dimension_semantics=("arbitrary",) * n_axes, vmem_limit_bytes=VMEM_LIMIT)


def _ada_kernel(c_ref, w_ref, b_ref, o_ref):
    c = c_ref[...]
    s = c * _sigmoid(c)
    o_ref[0] = jnp.dot(s, w_ref[0], preferred_element_type=f32, precision=lax.Precision.HIGHEST) + b_ref[0]


def _ada(cc, ada_w, ada_b):
    nt = (N_MOD * D) // ADA_TN
    return pl.pallas_call(
        _ada_kernel,
        grid=(DEPTH, nt),
        in_specs=[
            pl.BlockSpec((MOD_ROWS, D), lambda l, j: (0, 0)),
            pl.BlockSpec((1, D, ADA_TN), lambda l, j: (l, 0, j)),
            pl.BlockSpec((1, 1, ADA_TN), lambda l, j: (l, 0, j)),
        ],
        out_specs=pl.BlockSpec((1, MOD_ROWS, ADA_TN), lambda l, j: (l, 0, j)),
        out_shape=jax.ShapeDtypeStruct((DEPTH, MOD_ROWS, N_MOD * D), f32),
        compiler_params=_params(2),
        name="ada_mod",
    )(cc, ada_w, ada_b.reshape(DEPTH, 1, N_MOD * D))


def _ffn(x, g, shift, scale, gate, wi_ref, wo_ref):
    h = (_rms(x, g) * (1.0 + scale) + shift).astype(bf16)
    acc = None
    for c in range(D_FF // FF_CHUNK):
        lo, hi = c * FF_CHUNK, (c + 1) * FF_CHUNK
        gt = _dot(h, wi_ref[:, lo:hi])
        up = _dot(h, wi_ref[:, D_FF + lo:D_FF + hi])
        a = (gt * _sigmoid(gt) * up).astype(bf16)
        part = _dot(a, wo_ref[lo:hi, :])
        acc = part if acc is None else acc + part
    return x + HALF * gate * acc


def _ffn1_kernel(*refs, split_input):
    if split_input:
        xl_ref, xc_ref, mod_ref, ng_ref, wi_ref, wo_ref, x1_ref, h_ref = refs
        x = jnp.where(pl.program_id(0) < NL // TM_FFN, xl_ref[...], xc_ref[...])
    else:
        x_ref, mod_ref, ng_ref, wi_ref, wo_ref, x1_ref, h_ref = refs
        x = x_ref[...]
    mod = mod_ref[0]
    x1 = _ffn(x, ng_ref[0:1, :], mod[0:1, :], mod[1:2, :], mod[2:3, :], wi_ref, wo_ref)
    x1_ref[...] = x1
    h_ref[...] = (_rms(x1, ng_ref[1:2, :]) * (1.0 + mod[4:5, :]) + mod[3:4, :]).astype(bf16)


def _ffn1(l, xs, mod, ng, wi, wo):
    split_input = isinstance(xs, tuple)
    nlt = NL // TM_FFN
    if split_input:
        assert NCX % TM_FFN == 0
        x_specs = [pl.BlockSpec((TM_FFN, D), lambda i: (jnp.minimum(i, nlt - 1), 0)),
                   pl.BlockSpec((TM_FFN, D), lambda i: (jnp.maximum(i - nlt, 0), 0))]
    else:
        xs = (xs,)
        x_specs = [pl.BlockSpec((TM_FFN, D), lambda i: (i, 0))]
    return pl.pallas_call(
        functools.partial(_ffn1_kernel, split_input=split_input),
        grid=(N // TM_FFN,),
        in_specs=x_specs + [_mod_spec(l, TM_FFN), _layer(ng, l), _layer(wi, l), _layer(wo, l)],
        out_specs=[pl.BlockSpec((TM_FFN, D), lambda i: (i, 0))] * 2,
        out_shape=[jax.ShapeDtypeStruct((N, D), f32), jax.ShapeDtypeStruct((N, D), bf16)],
        compiler_params=_params(1),
        name="ffn1",
    )(*xs, mod, ng, wi, wo)


def _outproj_ffn_kernel(*refs, with_ctx):
    if with_ctx:
        (x_ref, mod_ref, ng_ref, hf_ref, hb_ref, o_ref, ybl_ref, ycl_ref, ybc_ref, ycc_ref, on_ref, wout_ref,
         wi_ref, wo_ref, out_ref) = refs
        is_latent = pl.program_id(0) < NL // TM_FFN
        yb = jnp.where(is_latent, ybl_ref[...], ybc_ref[...])
        yc = jnp.where(is_latent, ycl_ref[...], ycc_ref[...])
    else:
        (x_ref, mod_ref, ng_ref, hf_ref, hb_ref, o_ref, yb_ref, yc_ref, on_ref, wout_ref,
         wi_ref, wo_ref, out_ref) = refs
        yb, yc = yb_ref[...], yc_ref[...]
    x = x_ref[...]
    mod = mod_ref[0]
    hs = hf_ref[...] + hb_ref[...]
    sq = hs * hs
    head = lax.broadcasted_iota(jnp.int32, (1, A_HEADS * A_DV), 1) // A_DV
    ms = jnp.zeros_like(hs)
    for hh in range(A_HEADS):
        sel = head == hh
        ssh = jnp.sum(jnp.where(sel, sq, 0.0), axis=-1, keepdims=True) * (1.0 / A_DV)
        ms = jnp.where(sel, ssh, ms)
    ya = _sigmoid(o_ref[...]) * (hs * lax.rsqrt(ms + EPS) * on_ref[...])
    y = jnp.concatenate([ya.astype(bf16), yb, yc], axis=-1)
    x2 = x + mod[5:6, :] * _dot(y, wout_ref[...])
    out_ref[...] = _ffn(x2, ng_ref[2:3, :], mod[6:7, :], mod[7:8, :], mod[8:9, :], wi_ref, wo_ref)


def _outproj_ffn(l, x1, mod, ng, hf, hb, pa, y_lat, y_ctx, onorm, wout, wi, wo):
    with_ctx = y_ctx is not None
    rows = N if with_ctx else NL
    nlt = NL // TM_FFN
    row = lambda w, c=0: pl.BlockSpec((TM_FFN, w), lambda i: (i, c))
    lat = lambda w: pl.BlockSpec((TM_FFN, w), lambda i: (jnp.minimum(i, nlt - 1), 0))
    ctx = lambda w: pl.BlockSpec((TM_FFN, w), lambda i: (jnp.maximum(i - nlt, 0), 0))
    y_specs = [lat(B_HEADS * B_DV), lat(C_HEADS * C_DH)]
    if with_ctx:
        y_specs += [ctx(B_HEADS * B_DV), ctx(C_HEADS * C_DH)]
    return pl.pallas_call(
        functools.partial(_outproj_ffn_kernel, with_ctx=with_ctx),
        grid=(rows // TM_FFN,),
        in_specs=[
            row(D), _mod_spec(l, TM_FFN), _layer(ng, l),
            row(A_HEADS * A_DV), row(A_HEADS * A_DV), row(A_HEADS * A_DV, PA_O // (A_HEADS * A_DV)),
        ] + y_specs + [_layer(onorm, l), _layer(wout, l), _layer(wi, l), _layer(wo, l)],
        out_specs=row(D),
        out_shape=jax.ShapeDtypeStruct((rows, D), f32),
        compiler_params=_params(1),
        name="outproj_ffn2",
    )(x1, mod, ng, hf, hb, pa, *y_lat, *(y_ctx or ()), onorm, wout, wi, wo)


def _inproj_kernel(h_ref, win_ref, bias_ref, tab_ref, cqn_ref, ckvn_ref, wuq_ref, wukv_ref, hg_ref,
                   wkg_ref, gb_ref,
                   pa_ref, kt_ref, gt_ref, qb_ref, kb_ref, vb_ref, qc_ref, kc_ref, vc_ref):
    h = h_ref[...]
    p = _dot(h, win_ref[...]) + bias_ref[...]

    pa_ref[:, PA_V:PA_Q] = p[:, COL_AV:COL_AG]
    pa_ref[:, PA_Q:PA_G] = p[:, COL_AQ:COL_AV] * (A_DK ** -0.5)
    graw = p[:, COL_AG:COL_AG + LANES]
    lane = lax.broadcasted_iota(jnp.int32, (1, LANES), 1)
    for kk in range(4):
        gk = graw if kk == 0 else pltpu.roll(graw, LANES - A_HEADS * kk, 1)
        if kk % 2 == 1:
            gk = _log_sigmoid(gk)
        pa_ref[:, PA_G + LANES * kk:PA_G + LANES * (kk + 1)] = jnp.where(lane < A_HEADS, gk, 0.0)

    grow = lax.broadcasted_iota(jnp.int32, (2 * N_GATES, 1), 0)
    is_forget = ((grow // A_HEADS) % 2 == 1) == (grow < N_GATES)
    for c in range(TM_IN // A_CHUNK):
        t = _dot_nt(wkg_ref[...], h[c * A_CHUNK:(c + 1) * A_CHUNK, :])
        kt_ref[c] = t[0:LANES, :]
        g = t[LANES:LANES + 2 * N_GATES, :] + gb_ref[...]
        gt_ref[c] = jnp.where(is_forget, _log_sigmoid(g), g)

    tab = tab_ref[...]
    tb = [tab[:, LANES * i:LANES * (i + 1)] for i in range(4)]
    hg = hg_ref[...]

    cq = _rms(p[:, COL_BCQ:COL_BCKV], cqn_ref[...]).astype(bf16)
    ckv = _rms(p[:, COL_BCKV:COL_BKR], ckvn_ref[...]).astype(bf16)
    kr = p[:, COL_BKR:COL_CQ]
    q = _dot(cq, wuq_ref[...])
    kv = _dot(ckv, wukv_ref[...])
    vb_ref[...] = kv[:, B_HEADS * LANES:].astype(bf16)
    vc_ref[...] = p[:, COL_CV:COL_CV + LANES].astype(bf16)

    two = lambda a: jnp.concatenate([a, a], axis=1)
    pair = lambda a, j, first=0: a[:, first + 2 * LANES * j:first + 2 * LANES * (j + 1)]
    cos_b, sin_b, cos_c, sin_c = (two(t) for t in tb)
    kr2 = two(kr)
    ri = lax.broadcasted_iota(jnp.int32, (2 * LANES, 2 * LANES), 0)
    ci = lax.broadcasted_iota(jnp.int32, (2 * LANES, 2 * LANES), 1)
    swap_b = ((ri // LANES == ci // LANES) & (ri % LANES == (ci + HALF_LANES) % LANES)).astype(bf16)
    half_c = C_DH // 2
    swap_c = ((ri // half_c == ci // half_c) & (ri % half_c == (ci + half_c // 2) % half_c)).astype(bf16)
    full = slice(0, 2 * LANES)
    jobs = []
    for j in range(B_HEADS // 2):
        dst = slice(2 * LANES * j, 2 * LANES * (j + 1))
        jobs.append((pair(q, j), hg[0:1, :], LANES, B_DQK, cos_b, sin_b, swap_b, [(qb_ref, dst, full)]))
        jobs.append((pair(kv, j) + kr2, hg[1:2, :], LANES, B_DQK, cos_b, sin_b, swap_b, [(kb_ref, dst, full)]))
    jobs.append((pair(p, 0, COL_CQ), hg[2:3, :], C_DH, C_DH, cos_c, sin_c, swap_c, [(qc_ref, full, full)]))
    jobs.append((pair(p, 1, COL_CQ), hg[3:4, :], C_DH, C_DH, cos_c, sin_c, swap_c,
                 [(qc_ref, slice(2 * LANES, 3 * LANES), slice(0, LANES)),
                  (kc_ref, slice(0, LANES), slice(LANES, 2 * LANES))]))
    lane = lax.broadcasted_iota(jnp.int32, (1, LANES), 1)
    lane2 = lax.broadcasted_iota(jnp.int32, (1, 2 * LANES), 1)
    sums = []
    for x, _, width, *_ in jobs:
        sq = x * x
        parts = []
        for t in range(2):
            blk = sq[:, LANES * t:LANES * (t + 1)]
            if width == LANES:
                parts.append(jnp.sum(blk, axis=-1, keepdims=True))
            else:
                parts.append(jnp.sum(jnp.where(lane < width, blk, 0.0), axis=-1, keepdims=True))
                parts.append(jnp.sum(jnp.where(lane < width, 0.0, blk), axis=-1, keepdims=True))
        ss = parts[-1]
        for k in range(len(parts) - 2, -1, -1):
            ss = jnp.where(lane2 < (k + 1) * width, parts[k], ss)
        sums.append(ss)
    normed = [x * lax.rsqrt(ss * (1.0 / n_real) + EPS) * gain
              for (x, gain, _, n_real, *_), ss in zip(jobs, sums)]
    rolled = [_dot(y.astype(bf16), job[6]) for job, y in zip(jobs, normed)]
    for (_, _, _, _, cos, sin, _, dests), y, yr in zip(jobs, normed, rolled):
        out = (y * cos + yr * sin).astype(bf16)
        for dst_ref, dst_cols, src_cols in dests:
            dst_ref[:, dst_cols] = out[:, src_cols]


def _inproj(l, h, tab, win, bias, cqn, ckvn, wuq, wukv, hg, wkg, gb):
    tpb = T // TM_IN
    row = lambda w: pl.BlockSpec((TM_IN, w), lambda i: (i, 0))
    chunked = lambda r: pl.BlockSpec((TM_IN // A_CHUNK, r, A_CHUNK), lambda i: (i, 0, 0))
    out_w = [(B_HEADS * LANES, bf16), (B_HEADS * LANES, bf16), (B_HEADS * B_DV, bf16),
             (C_HEADS * C_DH, bf16), (C_KV_HEADS * C_DH, bf16), (C_KV_HEADS * C_DH, bf16)]
    return pl.pallas_call(
        _inproj_kernel,
        grid=(N // TM_IN,),
        in_specs=[
            row(D), _layer(win, l), _layer(bias, l),
            pl.BlockSpec((TM_IN, 4 * LANES), lambda i: (jnp.where(i < NL // TM_IN, i % tpb, tpb), 0)),
            _layer(cqn, l), _layer(ckvn, l), _layer(wuq, l), _layer(wukv, l),
            _layer(hg, l), _layer(wkg, l), _layer(gb, l),
        ],
        out_specs=[row(PA_W), chunked(LANES), chunked(2 * N_GATES)] + [row(w) for w, _ in out_w],
        out_shape=[jax.ShapeDtypeStruct((N, PA_W), f32),
                   jax.ShapeDtypeStruct((N // A_CHUNK, LANES, A_CHUNK), f32),
                   jax.ShapeDtypeStruct((N // A_CHUNK, 2 * N_GATES, A_CHUNK), f32)]
        + [jax.ShapeDtypeStruct((N, w), dt) for w, dt in out_w],
        compiler_params=_params(1),
        name="inproj",
    )(h, win, bias, tab, cqn, ckvn, wuq, wukv, hg, wkg, gb)


def _cummax_rows(x, rev):
    n = x.shape[0]
    row = lax.broadcasted_iota(jnp.int32, (n, 1), 0)
    sh = 1
    while sh < n:
        if rev:
            x = jnp.maximum(x, jnp.where(row < n - sh, pltpu.roll(x, n - sh, 0), -jnp.inf))
        else:
            x = jnp.maximum(x, jnp.where(row >= sh, pltpu.roll(x, sh, 0), -jnp.inf))
        sh *= 2
    return x


def _stack_heads(pieces):
    return jnp.concatenate(pieces, axis=0)


def _mlstm_kernel(*refs):
    streams = [(refs[0:7] + refs[14:15], False), (refs[7:14] + refs[15:16], True)]
    s_ref, ml_ref, ms_ref = refs[16:19]

    @pl.when(pl.program_id(1) == 0)
    def _():
        s_ref[...] = jnp.zeros_like(s_ref)
        ml_ref[...] = jnp.zeros_like(ml_ref)
        ms_ref[...] = jnp.zeros_like(ms_ref)

    L = A_CHUNK
    n_chunks = MLSTM_R // L
    heads = range(A_HEADS)
    ti = lax.broadcasted_iota(jnp.int32, (L, L), 0)
    si = lax.broadcasted_iota(jnp.int32, (L, L), 1)
    lane = lax.broadcasted_iota(jnp.int32, (1, LANES), 1)
    row8 = lax.broadcasted_iota(jnp.int32, (2 * A_HEADS, 1), 0)
    in_head = [(lane >= hh * A_DK) & (lane < (hh + 1) * A_DK) for hh in heads]
    ones_blk = jnp.ones((L, LANES), bf16)

    items = []
    for sidx, (srefs, rev) in enumerate(streams):
        q_ref, kt_ref, v_ref, ig_ref, lf_ref, gt_ref, gts_ref, h_ref = srefs
        attend = (si >= ti) if rev else (si <= ti)
        attend4 = _stack_heads([attend] * A_HEADS)
        cum_cols = attend.astype(bf16)
        cum_rows = ((ti >= si) if rev else (ti <= si)).astype(bf16)
        m_lane = ml_ref[sidx, 0:1, :]
        m_sub = ms_ref[sidx, :, 0:1]
        for cc in (range(n_chunks - 1, -1, -1) if rev else range(n_chunks)):
            rows = slice(cc * L, (cc + 1) * L)
            ig = ig_ref[rows, :]
            lf = lf_ref[rows, :]
            gt = gt_ref[cc]
            gts = gts_ref[cc]
            b_col = sum(_dot(cum_cols, piece) for piece in _split3(lf))
            r_col = ig - b_col
            big_m = jnp.maximum(m_lane, _cummax_rows(r_col, rev))
            mt_col = b_col + big_m
            b_last_l = jnp.sum(lf, axis=0, keepdims=True)
            m_new_l = jnp.maximum(m_lane, jnp.max(r_col, axis=0, keepdims=True)) + b_last_l
            b_rows = sum(_dot(piece, cum_rows) for piece in _split3(gts))
            live = row8 < A_HEADS
            r8 = jnp.where(live, gt - b_rows, 0.0)
            b_last_s = jnp.where(live, jnp.sum(gts, axis=-1, keepdims=True), 0.0)
            r_max_s = jnp.max(r8, axis=-1, keepdims=True)
            wg8 = jnp.exp(r8 - r_max_s)
            m_new_s = jnp.maximum(m_sub, r_max_s) + b_last_s
            decay_s = jnp.exp(b_last_s + m_sub - m_new_s)
            scale_s = jnp.exp(b_last_s + r_max_s - m_new_s)
            expand = lambda a, n: _stack_heads([jnp.broadcast_to(a[hh:hh + 1, :], (n, a.shape[1])) for hh in heads])
            q = q_ref[rows, :]
            big_m_b = _stack_heads([jnp.broadcast_to(big_m[:, hh:hh + 1], (L, LANES)) for hh in heads])
            mt_b = _stack_heads([jnp.broadcast_to(mt_col[:, hh:hh + 1], (L, LANES)) for hh in heads])
            m_old_b = _stack_heads([jnp.broadcast_to(m_lane[:, hh:hh + 1], (L, LANES)) for hh in heads])
            items.append(dict(
                sidx=sidx, rows=rows, h_ref=h_ref,
                qst=_stack_heads([jnp.where(in_head[hh], q, 0.0) for hh in heads]).astype(bf16),
                kt=kt_ref[cc].astype(bf16),
                kw=(kt_ref[cc] * expand(wg8, A_DK)).astype(bf16),
                vo=jnp.concatenate([v_ref[rows, :].astype(bf16), ones_blk], axis=1),
                w=jnp.exp(jnp.where(attend4, expand(r8, L) - big_m_b[:, 0:L], -jnp.inf)),
                a_inter=jnp.exp(m_old_b - big_m_b), floor=jnp.exp(-mt_b),
                decay=jnp.broadcast_to(expand(decay_s, A_DK), (LANES, LANES)),
                kv_scale=jnp.broadcast_to(expand(scale_s, A_DK), (LANES, LANES))))
            m_lane, m_sub = m_new_l, m_new_s
        ml_ref[sidx, 0:1, :] = m_lane
        ms_ref[sidx, :, 0:1] = m_sub

    for it in items:
        it["s"] = _dot(it["qst"], it["kt"])

    for it in items:
        it["kv"] = _dot(it["kw"], it["vo"])

    for it in items:
        it["p"] = (it["s"] * it["w"]).astype(bf16)

    state = [s_ref[0], s_ref[1]]
    tile3 = lambda a: jnp.concatenate([a] * 3, axis=1)
    for it in items:
        st = state[it["sidx"]]
        it["c_in"] = st.astype(bf16)
        state[it["sidx"]] = tile3(it["decay"]) * st + tile3(it["kv_scale"]) * it["kv"]
    s_ref[0] = state[0]
    s_ref[1] = state[1]

    nv = A_HEADS * A_DV
    for it in items:
        out = tile3(it["a_inter"]) * _dot(it["qst"], it["c_in"]) + _dot(it["p"], it["vo"])
        res = out[:, 0:nv] / tile3(jnp.maximum(jnp.abs(out[:, nv:]), it["floor"]))[:, 0:nv]
        for pair in range(A_HEADS // 2):
            sl = slice(LANES * pair, LANES * (pair + 1))
            even = res[L * 2 * pair:L * (2 * pair + 1), sl]
            odd = res[L * (2 * pair + 1):L * (2 * pair + 2), sl]
            it["h_ref"][it["rows"], sl] = jnp.where(lane < A_DV, even, odd)


def _mlstm(pa, kt, gt):
    nb = T // MLSTM_R
    nc = MLSTM_R // A_CHUNK
    assert CTX == MLSTM_R

    def rb(rev):
        def f(b, j):
            jj = j - 1
            return jnp.where(j == 0, NL // MLSTM_R + b, b * nb + (nb - 1 - jj if rev else jj))
        return f

    def stream_specs(rev):
        r = rb(rev)
        d = 2 if rev else 0
        col = lambda w, c: pl.BlockSpec((MLSTM_R, w), lambda b, j: (r(b, j), c))
        return [
            col(LANES, PA_Q // LANES),
            pl.BlockSpec((nc, LANES, A_CHUNK), lambda b, j: (r(b, j), 0, 0)),
            col(2 * LANES, PA_V // (2 * LANES)),
            col(LANES, PA_G // LANES + d), col(LANES, PA_G // LANES + d + 1),
            pl.BlockSpec((nc, 2 * A_HEADS, A_CHUNK), lambda b, j: (r(b, j), d // 2, 0)),
            pl.BlockSpec((nc, 2 * A_HEADS, A_CHUNK), lambda b, j: (r(b, j), 2 + d // 2, 0)),
        ]

    out_spec = lambda rev: pl.BlockSpec((MLSTM_R, A_HEADS * A_DV), lambda b, j: (rb(rev)(b, j), 0))
    return pl.pallas_call(
        _mlstm_kernel,
        grid=(B, nb + 1),
        in_specs=stream_specs(False) + stream_specs(True),
        out_specs=[out_spec(False), out_spec(True)],
        out_shape=[jax.ShapeDtypeStruct((N, A_HEADS * A_DV), f32)] * 2,
        scratch_shapes=[pltpu.VMEM((2, LANES, 3 * LANES), f32), pltpu.VMEM((2, 8, LANES), f32),
                        pltpu.VMEM((2, 8, LANES), f32)],
        compiler_params=_params(2),
        name="mlstm",
    )(*[pa, kt, pa, pa, pa, gt, gt] * 2)


def _tile_max(s, m128):
    for t in range(s.shape[1] // LANES):
        blk = s[:, LANES * t:LANES * (t + 1)]
        m128 = blk if m128 is None else jnp.maximum(m128, blk)
    return m128


def _mla_kernel(*refs, latent):
    if latent:
        q_ref, kc_ref, vc_ref, kl_ref, vl_ref, o_ref, s_ref = refs
        sources = [(kc_ref, vc_ref, 0, CTX)] + [(kl_ref, vl_ref, c, MLA_KC) for c in range(0, T, MLA_KC)]
    else:
        q_ref, kc_ref, vc_ref, o_ref, s_ref = refs
        sources = [(kc_ref, vc_ref, 0, CTX)]
    q = q_ref[...]
    lane = lax.broadcasted_iota(jnp.int32, (1, LANES), 1)
    row_max = []
    for hh in range(2):
        sl = slice(LANES * hh, LANES * (hh + 1))
        qh = q[:, sl]
        m128 = None
        off = 0
        for k_ref, _, r0, n in sources:
            s = _dot_nt(qh, k_ref[r0:r0 + n, sl])
            s_ref[hh, :, off:off + n] = s
            m128 = _tile_max(s, m128)
            off += n
        row_max.append(jnp.max(m128, axis=-1, keepdims=True))
    outs = []
    for hh in range(2):
        den_lane = B_DV if hh == 0 else 0
        acc = None
        off = 0
        for _, v_ref, r0, n in sources:
            p = jnp.exp2(s_ref[hh, :, off:off + n] - row_max[hh]).astype(bf16)
            vext = jnp.where(lane == den_lane, 1.0, v_ref[r0:r0 + n, :]).astype(bf16)
            part = _dot(p, vext)
            acc = part if acc is None else acc + part
            off += n
        outs.append(acc / acc[:, den_lane:den_lane + 1])
    o_ref[...] = jnp.where(lane < B_DV, outs[0], outs[1]).astype(o_ref.dtype)


def _mla(qb, kb, vb, latent):
    npair = B_HEADS // 2
    ctx_blk = NL // CTX
    kv_specs = [
        pl.BlockSpec((CTX, 2 * LANES), lambda b, p, i: (ctx_blk + b, p)),
        pl.BlockSpec((CTX, LANES), lambda b, p, i: (ctx_blk + b, p)),
    ]
    if latent:
        tq = MLA_TQ
        nq = T // tq
        qmap = omap = lambda b, p, i: (b * nq + i, p)
        kv_specs += [
            pl.BlockSpec((T, 2 * LANES), lambda b, p, i: (b, p)),
            pl.BlockSpec((T, LANES), lambda b, p, i: (b, p)),
        ]
        args = (qb, kb, vb, kb, vb)
        nkeys = CTX + T
    else:
        tq = CTX
        nq = 1
        qmap = lambda b, p, i: (ctx_blk + b, p)
        omap = lambda b, p, i: (b, p)
        args = (qb, kb, vb)
        nkeys = CTX
    return pl.pallas_call(
        functools.partial(_mla_kernel, latent=latent),
        grid=(B, npair, nq),
        in_specs=[pl.BlockSpec((tq, 2 * LANES), qmap)] + kv_specs,
        out_specs=pl.BlockSpec((tq, LANES), omap),
        out_shape=jax.ShapeDtypeStruct((NL if latent else NCX, B_HEADS * B_DV), bf16),
        scratch_shapes=[pltpu.VMEM((2, tq, nkeys), f32)],
        compiler_params=_params(3),
        name="mla_latent" if latent else "mla_context",
    )(*args)


def _gqa_kernel(sink_ref, *refs, latent):
    if latent:
        q_ref, kc_ref, vc_ref, kl_ref, vl_ref, o_ref = refs
    else:
        q_ref, kc_ref, vc_ref, o_ref = refs
    q = q_ref[...]
    tq = q.shape[0]
    lane = lax.broadcasted_iota(jnp.int32, (1, LANES), 1)
    keys = kc_ref[...]
    vals = vc_ref[...]
    valid = None
    if latent:
        n = pl.program_id(1)
        start = pl.multiple_of(jnp.clip(n * GQA_TQ - WINDOW, 0, T - GQA_BAND), WINDOW)
        keys = jnp.concatenate([keys, kl_ref[pl.ds(start, GQA_BAND), :]], axis=0)
        vals = jnp.concatenate([vals, vl_ref[pl.ds(start, GQA_BAND), :]], axis=0)
        qpos1 = n * GQA_TQ + lax.broadcasted_iota(jnp.int32, (tq, 1), 0)
        qpos = jnp.concatenate([qpos1] * C_GROUP, axis=0)
        kidx = lax.broadcasted_iota(jnp.int32, (1, CTX + GQA_BAND), 1)
        valid = (kidx < CTX) | (jnp.abs(qpos - (start - CTX + kidx)) <= WINDOW)
    lo = lane < C_DH
    outs = []
    for kvh in range(C_KV_HEADS):
        mine = lo if kvh == 0 else ~lo
        qs = jnp.concatenate([jnp.where(mine, q[:, LANES * g:LANES * (g + 1)], 0) for g in range(C_GROUP)], axis=0)
        sink = jnp.concatenate([jnp.full((tq, 1), sink_ref[C_GROUP * kvh + g] * LOG2E, f32)
                                for g in range(C_GROUP)], axis=0)
        s = _dot_nt(qs, keys)
        if latent:
            s = jnp.where(valid, s, -jnp.inf)
        m = jnp.maximum(sink, jnp.max(_tile_max(s, None), axis=-1, keepdims=True))
        den_lane = C_DH * (1 - kvh)
        acc = _dot(jnp.exp2(s - m).astype(bf16), jnp.where(lane == den_lane, 1.0, vals).astype(bf16))
        outs.append(acc / (jnp.exp2(sink - m) + acc[:, den_lane:den_lane + 1]))
    for g in range(C_GROUP):
        rows = slice(g * tq, (g + 1) * tq)
        o_ref[:, LANES * g:LANES * (g + 1)] = jnp.where(lo, outs[0][rows, :], outs[1][rows, :]).astype(o_ref.dtype)


def _gqa(sink, qc, kc, vc, latent):
    ctx_blk = NL // CTX
    kv_specs = [
        pl.BlockSpec((CTX, LANES), lambda b, i: (ctx_blk + b, 0)),
        pl.BlockSpec((CTX, LANES), lambda b, i: (ctx_blk + b, 0)),
    ]
    if latent:
        tq = GQA_TQ
        nq = T // tq
        qmap = omap = lambda b, i: (b * nq + i, 0)
        kv_specs += [
            pl.BlockSpec((T, LANES), lambda b, i: (b, 0)),
            pl.BlockSpec((T, LANES), lambda b, i: (b, 0)),
        ]
        args = (sink, qc, kc, vc, kc, vc)
    else:
        tq = CTX
        nq = 1
        qmap = lambda b, i: (ctx_blk + b, 0)
        omap = lambda b, i: (b, 0)
        args = (sink, qc, kc, vc)
    return pl.pallas_call(
        functools.partial(_gqa_kernel, latent=latent),
        grid=(B, nq),
        in_specs=[pl.BlockSpec(memory_space=pltpu.SMEM), pl.BlockSpec((tq, C_HEADS * C_DH), qmap)] + kv_specs,
        out_specs=pl.BlockSpec((tq, C_HEADS * C_DH), omap),
        out_shape=jax.ShapeDtypeStruct((NL if latent else NCX, C_HEADS * C_DH), bf16),
        compiler_params=_params(2),
        name="gqa_latent" if latent else "gqa_context",
    )(*args)


def _pad_cols(w, width):
    return jnp.pad(w, ((0, 0), (0, width - w.shape[1])))


def _lane_runs(lane_map):
    runs = []
    for src in lane_map:
        src = int(src)
        if runs and ((src < 0 and runs[-1][0] < 0) or (src >= 0 and runs[-1][0] >= 0 and src == sum(runs[-1]))):
            runs[-1] = (runs[-1][0], runs[-1][1] + 1)
        else:
            runs.append((src, 1))
    return runs


def _place(w, lane_map):
    parts = [jnp.zeros(w.shape[:-1] + (n,), w.dtype) if s < 0 else w[..., s:s + n] for s, n in _lane_runs(lane_map)]
    return jnp.concatenate(parts, axis=-1)


def _place_heads(w, heads, lane_map):
    r = w.shape[0]
    return _place(w.reshape(r, heads, -1), lane_map).reshape(r, heads * LANES)


def _reorder_c_heads(w, axis):
    heads = jnp.split(w, C_HEADS, axis=axis)
    return jnp.concatenate([heads[h] for h in C_HEAD_ORDER], axis=axis)


def _arrange_w_in(w, map_b):
    o = np.cumsum((0, 128, 128, 256, 256, 16, 256, 128, 32, 384, 128, 128))
    part = lambda i: w[:, int(o[i]):int(o[i + 1])]
    kr = _place(part(7), np.where(map_b >= B_NOPE, map_b - B_NOPE, -1))
    return jnp.concatenate([
        part(0), part(2), part(3), _pad_cols(part(4), LANES),
        part(5), part(6), kr,
        _reorder_c_heads(part(8), 1), part(9), part(10),
    ], axis=1)


def _rope_tables(map_b, map_c):
    t = jnp.arange(T + TM_IN, dtype=jnp.int32)[:, None]
    live = t < T
    row = (t // GRID_W).astype(f32)
    col = (t % GRID_W).astype(f32)

    def tables(lane_map, rope_start, half):
        rel = lane_map - rope_start
        in_rope = (lane_map >= 0) & (rel >= 0) & (rel < 4 * half)
        rel = np.where(in_rope, rel, 0)
        use_col = jnp.asarray(rel >= 2 * half)[None, :]
        second = jnp.asarray((rel // half) % 2 == 1)[None, :]
        freq = ROPE_BASE ** (-jnp.asarray(rel % half, f32) / half)
        rot = jnp.asarray(in_rope)[None, :] & live
        ang = jnp.where(use_col, col, row) * freq[None, :]
        sin = jnp.sin(ang)
        return [jnp.where(rot, jnp.cos(ang), 1.0), jnp.where(rot, jnp.where(second, sin, -sin), 0.0)]

    return jnp.concatenate(tables(map_b, B_NOPE, B_ROPE // 4) + tables(map_c, 0, C_DH // 4), axis=1)


def kernel(x, c, ctx, c_ctx, ada_w, ada_b, norm_g, ffn1_wi, ffn1_wo, ffn2_wi, ffn2_wo, w_in, w_out,
           mlstm_gate_b, mlstm_out_norm, mla_cq_norm, mla_ckv_norm, mla_w_uq, mla_w_ukv, mla_q_norm, mla_k_norm,
           gqa_q_norm, gqa_k_norm, gqa_sink):
    map_b, map_c = _head_lane_map_b(), _head_lane_map_c()
    nope_map = np.where(map_b < B_NOPE, map_b, -1)
    q_scale_b, q_scale_c = B_DQK ** -0.5 * LOG2E, C_DH ** -0.5 * LOG2E

    def both_gate_orders(g):
        grp = [g[..., A_HEADS * i:A_HEADS * (i + 1)] for i in range(4)]
        return jnp.concatenate([g, grp[1], grp[0], grp[3], grp[2]], axis=-1)

    def arrange(w_in_l, gate_b_l, w_uq_l, w_ukv_l, bq_l, bk_l, cq_l, ck_l):
        ukv = w_ukv_l.reshape(B_KV_RANK, B_HEADS, B_NOPE + B_DV)
        gbq, gbk = _place(bq_l[None], map_b) * q_scale_b, _place(bk_l[None], map_b)
        gcq, gck = _place(cq_l[None], map_c) * q_scale_c, _place(ck_l[None], map_c)
        return dict(
            win=_arrange_w_in(w_in_l, map_b).astype(bf16),
            bias=_pad_cols(jnp.pad(gate_b_l[None], ((0, 0), (COL_AG, 0))), WP),
            wkg=jnp.concatenate([w_in_l[:, 128:256], both_gate_orders(w_in_l[:, 768:768 + N_GATES])],
                                axis=1).T.astype(bf16),
            gb=both_gate_orders(gate_b_l)[:, None],
            wuq=_place_heads(w_uq_l, B_HEADS, map_b).astype(bf16),
            wukv=jnp.concatenate([_place_heads(ukv[:, :, :B_NOPE].reshape(B_KV_RANK, -1), B_HEADS, nope_map),
                                  ukv[:, :, B_NOPE:].reshape(B_KV_RANK, -1)], axis=1).astype(bf16),
            hg=jnp.concatenate([jnp.concatenate(pair, axis=1)
                                for pair in ((gbq, gbq), (gbk, gbk), (gcq, gcq), (gcq, gck))]))

    pw = jax.vmap(arrange)(w_in, mlstm_gate_b, mla_w_uq, mla_w_ukv, mla_q_norm, mla_k_norm, gqa_q_norm, gqa_k_norm)
    wi1, wo1, wi2, wo2 = (w.astype(bf16) for w in (ffn1_wi, ffn1_wo, ffn2_wi, ffn2_wo))
    c_rows = A_HEADS * A_DV + B_HEADS * B_DV
    wout = jnp.concatenate([w_out[:, :c_rows], _reorder_c_heads(w_out[:, c_rows:], 1)], axis=1).astype(bf16)
    cqn, ckvn, onorm = mla_cq_norm[:, None, :], mla_ckv_norm[:, None, :], mlstm_out_norm[:, None, :]

    cc = jnp.concatenate([c, c_ctx[None, :], jnp.zeros((MOD_ROWS - B - 1, D), f32)], axis=0)
    mod = _ada(cc, ada_w, ada_b).reshape(DEPTH, MOD_ROWS, N_MOD, D)
    tab = _rope_tables(map_b, map_c)
    xs = (x.reshape(NL, D), ctx.reshape(NCX, D))

    for l in range(DEPTH):
        need_ctx = l < DEPTH - 1
        x1, h = _ffn1(l, xs, mod, norm_g, wi1, wo1)
        pa, kt, gt, qb, kb, vb, qc, kc, vc = _inproj(
            l, h, tab, pw["win"], pw["bias"], cqn, ckvn, pw["wuq"], pw["wukv"], pw["hg"], pw["wkg"], pw["gb"])
        hf, hb = _mlstm(pa, kt, gt)
        y_lat = (_mla(qb, kb, vb, True), _gqa(gqa_sink[l], qc, kc, vc, True))
        y_ctx = (_mla(qb, kb, vb, False), _gqa(gqa_sink[l], qc, kc, vc, False)) if need_ctx else None
        xs = _outproj_ffn(l, x1, mod, norm_g, hf, hb, pa, y_lat, y_ctx, onorm, wout, wi2, wo2)
    return xs.reshape(B, T, D)
```

```python
import functools
import math

import jax
import jax.numpy as jnp
import numpy as np
from jax import lax
from jax.experimental import pallas as pl
from jax.experimental.pallas import tpu as pltpu

f32 = jnp.float32
bf16 = jnp.bfloat16

D = 1024
B = 4
T = 4096
CTX = 256
DEPTH = 2
GRID_W = 64
ROPE_BASE = 10000.0
EPS = 1e-6
HALF = 0.5
N_MOD = 9
D_FF = 2816
A_HEADS, A_DK, A_DV, A_CHUNK = 4, 32, 64, 64
B_HEADS, B_Q_RANK, B_KV_RANK, B_NOPE, B_ROPE, B_DV = 6, 256, 128, 64, 32, 64
B_DQK = B_NOPE + B_ROPE
C_HEADS, C_KV_HEADS, C_DH, WINDOW = 6, 2, 64, 128
C_GROUP = C_HEADS // C_KV_HEADS

NL = B * T
NCX = B * CTX
N = NL + NCX

LANES = 128
HALF_LANES = LANES // 2
MOD_ROWS = 8
VMEM_LIMIT = 56 * 1024 * 1024
LOG2E = math.log2(math.e)

TM_FFN = 512
FF_CHUNK = 256
TM_IN = 512
ADA_TN = 1152
MLSTM_R = 256
MLA_TQ = 512
MLA_KC = 512
GQA_TQ = 256
GQA_BAND = GQA_TQ + 2 * WINDOW

COL_AQ, COL_AV, COL_AO, COL_AG = 0, 128, 384, 640
COL_BCQ, COL_BCKV, COL_BKR = 768, 1024, 1152
COL_CQ, COL_CK, COL_CV = 1280, 1664, 1792
WP = 1920
C_HEAD_ORDER = tuple(h for g in range(C_GROUP) for h in (g, C_GROUP + g))
PA_V, PA_O, PA_Q, PA_G = 0, 256, 512, 640
PA_W = PA_G + 4 * LANES
N_GATES = 4 * A_HEADS


def _head_lane_map_b():
    m = -np.ones(LANES, np.int64)
    m[0:8], m[8:16], m[16:64] = np.arange(64, 72), np.arange(80, 88), np.arange(0, 48)
    m[64:72], m[72:80], m[80:96] = np.arange(72, 80), np.arange(88, 96), np.arange(48, 64)
    return m


def _head_lane_map_c():
    return np.concatenate([np.arange(C_DH), np.arange(C_DH)])


def _sigmoid(x):
    return 1.0 / (1.0 + jnp.exp(-x))


def _log_sigmoid(x):
    return jnp.minimum(x, 0.0) - jnp.log(1.0 + jnp.exp(-jnp.abs(x)))


def _rms(x, g):
    ms = jnp.mean(x * x, axis=-1, keepdims=True)
    return x * lax.rsqrt(ms + EPS) * g


def _dot(a, b):
    return jnp.dot(a, b, preferred_element_type=f32)


def _dot_nt(a, b):
    return lax.dot_general(a, b, (((1,), (1,)), ((), ())), preferred_element_type=f32)


def _split3(x):
    hi = x.astype(bf16)
    r1 = x - hi.astype(f32)
    mid = r1.astype(bf16)
    return hi, mid, (r1 - mid.astype(f32)).astype(bf16)


def _layer(arr, l):
    rest = (0,) * (arr.ndim - 1)
    return pl.BlockSpec((None,) + arr.shape[1:], lambda *_: (l,) + rest, pipeline_mode=pl.Buffered(1))


def _mod_spec(l, tm):
    tpb = T // tm
    return pl.BlockSpec((None, 1, N_MOD, D), lambda i: (l, i // tpb, 0, 0))


def _params(n_axes):
    return pltpu.CompilerParams(dimension_semantics=("arbitrary",) * n_axes, vmem_limit_bytes=VMEM_LIMIT)


def _ada_kernel(c_ref, w_ref, b_ref, o_ref):
    c = c_ref[...]
    s = c * _sigmoid(c)
    o_ref[0] = jnp.dot(s, w_ref[0], preferred_element_type=f32, precision=lax.Precision.HIGHEST) + b_ref[0]


def _ada(cc, ada_w, ada_b):
    nt = (N_MOD * D) // ADA_TN
    return pl.pallas_call(
        _ada_kernel,
        grid=(DEPTH, nt),
        in_specs=[
            pl.BlockSpec((MOD_ROWS, D), lambda l, j: (0, 0)),
            pl.BlockSpec((1, D, ADA_TN), lambda l, j: (l, 0, j)),
            pl.BlockSpec((1, 1, ADA_TN), lambda l, j: (l, 0, j)),
        ],
        out_specs=pl.BlockSpec((1, MOD_ROWS, ADA_TN), lambda l, j: (l, 0, j)),
        out_shape=jax.ShapeDtypeStruct((DEPTH, MOD_ROWS, N_MOD * D), f32),
        compiler_params=_params(2),
        name="ada_mod",
    )(cc, ada_w, ada_b.reshape(DEPTH, 1, N_MOD * D))


def _ffn(x, g, shift, scale, gate, wi_ref, wo_ref):
    h = (_rms(x, g) * (1.0 + scale) + shift).astype(bf16)
    acc = None
    for c in range(D_FF // FF_CHUNK):
        lo, hi = c * FF_CHUNK, (c + 1) * FF_CHUNK
        gt = _dot(h, wi_ref[:, lo:hi])
        up = _dot(h, wi_ref[:, D_FF + lo:D_FF + hi])
        a = (gt * _sigmoid(gt) * up).astype(bf16)
        part = _dot(a, wo_ref[lo:hi, :])
        acc = part if acc is None else acc + part
    return x + HALF * gate * acc


def _ffn1_kernel(*refs, split_input):
    if split_input:
        xl_ref, xc_ref, mod_ref, ng_ref, wi_ref, wo_ref, x1_ref, h_ref = refs
        x = jnp.where(pl.program_id(0) < NL // TM_FFN, xl_ref[...], xc_ref[...])
    else:
        x_ref, mod_ref, ng_ref, wi_ref, wo_ref, x1_ref, h_ref = refs
        x = x_ref[...]
    mod = mod_ref[0]
    x1 = _ffn(x, ng_ref[0:1, :], mod[0:1, :], mod[1:2, :], mod[2:3, :], wi_ref, wo_ref)
    x1_ref[...] = x1
    h_ref[...] = (_rms(x1, ng_ref[1:2, :]) * (1.0 + mod[4:5, :]) + mod[3:4, :]).astype(bf16)


def _ffn1(l, xs, mod, ng, wi, wo):
    split_input = isinstance(xs, tuple)
    nlt = NL // TM_FFN
    if split_input:
        assert NCX % TM_FFN == 0
        x_specs = [pl.BlockSpec((TM_FFN, D), lambda i: (jnp.minimum(i, nlt - 1), 0)),
                   pl.BlockSpec((TM_FFN, D), lambda i: (jnp.maximum(i - nlt, 0), 0))]
    else:
        xs = (xs,)
        x_specs = [pl.BlockSpec((TM_FFN, D), lambda i: (i, 0))]
    return pl.pallas_call(
        functools.partial(_ffn1_kernel, split_input=split_input),
        grid=(N // TM_FFN,),
        in_specs=x_specs + [_mod_spec(l, TM_FFN), _layer(ng, l), _layer(wi, l), _layer(wo, l)],
        out_specs=[pl.BlockSpec((TM_FFN, D), lambda i: (i, 0))] * 2,
        out_shape=[jax.ShapeDtypeStruct((N, D), f32), jax.ShapeDtypeStruct((N, D), bf16)],
        compiler_params=_params(1),
        name="ffn1",
    )(*xs, mod, ng, wi, wo)


def _outproj_ffn_kernel(*refs, with_ctx):
    if with_ctx:
        (x_ref, mod_ref, ng_ref, hf_ref, hb_ref, o_ref, ybl_ref, ycl_ref, ybc_ref, ycc_ref, on_ref, wout_ref,
         wi_ref, wo_ref, out_ref) = refs
        is_latent = pl.program_id(0) < NL // TM_FFN
        yb = jnp.where(is_latent, ybl_ref[...], ybc_ref[...])
        yc = jnp.where(is_latent, ycl_ref[...], ycc_ref[...])
    else:
        (x_ref, mod_ref, ng_ref, hf_ref, hb_ref, o_ref, yb_ref, yc_ref, on_ref, wout_ref,
         wi_ref, wo_ref, out_ref) = refs
        yb, yc = yb_ref[...], yc_ref[...]
    x = x_ref[...]
    mod = mod_ref[0]
    hs = hf_ref[...] + hb_ref[...]
    sq = hs * hs
    head = lax.broadcasted_iota(jnp.int32, (1, A_HEADS * A_DV), 1) // A_DV
    ms = jnp.zeros_like(hs)
    for hh in range(A_HEADS):
        sel = head == hh
        ssh = jnp.sum(jnp.where(sel, sq, 0.0), axis=-1, keepdims=True) * (1.0 / A_DV)
        ms = jnp.where(sel, ssh, ms)
    ya = _sigmoid(o_ref[...]) * (hs * lax.rsqrt(ms + EPS) * on_ref[...])
    y = jnp.concatenate([ya.astype(bf16), yb, yc], axis=-1)
    x2 = x + mod[5:6, :] * _dot(y, wout_ref[...])
    out_ref[...] = _ffn(x2, ng_ref[2:3, :], mod[6:7, :], mod[7:8, :], mod[8:9, :], wi_ref, wo_ref)


def _outproj_ffn(l, x1, mod, ng, hf, hb, pa, y_lat, y_ctx, onorm, wout, wi, wo):
    with_ctx = y_ctx is not None
    rows = N if with_ctx else NL
    nlt = NL // TM_FFN
    row = lambda w, c=0: pl.BlockSpec((TM_FFN, w), lambda i: (i, c))
    lat = lambda w: pl.BlockSpec((TM_FFN, w), lambda i: (jnp.minimum(i, nlt - 1), 0))
    ctx = lambda w: pl.BlockSpec((TM_FFN, w), lambda i: (jnp.maximum(i - nlt, 0), 0))
    y_specs = [lat(B_HEADS * B_DV), lat(C_HEADS * C_DH)]
    if with_ctx:
        y_specs += [ctx(B_HEADS * B_DV), ctx(C_HEADS * C_DH)]
    return pl.pallas_call(
        functools.partial(_outproj_ffn_kernel, with_ctx=with_ctx),
        grid=(rows // TM_FFN,),
        in_specs=[
            row(D), _mod_spec(l, TM_FFN), _layer(ng, l),
            row(A_HEADS * A_DV), row(A_HEADS * A_DV), row(A_HEADS * A_DV, PA_O // (A_HEADS * A_DV)),
        ] + y_specs + [_layer(onorm, l), _layer(wout, l), _layer(wi, l), _layer(wo, l)],
        out_specs=row(D),
        out_shape=jax.ShapeDtypeStruct((rows, D), f32),
        compiler_params=_params(1),
        name="outproj_ffn2",
    )(x1, mod, ng, hf, hb, pa, *y_lat, *(y_ctx or ()), onorm, wout, wi, wo)


def _inproj_kernel(h_ref, win_ref, bias_ref, tab_ref, cqn_ref, ckvn_ref, wuq_ref, wukv_ref, hg_ref,
                   wkg_ref, gb_ref,
                   pa_ref, kt_ref, gt_ref, qb_ref, kb_ref, vb_ref, qc_ref, kc_ref, vc_ref):
    h = h_ref[...]
    p = _dot(h, win_ref[...]) + bias_ref[...]

    pa_ref[:, PA_V:PA_Q] = p[:, COL_AV:COL_AG]
    pa_ref[:, PA_Q:PA_G] = p[:, COL_AQ:COL_AV] * (A_DK ** -0.5)
    graw = p[:, COL_AG:COL_AG + LANES]
    lane = lax.broadcasted_iota(jnp.int32, (1, LANES), 1)
    for kk in range(4):
        gk = graw if kk == 0 else pltpu.roll(graw, LANES - A_HEADS * kk, 1)
        if kk % 2 == 1:
            gk = _log_sigmoid(gk)
        pa_ref[:, PA_G + LANES * kk:PA_G + LANES * (kk + 1)] = jnp.where(lane < A_HEADS, gk, 0.0)

    grow = lax.broadcasted_iota(jnp.int32, (2 * N_GATES, 1), 0)
    is_forget = ((grow // A_HEADS) % 2 == 1) == (grow < N_GATES)
    for c in range(TM_IN // A_CHUNK):
        t = _dot_nt(wkg_ref[...], h[c * A_CHUNK:(c + 1) * A_CHUNK, :])
        kt_ref[c] = t[0:LANES, :]
        g = t[LANES:LANES + 2 * N_GATES, :] + gb_ref[...]
        gt_ref[c] = jnp.where(is_forget, _log_sigmoid(g), g)

    is_latent = pl.program_id(0) < NL // TM_IN
    tab = tab_ref[...]
    tb = [jnp.where(is_latent, tab[:, LANES * i:LANES * (i + 1)], 1.0 - (i % 2)) for i in range(4)]
    hg = hg_ref[...]

    cq = _rms(p[:, COL_BCQ:COL_BCKV], cqn_ref[...]).astype(bf16)
    ckv = _rms(p[:, COL_BCKV:COL_BKR], ckvn_ref[...]).astype(bf16)
    kr = p[:, COL_BKR:COL_CQ]
    q = _dot(cq, wuq_ref[...])
    kv = _dot(ckv, wukv_ref[...])
    vb_ref[...] = kv[:, B_HEADS * LANES:].astype(bf16)
    vc_ref[...] = p[:, COL_CV:COL_CV + LANES].astype(bf16)

    two = lambda a: jnp.concatenate([a, a], axis=1)
    pair = lambda a, j, first=0: a[:, first + 2 * LANES * j:first + 2 * LANES * (j + 1)]
    cos_b, sin_b, cos_c, sin_c = (two(t) for t in tb)
    kr2 = two(kr)
    ri = lax.broadcasted_iota(jnp.int32, (2 * LANES, 2 * LANES), 0)
    ci = lax.broadcasted_iota(jnp.int32, (2 * LANES, 2 * LANES), 1)
    swap_b = ((ri // LANES == ci // LANES) & (ri % LANES == (ci + HALF_LANES) % LANES)).astype(bf16)
    half_c = C_DH // 2
    swap_c = ((ri // half_c == ci // half_c) & (ri % half_c == (ci + half_c // 2) % half_c)).astype(bf16)
    full = slice(0, 2 * LANES)
    jobs = []
    for j in range(B_HEADS // 2):
        dst = slice(2 * LANES * j, 2 * LANES * (j + 1))
        jobs.append((pair(q, j), hg[0:1, :], LANES, B_DQK, cos_b, sin_b, swap_b, [(qb_ref, dst, full)]))
        jobs.append((pair(kv, j) + kr2, hg[1:2, :], LANES, B_DQK, cos_b, sin_b, swap_b, [(kb_ref, dst, full)]))
    jobs.append((pair(p, 0, COL_CQ), hg[2:3, :], C_DH, C_DH, cos_c, sin_c, swap_c, [(qc_ref, full, full)]))
    jobs.append((pair(p, 1, COL_CQ), hg[3:4, :], C_DH, C_DH, cos_c, sin_c, swap_c,
                 [(qc_ref, slice(2 * LANES, 3 * LANES), slice(0, LANES)),
                  (kc_ref, slice(0, LANES), slice(LANES, 2 * LANES))]))
    lane = lax.broadcasted_iota(jnp.int32, (1, LANES), 1)
    lane2 = lax.broadcasted_iota(jnp.int32, (1, 2 * LANES), 1)
    sums = []
    for x, _, width, *_ in jobs:
        sq = x * x
        parts = []
        for t in range(2):
            blk = sq[:, LANES * t:LANES * (t + 1)]
            if width == LANES:
                parts.append(jnp.sum(blk, axis=-1, keepdims=True))
            else:
                parts.append(jnp.sum(jnp.where(lane < width, blk, 0.0), axis=-1, keepdims=True))
                parts.append(jnp.sum(jnp.where(lane < width, 0.0, blk), axis=-1, keepdims=True))
        ss = parts[-1]
        for k in range(len(parts) - 2, -1, -1):
            ss = jnp.where(lane2 < (k + 1) * width, parts[k], ss)
        sums.append(ss)
    normed = [x * lax.rsqrt(ss * (1.0 / n_real) + EPS) * gain
              for (x, gain, _, n_real, *_), ss in zip(jobs, sums)]
    rolled = [_dot(y.astype(bf16), job[6]) for job, y in zip(jobs, normed)]
    for (_, _, _, _, cos, sin, _, dests), y, yr in zip(jobs, normed, rolled):
        out = (y * cos + yr * sin).astype(bf16)
        for dst_ref, dst_cols, src_cols in dests:
            dst_ref[:, dst_cols] = out[:, src_cols]


def _inproj(l, h, tab, win, bias, cqn, ckvn, wuq, wukv, hg, wkg, gb):
    tpb = T // TM_IN
    row = lambda w: pl.BlockSpec((TM_IN, w), lambda i: (i, 0))
    chunked = lambda r: pl.BlockSpec((TM_IN // A_CHUNK, r, A_CHUNK), lambda i: (i, 0, 0))
    out_w = [(B_HEADS * LANES, bf16), (B_HEADS * LANES, bf16), (B_HEADS * B_DV, bf16),
             (C_HEADS * C_DH, bf16), (C_KV_HEADS * C_DH, bf16), (C_KV_HEADS * C_DH, bf16)]
    return pl.pallas_call(
        _inproj_kernel,
        grid=(N // TM_IN,),
        in_specs=[
            row(D), _layer(win, l), _layer(bias, l),
            pl.BlockSpec((TM_IN, 4 * LANES), lambda i: (jnp.where(i < NL // TM_IN, i % tpb, 0), 0)),
            _layer(cqn, l), _layer(ckvn, l), _layer(wuq, l), _layer(wukv, l),
            _layer(hg, l), _layer(wkg, l), _layer(gb, l),
        ],
        out_specs=[row(PA_W), chunked(LANES), chunked(2 * N_GATES)] + [row(w) for w, _ in out_w],
        out_shape=[jax.ShapeDtypeStruct((N, PA_W), f32),
                   jax.ShapeDtypeStruct((N // A_CHUNK, LANES, A_CHUNK), f32),
                   jax.ShapeDtypeStruct((N // A_CHUNK, 2 * N_GATES, A_CHUNK), f32)]
        + [jax.ShapeDtypeStruct((N, w), dt) for w, dt in out_w],
        compiler_params=_params(1),
        name="inproj",
    )(h, win, bias, tab, cqn, ckvn, wuq, wukv, hg, wkg, gb)


def _cummax_rows(x, rev):
    n = x.shape[0]
    row = lax.broadcasted_iota(jnp.int32, (n, 1), 0)
    sh = 1
    while sh < n:
        if rev:
            x = jnp.maximum(x, jnp.where(row < n - sh, pltpu.roll(x, n - sh, 0), -jnp.inf))
        else:
            x = jnp.maximum(x, jnp.where(row >= sh, pltpu.roll(x, sh, 0), -jnp.inf))
        sh *= 2
    return x


def _stack_heads(pieces):
    return jnp.concatenate(pieces, axis=0)


def _mlstm_kernel(*refs):
    streams = [(refs[0:7] + refs[14:15], False), (refs[7:14] + refs[15:16], True)]
    s_ref, ml_ref, ms_ref = refs[16:19]

    @pl.when(pl.program_id(1) == 0)
    def _():
        s_ref[...] = jnp.zeros_like(s_ref)
        ml_ref[...] = jnp.zeros_like(ml_ref)
        ms_ref[...] = jnp.zeros_like(ms_ref)

    L = A_CHUNK
    n_chunks = MLSTM_R // L
    heads = range(A_HEADS)
    ti = lax.broadcasted_iota(jnp.int32, (L, L), 0)
    si = lax.broadcasted_iota(jnp.int32, (L, L), 1)
    lane = lax.broadcasted_iota(jnp.int32, (1, LANES), 1)
    row8 = lax.broadcasted_iota(jnp.int32, (2 * A_HEADS, 1), 0)
    in_head = [(lane >= hh * A_DK) & (lane < (hh + 1) * A_DK) for hh in heads]
    ones_blk = jnp.ones((L, LANES), bf16)

    items = []
    for sidx, (srefs, rev) in enumerate(streams):
        q_ref, kt_ref, v_ref, ig_ref, lf_ref, gt_ref, gts_ref, h_ref = srefs
        attend = (si >= ti) if rev else (si <= ti)
        attend4 = _stack_heads([attend] * A_HEADS)
        cum_cols = attend.astype(bf16)
        cum_rows = ((ti >= si) if rev else (ti <= si)).astype(bf16)
        m_lane = ml_ref[sidx, 0:1, :]
        m_sub = ms_ref[sidx, :, 0:1]
        for cc in (range(n_chunks - 1, -1, -1) if rev else range(n_chunks)):
            rows = slice(cc * L, (cc + 1) * L)
            ig = ig_ref[rows, :]
            lf = lf_ref[rows, :]
            gt = gt_ref[cc]
            gts = gts_ref[cc]
            b_col = sum(_dot(cum_cols, piece) for piece in _split3(lf))
            r_col = ig - b_col
            big_m = jnp.maximum(m_lane, _cummax_rows(r_col, rev))
            mt_col = b_col + big_m
            b_last_l = jnp.sum(lf, axis=0, keepdims=True)
            m_new_l = jnp.maximum(m_lane, jnp.max(r_col, axis=0, keepdims=True)) + b_last_l
            b_rows = sum(_dot(piece, cum_rows) for piece in _split3(gts))
            live = row8 < A_HEADS
            r8 = jnp.where(live, gt - b_rows, 0.0)
            b_last_s = jnp.where(live, jnp.sum(gts, axis=-1, keepdims=True), 0.0)
            r_max_s = jnp.max(r8, axis=-1, keepdims=True)
            wg8 = jnp.exp(r8 - r_max_s)
            m_new_s = jnp.maximum(m_sub, r_max_s) + b_last_s
            decay_s = jnp.exp(b_last_s + m_sub - m_new_s)
            scale_s = jnp.exp(b_last_s + r_max_s - m_new_s)
            expand = lambda a, n: _stack_heads([jnp.broadcast_to(a[hh:hh + 1, :], (n, a.shape[1])) for hh in heads])
            q = q_ref[rows, :]
            big_m_b = _stack_heads([jnp.broadcast_to(big_m[:, hh:hh + 1], (L, LANES)) for hh in heads])
            mt_b = _stack_heads([jnp.broadcast_to(mt_col[:, hh:hh + 1], (L, LANES)) for hh in heads])
            m_old_b = _stack_heads([jnp.broadcast_to(m_lane[:, hh:hh + 1], (L, LANES)) for hh in heads])
            items.append(dict(
                sidx=sidx, rows=rows, h_ref=h_ref,
                qst=_stack_heads([jnp.where(in_head[hh], q, 0.0) for hh in heads]).astype(bf16),
                kt=kt_ref[cc].astype(bf16),
                kw=(kt_ref[cc] * expand(wg8, A_DK)).astype(bf16),
                vo=jnp.concatenate([v_ref[rows, :].astype(bf16), ones_blk], axis=1),
                w=jnp.exp(jnp.where(attend4, expand(r8, L) - big_m_b[:, 0:L], -jnp.inf)),
                a_inter=jnp.exp(m_old_b - big_m_b), floor=jnp.exp(-mt_b),
                decay=jnp.broadcast_to(expand(decay_s, A_DK), (LANES, LANES)),
                kv_scale=jnp.broadcast_to(expand(scale_s, A_DK), (LANES, LANES))))
            m_lane, m_sub = m_new_l, m_new_s
        ml_ref[sidx, 0:1, :] = m_lane
        ms_ref[sidx, :, 0:1] = m_sub

    for it in items:
        it["s"] = _dot(it["qst"], it["kt"])

    for it in items:
        it["kv"] = _dot(it["kw"], it["vo"])

    for it in items:
        it["p"] = (it["s"] * it["w"]).astype(bf16)

    state = [s_ref[0], s_ref[1]]
    tile3 = lambda a: jnp.concatenate([a] * 3, axis=1)
    for it in items:
        st = state[it["sidx"]]
        it["c_in"] = st.astype(bf16)
        state[it["sidx"]] = tile3(it["decay"]) * st + tile3(it["kv_scale"]) * it["kv"]
    s_ref[0] = state[0]
    s_ref[1] = state[1]

    nv = A_HEADS * A_DV
    for it in items:
        out = tile3(it["a_inter"]) * _dot(it["qst"], it["c_in"]) + _dot(it["p"], it["vo"])
        res = out[:, 0:nv] / tile3(jnp.maximum(jnp.abs(out[:, nv:]), it["floor"]))[:, 0:nv]
        for pair in range(A_HEADS // 2):
            sl = slice(LANES * pair, LANES * (pair + 1))
            even = res[L * 2 * pair:L * (2 * pair + 1), sl]
            odd = res[L * (2 * pair + 1):L * (2 * pair + 2), sl]
            it["h_ref"][it["rows"], sl] = jnp.where(lane < A_DV, even, odd)


def _mlstm(pa, kt, gt):
    nb = T // MLSTM_R
    nc = MLSTM_R // A_CHUNK
    assert CTX == MLSTM_R

    def rb(rev):
        def f(b, j):
            jj = j - 1
            return jnp.where(j == 0, NL // MLSTM_R + b, b * nb + (nb - 1 - jj if rev else jj))
        return f

    def stream_specs(rev):
        r = rb(rev)
        d = 2 if rev else 0
        col = lambda w, c: pl.BlockSpec((MLSTM_R, w), lambda b, j: (r(b, j), c))
        return [
            col(LANES, PA_Q // LANES),
            pl.BlockSpec((nc, LANES, A_CHUNK), lambda b, j: (r(b, j), 0, 0)),
            col(2 * LANES, PA_V // (2 * LANES)),
            col(LANES, PA_G // LANES + d), col(LANES, PA_G // LANES + d + 1),
            pl.BlockSpec((nc, 2 * A_HEADS, A_CHUNK), lambda b, j: (r(b, j), d // 2, 0)),
            pl.BlockSpec((nc, 2 * A_HEADS, A_CHUNK), lambda b, j: (r(b, j), 2 + d // 2, 0)),
        ]

    out_spec = lambda rev: pl.BlockSpec((MLSTM_R, A_HEADS * A_DV), lambda b, j: (rb(rev)(b, j), 0))
    return pl.pallas_call(
        _mlstm_kernel,
        grid=(B, nb + 1),
        in_specs=stream_specs(False) + stream_specs(True),
        out_specs=[out_spec(False), out_spec(True)],
        out_shape=[jax.ShapeDtypeStruct((N, A_HEADS * A_DV), f32)] * 2,
        scratch_shapes=[pltpu.VMEM((2, LANES, 3 * LANES), f32), pltpu.VMEM((2, 8, LANES), f32),
                        pltpu.VMEM((2, 8, LANES), f32)],
        compiler_params=_params(2),
        name="mlstm",
    )(*[pa, kt, pa, pa, pa, gt, gt] * 2)


def _tile_max(s, m128):
    for t in range(s.shape[1] // LANES):
        blk = s[:, LANES * t:LANES * (t + 1)]
        m128 = blk if m128 is None else jnp.maximum(m128, blk)
    return m128


def _mla_kernel(*refs, latent):
    if latent:
        q_ref, kc_ref, vc_ref, kl_ref, vl_ref, o_ref, s_ref = refs
        sources = [(kc_ref, vc_ref, 0, CTX)] + [(kl_ref, vl_ref, c, MLA_KC) for c in range(0, T, MLA_KC)]
    else:
        q_ref, kc_ref, vc_ref, o_ref, s_ref = refs
        sources = [(kc_ref, vc_ref, 0, CTX)]
    q = q_ref[...]
    lane = lax.broadcasted_iota(jnp.int32, (1, LANES), 1)
    row_max = []
    for hh in range(2):
        sl = slice(LANES * hh, LANES * (hh + 1))
        qh = q[:, sl]
        m128 = None
        off = 0
        for k_ref, _, r0, n in sources:
            s = _dot_nt(qh, k_ref[r0:r0 + n, sl])
            s_ref[hh, :, off:off + n] = s
            m128 = _tile_max(s, m128)
            off += n
        row_max.append(jnp.max(m128, axis=-1, keepdims=True))
    outs = []
    for hh in range(2):
        den_lane = B_DV if hh == 0 else 0
        acc = None
        off = 0
        for _, v_ref, r0, n in sources:
            p = jnp.exp2(s_ref[hh, :, off:off + n] - row_max[hh]).astype(bf16)
            vext = jnp.where(lane == den_lane, 1.0, v_ref[r0:r0 + n, :]).astype(bf16)
            part = _dot(p, vext)
            acc = part if acc is None else acc + part
            off += n
        outs.append(acc / acc[:, den_lane:den_lane + 1])
    o_ref[...] = jnp.where(lane < B_DV, outs[0], outs[1]).astype(o_ref.dtype)


def _mla(qb, kb, vb, latent):
    npair = B_HEADS // 2
    ctx_blk = NL // CTX
    kv_specs = [
        pl.BlockSpec((CTX, 2 * LANES), lambda b, p, i: (ctx_blk + b, p)),
        pl.BlockSpec((CTX, LANES), lambda b, p, i: (ctx_blk + b, p)),
    ]
    if latent:
        tq = MLA_TQ
        nq = T // tq
        qmap = omap = lambda b, p, i: (b * nq + i, p)
        kv_specs += [
            pl.BlockSpec((T, 2 * LANES), lambda b, p, i: (b, p)),
            pl.BlockSpec((T, LANES), lambda b, p, i: (b, p)),
        ]
        args = (qb, kb, vb, kb, vb)
        nkeys = CTX + T
    else:
        tq = CTX
        nq = 1
        qmap = lambda b, p, i: (ctx_blk + b, p)
        omap = lambda b, p, i: (b, p)
        args = (qb, kb, vb)
        nkeys = CTX
    return pl.pallas_call(
        functools.partial(_mla_kernel, latent=latent),
        grid=(B, npair, nq),
        in_specs=[pl.BlockSpec((tq, 2 * LANES), qmap)] + kv_specs,
        out_specs=pl.BlockSpec((tq, LANES), omap),
        out_shape=jax.ShapeDtypeStruct((NL if latent else NCX, B_HEADS * B_DV), bf16),
        scratch_shapes=[pltpu.VMEM((2, tq, nkeys), f32)],
        compiler_params=_params(3),
        name="mla_latent" if latent else "mla_context",
    )(*args)


def _gqa_kernel(sink_ref, *refs, latent):
    if latent:
        q_ref, kc_ref, vc_ref, kl_ref, vl_ref, o_ref = refs
    else:
        q_ref, kc_ref, vc_ref, o_ref = refs
    q = q_ref[...]
    tq = q.shape[0]
    lane = lax.broadcasted_iota(jnp.int32, (1, LANES), 1)
    keys = kc_ref[...]
    vals = vc_ref[...]
    valid = None
    if latent:
        n = pl.program_id(1)
        start = pl.multiple_of(jnp.clip(n * GQA_TQ - WINDOW, 0, T - GQA_BAND), WINDOW)
        keys = jnp.concatenate([keys, kl_ref[pl.ds(start, GQA_BAND), :]], axis=0)
        vals = jnp.concatenate([vals, vl_ref[pl.ds(start, GQA_BAND), :]], axis=0)
        qpos = n * GQA_TQ + lax.broadcasted_iota(jnp.int32, (tq, 1), 0)
        kidx = lax.broadcasted_iota(jnp.int32, (1, CTX + GQA_BAND), 1)
        valid = (kidx < CTX) | (jnp.abs(qpos - (start - CTX + kidx)) <= WINDOW)
    lo = lane < C_DH
    outs = [[], []]
    for kvh in range(C_KV_HEADS):
        mine = lo if kvh == 0 else ~lo
        qs = jnp.concatenate([jnp.where(mine, q[:, LANES * g:LANES * (g + 1)], 0) for g in range(C_GROUP)], axis=0)
        s_all = _dot_nt(qs, keys)
        den_lane = C_DH * (1 - kvh)
        vext = jnp.where(lane == den_lane, 1.0, vals).astype(bf16)
        for g in range(C_GROUP):
            s = s_all[g * tq:(g + 1) * tq, :]
            if latent:
                s = jnp.where(valid, s, -jnp.inf)
            sink = sink_ref[C_GROUP * kvh + g] * LOG2E
            m = jnp.maximum(sink, jnp.max(_tile_max(s, None), axis=-1, keepdims=True))
            acc = _dot(jnp.exp2(s - m).astype(bf16), vext)
            outs[kvh].append(acc / (jnp.exp2(sink - m) + acc[:, den_lane:den_lane + 1]))
    for g in range(C_GROUP):
        o_ref[:, LANES * g:LANES * (g + 1)] = jnp.where(lo, outs[0][g], outs[1][g]).astype(o_ref.dtype)


def _gqa(sink, qc, kc, vc, latent):
    ctx_blk = NL // CTX
    kv_specs = [
        pl.BlockSpec((CTX, LANES), lambda b, i: (ctx_blk + b, 0)),
        pl.BlockSpec((CTX, LANES), lambda b, i: (ctx_blk + b, 0)),
    ]
    if latent:
        tq = GQA_TQ
        nq = T // tq
        qmap = omap = lambda b, i: (b * nq + i, 0)
        kv_specs += [
            pl.BlockSpec((T, LANES), lambda b, i: (b, 0)),
            pl.BlockSpec((T, LANES), lambda b, i: (b, 0)),
        ]
        args = (sink, qc, kc, vc, kc, vc)
    else:
        tq = CTX
        nq = 1
        qmap = lambda b, i: (ctx_blk + b, 0)
        omap = lambda b, i: (b, 0)
        args = (sink, qc, kc, vc)
    return pl.pallas_call(
        functools.partial(_gqa_kernel, latent=latent),
        grid=(B, nq),
        in_specs=[pl.BlockSpec(memory_space=pltpu.SMEM), pl.BlockSpec((tq, C_HEADS * C_DH), qmap)] + kv_specs,
        out_specs=pl.BlockSpec((tq, C_HEADS * C_DH), omap),
        out_shape=jax.ShapeDtypeStruct((NL if latent else NCX, C_HEADS * C_DH), bf16),
        compiler_params=_params(2),
        name="gqa_latent" if latent else "gqa_context",
    )(*args)


def _pad_cols(w, width):
    return jnp.pad(w, ((0, 0), (0, width - w.shape[1])))


def _lane_runs(lane_map):
    runs = []
    for src in lane_map:
        src = int(src)
        if runs and ((src < 0 and runs[-1][0] < 0) or (src >= 0 and runs[-1][0] >= 0 and src == sum(runs[-1]))):
            runs[-1] = (runs[-1][0], runs[-1][1] + 1)
        else:
            runs.append((src, 1))
    return runs


def _place(w, lane_map):
    parts = [jnp.zeros(w.shape[:-1] + (n,), w.dtype) if s < 0 else w[..., s:s + n] for s, n in _lane_runs(lane_map)]
    return jnp.concatenate(parts, axis=-1)


def _place_heads(w, heads, lane_map):
    r = w.shape[0]
    return _place(w.reshape(r, heads, -1), lane_map).reshape(r, heads * LANES)


def _reorder_c_heads(w, axis):
    heads = jnp.split(w, C_HEADS, axis=axis)
    return jnp.concatenate([heads[h] for h in C_HEAD_ORDER], axis=axis)


def _arrange_w_in(w, map_b):
    o = np.cumsum((0, 128, 128, 256, 256, 16, 256, 128, 32, 384, 128, 128))
    part = lambda i: w[:, int(o[i]):int(o[i + 1])]
    kr = _place(part(7), np.where(map_b >= B_NOPE, map_b - B_NOPE, -1))
    return jnp.concatenate([
        part(0), part(2), part(3), _pad_cols(part(4), LANES),
        part(5), part(6), kr,
        _reorder_c_heads(part(8), 1), part(9), part(10),
    ], axis=1)


def _rope_tables(map_b, map_c):
    assert T == GRID_W * GRID_W
    pos = jnp.arange(GRID_W, dtype=f32)[:, None]
    small, by_col = [], []
    for lane_map, rope_start, half in ((map_b, B_NOPE, B_ROPE // 4), (map_c, 0, C_DH // 4)):
        rel = lane_map - rope_start
        in_rope = (lane_map >= 0) & (rel >= 0) & (rel < 4 * half)
        rel = np.where(in_rope, rel, 0)
        second = jnp.asarray((rel // half) % 2 == 1)[None, :]
        freq = ROPE_BASE ** (-jnp.asarray(rel % half, f32) / half)
        rot = jnp.asarray(in_rope)[None, :]
        ang = pos * freq[None, :]
        sin = jnp.sin(ang)
        small += [jnp.where(rot, jnp.cos(ang), 1.0), jnp.where(rot, jnp.where(second, sin, -sin), 0.0)]
        by_col += [rel >= 2 * half] * 2
    small = jnp.concatenate(small, axis=1)
    by_col = jnp.asarray(np.concatenate(by_col))[None, None, :]
    shape = (GRID_W, GRID_W, small.shape[1])
    full = jnp.where(by_col, jnp.broadcast_to(small[None], shape), jnp.broadcast_to(small[:, None], shape))
    return full.reshape(T, small.shape[1])


def kernel(x, c, ctx, c_ctx, ada_w, ada_b, norm_g, ffn1_wi, ffn1_wo, ffn2_wi, ffn2_wo, w_in, w_out,
           mlstm_gate_b, mlstm_out_norm, mla_cq_norm, mla_ckv_norm, mla_w_uq, mla_w_ukv, mla_q_norm, mla_k_norm,
           gqa_q_norm, gqa_k_norm, gqa_sink):
    map_b, map_c = _head_lane_map_b(), _head_lane_map_c()
    nope_map = np.where(map_b < B_NOPE, map_b, -1)
    q_scale_b, q_scale_c = B_DQK ** -0.5 * LOG2E, C_DH ** -0.5 * LOG2E

    def both_gate_orders(g):
        grp = [g[..., A_HEADS * i:A_HEADS * (i + 1)] for i in range(4)]
        return jnp.concatenate([g, grp[1], grp[0], grp[3], grp[2]], axis=-1)

    def arrange(w_in_l, gate_b_l, w_uq_l, w_ukv_l, bq_l, bk_l, cq_l, ck_l):
        ukv = w_ukv_l.reshape(B_KV_RANK, B_HEADS, B_NOPE + B_DV)
        gbq, gbk = _place(bq_l[None], map_b) * q_scale_b, _place(bk_l[None], map_b)
        gcq, gck = _place(cq_l[None], map_c) * q_scale_c, _place(ck_l[None], map_c)
        return dict(
            win=_arrange_w_in(w_in_l, map_b).astype(bf16),
            bias=_pad_cols(jnp.pad(gate_b_l[None], ((0, 0), (COL_AG, 0))), WP),
            wkg=jnp.concatenate([w_in_l[:, 128:256], both_gate_orders(w_in_l[:, 768:768 + N_GATES])],
                                axis=1).T.astype(bf16),
            gb=both_gate_orders(gate_b_l)[:, None],
            wuq=_place_heads(w_uq_l, B_HEADS, map_b).astype(bf16),
            wukv=jnp.concatenate([_place_heads(ukv[:, :, :B_NOPE].reshape(B_KV_RANK, -1), B_HEADS, nope_map),
                                  ukv[:, :, B_NOPE:].reshape(B_KV_RANK, -1)], axis=1).astype(bf16),
            hg=jnp.concatenate([jnp.concatenate(pair, axis=1)
                                for pair in ((gbq, gbq), (gbk, gbk), (gcq, gcq), (gcq, gck))]))

    pw = jax.vmap(arrange)(w_in, mlstm_gate_b, mla_w_uq, mla_w_ukv, mla_q_norm, mla_k_norm, gqa_q_norm, gqa_k_norm)
    wi1, wo1, wi2, wo2 = (w.astype(bf16) for w in (ffn1_wi, ffn1_wo, ffn2_wi, ffn2_wo))
    c_rows = A_HEADS * A_DV + B_HEADS * B_DV
    wout = jnp.concatenate([w_out[:, :c_rows], _reorder_c_heads(w_out[:, c_rows:], 1)], axis=1).astype(bf16)
    cqn, ckvn, onorm = mla_cq_norm[:, None, :], mla_ckv_norm[:, None, :], mlstm_out_norm[:, None, :]

    cc = jnp.concatenate([c, c_ctx[None, :], jnp.zeros((MOD_ROWS - B - 1, D), f32)], axis=0)
    mod = _ada(cc, ada_w, ada_b).reshape(DEPTH, MOD_ROWS, N_MOD, D)
    tab = _rope_tables(map_b, map_c)
    xs = (x.reshape(NL, D), ctx.reshape(NCX, D))

    for l in range(DEPTH):
        need_ctx = l < DEPTH - 1
        x1, h = _ffn1(l, xs, mod, norm_g, wi1, wo1)
        pa, kt, gt, qb, kb, vb, qc, kc, vc = _inproj(
            l, h, tab, pw["win"], pw["bias"], cqn, ckvn, pw["wuq"], pw["wukv"], pw["hg"], pw["wkg"], pw["gb"])
        hf, hb = _mlstm(pa, kt, gt)
        y_lat = (_mla(qb, kb, vb, True), _gqa(gqa_sink[l], qc, kc, vc, True))
        y_ctx = (_mla(qb, kb, vb, False), _gqa(gqa_sink[l], qc, kc, vc, False)) if need_ctx else None
        xs = _outproj_ffn(l, x1, mod, norm_g, hf, hb, pa, y_lat, y_ctx, onorm, wout, wi2, wo2)
    return xs.reshape(B, T, D)
```

```python
import functools
import math

import jax
import jax.numpy as jnp
import numpy as np
from jax import lax
from jax.experimental import pallas as pl
from jax.experimental.pallas import tpu as pltpu

f32 = jnp.float32
bf16 = jnp.bfloat16

D = 1024
B = 4
T = 4096
CTX = 256
DEPTH = 2
GRID_W = 64
ROPE_BASE = 10000.0
EPS = 1e-6
HALF = 0.5
N_MOD = 9
D_FF = 2816
A_HEADS, A_DK, A_DV, A_CHUNK = 4, 32, 64, 64
B_HEADS, B_Q_RANK, B_KV_RANK, B_NOPE, B_ROPE, B_DV = 6, 256, 128, 64, 32, 64
B_DQK = B_NOPE + B_ROPE
C_HEADS, C_KV_HEADS, C_DH, WINDOW = 6, 2, 64, 128
C_GROUP = C_HEADS // C_KV_HEADS

NL = B * T
NCX = B * CTX
N = NL + NCX

LANES = 128
HALF_LANES = LANES // 2
MOD_ROWS = 8
VMEM_LIMIT = 58 * 1024 * 1024
LOG2E = math.log2(math.e)

TM_FFN = 1024
FFN_SPLIT = 2
FF_CHUNK = 256
TM_IN = 512
ADA_TN = 1152
MLSTM_R = 256
MLA_TQ = 512
MLA_KC = 512
GQA_TQ = 256
GQA_BAND = GQA_TQ + 2 * WINDOW

COL_AQ, COL_AV, COL_AO, COL_AG = 0, 128, 384, 640
COL_BCQ, COL_BCKV, COL_BKR = 768, 1024, 1152
COL_CQ, COL_CK, COL_CV = 1280, 1664, 1792
WP = 1920
C_HEAD_ORDER = tuple(h for g in range(C_GROUP) for h in (g, C_GROUP + g))
PA_V, PA_O, PA_Q, PA_G = 0, 256, 512, 640
PA_W = PA_G + 4 * LANES
N_GATES = 4 * A_HEADS


def _head_lane_map_b():
    m = -np.ones(LANES, np.int64)
    m[0:8], m[8:16], m[16:64] = np.arange(64, 72), np.arange(80, 88), np.arange(0, 48)
    m[64:72], m[72:80], m[80:96] = np.arange(72, 80), np.arange(88, 96), np.arange(48, 64)
    return m


def _head_lane_map_c():
    return np.concatenate([np.arange(C_DH), np.arange(C_DH)])


def _sigmoid(x):
    return 1.0 / (1.0 + jnp.exp(-x))


def _log_sigmoid(x):
    return jnp.minimum(x, 0.0) - jnp.log(1.0 + jnp.exp(-jnp.abs(x)))


def _rms(x, g):
    ms = jnp.mean(x * x, axis=-1, keepdims=True)
    return x * lax.rsqrt(ms + EPS) * g


def _dot(a, b):
    return jnp.dot(a, b, preferred_element_type=f32)


def _dot_nt(a, b):
    return lax.dot_general(a, b, (((1,), (1,)), ((), ())), preferred_element_type=f32)


def _split3(x):
    hi = x.astype(bf16)
    r1 = x - hi.astype(f32)
    mid = r1.astype(bf16)
    return hi, mid, (r1 - mid.astype(f32)).astype(bf16)


def _layer(arr, l):
    rest = (0,) * (arr.ndim - 1)
    return pl.BlockSpec((None,) + arr.shape[1:], lambda *_: (l,) + rest, pipeline_mode=pl.Buffered(1))


def _mod_spec(l, tm):
    tpb = T // tm
    return pl.BlockSpec((None, 1, N_MOD, D), lambda i: (l, i // tpb, 0, 0))


def _params(n_axes):
    return pltpu.CompilerParams(dimension_semantics=("arbitrary",) * n_axes, vmem_limit_bytes=VMEM_LIMIT)


def _ada_kernel(c_ref, w_ref, b_ref, o_ref):
    c = c_ref[...]
    s = c * _sigmoid(c)
    pieces = _split3(s)
    s3 = jnp.concatenate([piece.astype(f32) for piece in pieces], axis=0).astype(bf16)
    w = w_ref[0]
    w_hi = w.astype(bf16)
    w_lo = (w - w_hi.astype(f32)).astype(bf16)
    r = _dot(s3, w_hi)
    o_ref[0] = (r[0:MOD_ROWS] + r[MOD_ROWS:2 * MOD_ROWS] + r[2 * MOD_ROWS:]
                + _dot(pieces[0], w_lo) + b_ref[0])


def _ada(cc, ada_w, ada_b):
    nt = (N_MOD * D) // ADA_TN
    return pl.pallas_call(
        _ada_kernel,
        grid=(DEPTH, nt),
        in_specs=[
            pl.BlockSpec((MOD_ROWS, D), lambda l, j: (0, 0)),
            pl.BlockSpec((1, D, ADA_TN), lambda l, j: (l, 0, j)),
            pl.BlockSpec((1, 1, ADA_TN), lambda l, j: (l, 0, j)),
        ],
        out_specs=pl.BlockSpec((1, MOD_ROWS, ADA_TN), lambda l, j: (l, 0, j)),
        out_shape=jax.ShapeDtypeStruct((DEPTH, MOD_ROWS, N_MOD * D), f32),
        compiler_params=_params(2),
        name="ada_mod",
    )(cc, ada_w, ada_b.reshape(DEPTH, 1, N_MOD * D))


def _sub_rows(k):
    return slice(k * (TM_FFN // FFN_SPLIT), (k + 1) * (TM_FFN // FFN_SPLIT))


def _ffn(xs, g, shift, scale, gate, wi_ref, wo_ref, finish):
    hs = [(_rms(x, g) * (1.0 + scale) + shift).astype(bf16) for x in xs]
    for k, (x, h) in enumerate(zip(xs, hs)):
        acc = None
        for c in range(D_FF // FF_CHUNK):
            lo, hi = c * FF_CHUNK, (c + 1) * FF_CHUNK
            gt = _dot(h, wi_ref[:, lo:hi])
            up = _dot(h, wi_ref[:, D_FF + lo:D_FF + hi])
            a = (gt * _sigmoid(gt) * up).astype(bf16)
            part = _dot(a, wo_ref[lo:hi, :])
            acc = part if acc is None else acc + part
        finish(k, x + HALF * gate * acc)


def _ffn1_kernel(*refs, split_input):
    if split_input:
        xl_ref, xc_ref, mod_ref, ng_ref, wi_ref, wo_ref, x1_ref, h_ref = refs
        is_latent = pl.program_id(0) < NL // TM_FFN
        xs = [jnp.where(is_latent, xl_ref[_sub_rows(k), :], xc_ref[_sub_rows(k), :]) for k in range(FFN_SPLIT)]
    else:
        x_ref, mod_ref, ng_ref, wi_ref, wo_ref, x1_ref, h_ref = refs
        xs = [x_ref[_sub_rows(k), :] for k in range(FFN_SPLIT)]
    mod = mod_ref[0]

    def finish(k, x1):
        x1_ref[_sub_rows(k), :] = x1
        h_ref[_sub_rows(k), :] = (_rms(x1, ng_ref[1:2, :]) * (1.0 + mod[4:5, :]) + mod[3:4, :]).astype(bf16)

    _ffn(xs, ng_ref[0:1, :], mod[0:1, :], mod[1:2, :], mod[2:3, :], wi_ref, wo_ref, finish)


def _ffn1(l, xs, mod, ng, wi, wo):
    split_input = isinstance(xs, tuple)
    nlt = NL // TM_FFN
    if split_input:
        assert NCX % TM_FFN == 0
        x_specs = [pl.BlockSpec((TM_FFN, D), lambda i: (jnp.minimum(i, nlt - 1), 0)),
                   pl.BlockSpec((TM_FFN, D), lambda i: (jnp.maximum(i - nlt, 0), 0))]
    else:
        xs = (xs,)
        x_specs = [pl.BlockSpec((TM_FFN, D), lambda i: (i, 0))]
    return pl.pallas_call(
        functools.partial(_ffn1_kernel, split_input=split_input),
        grid=(N // TM_FFN,),
        in_specs=x_specs + [_mod_spec(l, TM_FFN), _layer(ng, l), _layer(wi, l), _layer(wo, l)],
        out_specs=[pl.BlockSpec((TM_FFN, D), lambda i: (i, 0))] * 2,
        out_shape=[jax.ShapeDtypeStruct((N, D), f32), jax.ShapeDtypeStruct((N, D), bf16)],
        compiler_params=_params(1),
        name="ffn1",
    )(*xs, mod, ng, wi, wo)


def _outproj_ffn_kernel(*refs, with_ctx):
    if with_ctx:
        (x_ref, mod_ref, ng_ref, hf_ref, hb_ref, o_ref, ybl_ref, ycl_ref, ybc_ref, ycc_ref, on_ref, wout_ref,
         wi_ref, wo_ref, out_ref) = refs
        is_latent = pl.program_id(0) < NL // TM_FFN
        yb_of = lambda r: jnp.where(is_latent, ybl_ref[r, :], ybc_ref[r, :])
        yc_of = lambda r: jnp.where(is_latent, ycl_ref[r, :], ycc_ref[r, :])
    else:
        (x_ref, mod_ref, ng_ref, hf_ref, hb_ref, o_ref, yb_ref, yc_ref, on_ref, wout_ref,
         wi_ref, wo_ref, out_ref) = refs
        yb_of = lambda r: yb_ref[r, :]
        yc_of = lambda r: yc_ref[r, :]
    mod = mod_ref[0]
    head = lax.broadcasted_iota(jnp.int32, (1, A_HEADS * A_DV), 1) // A_DV

    def mixed(r):
        hs = hf_ref[r, :] + hb_ref[r, :]
        sq = hs * hs
        ms = jnp.zeros_like(hs)
        for hh in range(A_HEADS):
            sel = head == hh
            ssh = jnp.sum(jnp.where(sel, sq, 0.0), axis=-1, keepdims=True) * (1.0 / A_DV)
            ms = jnp.where(sel, ssh, ms)
        ya = _sigmoid(o_ref[r, :]) * (hs * lax.rsqrt(ms + EPS) * on_ref[...])
        y = jnp.concatenate([ya.astype(bf16), yb_of(r), yc_of(r)], axis=-1)
        return x_ref[r, :] + mod[5:6, :] * _dot(y, wout_ref[...])

    def finish(k, x3):
        out_ref[_sub_rows(k), :] = x3

    _ffn([mixed(_sub_rows(k)) for k in range(FFN_SPLIT)], ng_ref[2:3, :], mod[6:7, :], mod[7:8, :], mod[8:9, :],
         wi_ref, wo_ref, finish)


def _outproj_ffn(l, x1, mod, ng, hf, hb, pa, y_lat, y_ctx, onorm, wout, wi, wo):
    with_ctx = y_ctx is not None
    rows = N if with_ctx else NL
    nlt = NL // TM_FFN
    row = lambda w, c=0: pl.BlockSpec((TM_FFN, w), lambda i: (i, c))
    lat = lambda w: pl.BlockSpec((TM_FFN, w), lambda i: (jnp.minimum(i, nlt - 1), 0))
    ctx = lambda w: pl.BlockSpec((TM_FFN, w), lambda i: (jnp.maximum(i - nlt, 0), 0))
    y_specs = [lat(B_HEADS * B_DV), lat(C_HEADS * C_DH)]
    if with_ctx:
        y_specs += [ctx(B_HEADS * B_DV), ctx(C_HEADS * C_DH)]
    return pl.pallas_call(
        functools.partial(_outproj_ffn_kernel, with_ctx=with_ctx),
        grid=(rows // TM_FFN,),
        in_specs=[
            row(D), _mod_spec(l, TM_FFN), _layer(ng, l),
            row(A_HEADS * A_DV), row(A_HEADS * A_DV), row(A_HEADS * A_DV, PA_O // (A_HEADS * A_DV)),
        ] + y_specs + [_layer(onorm, l), _layer(wout, l), _layer(wi, l), _layer(wo, l)],
        out_specs=row(D),
        out_shape=jax.ShapeDtypeStruct((rows, D), f32),
        compiler_params=_params(1),
        name="outproj_ffn2",
    )(x1, mod, ng, hf, hb, pa, *y_lat, *(y_ctx or ()), onorm, wout, wi, wo)


def _inproj_kernel(h_ref, win_ref, bias_ref, tab_ref, cqn_ref, ckvn_ref, wuq_ref, wukv_ref, hg_ref,
                   wkg_ref, gb_ref,
                   pa_ref, kt_ref, gt_ref, qb_ref, kb_ref, vb_ref, qc_ref, kc_ref, vc_ref):
    h = h_ref[...]
    p = _dot(h, win_ref[...]) + bias_ref[...]

    pa_ref[:, PA_V:PA_Q] = p[:, COL_AV:COL_AG]
    pa_ref[:, PA_Q:PA_G] = p[:, COL_AQ:COL_AV] * (A_DK ** -0.5)
    graw = p[:, COL_AG:COL_AG + LANES]
    lane = lax.broadcasted_iota(jnp.int32, (1, LANES), 1)
    for kk in range(4):
        gk = graw if kk == 0 else pltpu.roll(graw, LANES - A_HEADS * kk, 1)
        if kk % 2 == 1:
            gk = _log_sigmoid(gk)
        pa_ref[:, PA_G + LANES * kk:PA_G + LANES * (kk + 1)] = jnp.where(lane < A_HEADS, gk, 0.0)

    grow = lax.broadcasted_iota(jnp.int32, (2 * N_GATES, 1), 0)
    is_forget = ((grow // A_HEADS) % 2 == 1) == (grow < N_GATES)
    for c in range(TM_IN // A_CHUNK):
        t = _dot_nt(wkg_ref[...], h[c * A_CHUNK:(c + 1) * A_CHUNK, :])
        kt_ref[c] = t[0:LANES, :]
        g = t[LANES:LANES + 2 * N_GATES, :] + gb_ref[...]
        gt_ref[c] = jnp.where(is_forget, _log_sigmoid(g), g)

    is_latent = pl.program_id(0) < NL // TM_IN
    tab = tab_ref[...]
    tb = [jnp.where(is_latent, tab[:, LANES * i:LANES * (i + 1)], 1.0 - (i % 2)) for i in range(4)]
    hg = hg_ref[...]

    cq = _rms(p[:, COL_BCQ:COL_BCKV], cqn_ref[...]).astype(bf16)
    ckv = _rms(p[:, COL_BCKV:COL_BKR], ckvn_ref[...]).astype(bf16)
    kr = p[:, COL_BKR:COL_CQ]
    q = _dot(cq, wuq_ref[...])
    kv = _dot(ckv, wukv_ref[...])
    vb_ref[...] = kv[:, B_HEADS * LANES:].astype(bf16)
    vc_ref[...] = p[:, COL_CV:COL_CV + LANES].astype(bf16)

    two = lambda a: jnp.concatenate([a, a], axis=1)
    pair = lambda a, j, first=0: a[:, first + 2 * LANES * j:first + 2 * LANES * (j + 1)]
    cos_b, sin_b, cos_c, sin_c = (two(t) for t in tb)
    kr2 = two(kr)
    ri = lax.broadcasted_iota(jnp.int32, (2 * LANES, 2 * LANES), 0)
    ci = lax.broadcasted_iota(jnp.int32, (2 * LANES, 2 * LANES), 1)
    swap_b = ((ri // LANES == ci // LANES) & (ri % LANES == (ci + HALF_LANES) % LANES)).astype(bf16)
    half_c = C_DH // 2
    swap_c = ((ri // half_c == ci // half_c) & (ri % half_c == (ci + half_c // 2) % half_c)).astype(bf16)
    full = slice(0, 2 * LANES)
    jobs = []
    for j in range(B_HEADS // 2):
        dst = slice(2 * LANES * j, 2 * LANES * (j + 1))
        jobs.append((pair(q, j), hg[0:1, :], LANES, B_DQK, cos_b, sin_b, swap_b, [(qb_ref, dst, full)]))
        jobs.append((pair(kv, j) + kr2, hg[1:2, :], LANES, B_DQK, cos_b, sin_b, swap_b, [(kb_ref, dst, full)]))
    jobs.append((pair(p, 0, COL_CQ), hg[2:3, :], C_DH, C_DH, cos_c, sin_c, swap_c, [(qc_ref, full, full)]))
    jobs.append((pair(p, 1, COL_CQ), hg[3:4, :], C_DH, C_DH, cos_c, sin_c, swap_c,
                 [(qc_ref, slice(2 * LANES, 3 * LANES), slice(0, LANES)),
                  (kc_ref, slice(0, LANES), slice(LANES, 2 * LANES))]))
    lane = lax.broadcasted_iota(jnp.int32, (1, LANES), 1)
    lane2 = lax.broadcasted_iota(jnp.int32, (1, 2 * LANES), 1)
    sums = []
    for x, _, width, *_ in jobs:
        sq = x * x
        parts = []
        for t in range(2):
            blk = sq[:, LANES * t:LANES * (t + 1)]
            if width == LANES:
                parts.append(jnp.sum(blk, axis=-1, keepdims=True))
            else:
                parts.append(jnp.sum(jnp.where(lane < width, blk, 0.0), axis=-1, keepdims=True))
                parts.append(jnp.sum(jnp.where(lane < width, 0.0, blk), axis=-1, keepdims=True))
        ss = parts[-1]
        for k in range(len(parts) - 2, -1, -1):
            ss = jnp.where(lane2 < (k + 1) * width, parts[k], ss)
        sums.append(ss)
    normed = [x * lax.rsqrt(ss * (1.0 / n_real) + EPS) * gain
              for (x, gain, _, n_real, *_), ss in zip(jobs, sums)]
    rolled = [_dot(y.astype(bf16), job[6]) for job, y in zip(jobs, normed)]
    for (_, _, _, _, cos, sin, _, dests), y, yr in zip(jobs, normed, rolled):
        out = (y * cos + yr * sin).astype(bf16)
        for dst_ref, dst_cols, src_cols in dests:
            dst_ref[:, dst_cols] = out[:, src_cols]


def _inproj(l, h, tab, win, bias, cqn, ckvn, wuq, wukv, hg, wkg, gb):
    tpb = T // TM_IN
    row = lambda w: pl.BlockSpec((TM_IN, w), lambda i: (i, 0))
    chunked = lambda r: pl.BlockSpec((TM_IN // A_CHUNK, r, A_CHUNK), lambda i: (i, 0, 0))
    out_w = [(B_HEADS * LANES, bf16), (B_HEADS * LANES, bf16), (B_HEADS * B_DV, bf16),
             (C_HEADS * C_DH, bf16), (C_KV_HEADS * C_DH, bf16), (C_KV_HEADS * C_DH, bf16)]
    return pl.pallas_call(
        _inproj_kernel,
        grid=(N // TM_IN,),
        in_specs=[
            row(D), _layer(win, l), _layer(bias, l),
            pl.BlockSpec((TM_IN, 4 * LANES), lambda i: (jnp.where(i < NL // TM_IN, i % tpb, 0), 0)),
            _layer(cqn, l), _layer(ckvn, l), _layer(wuq, l), _layer(wukv, l),
            _layer(hg, l), _layer(wkg, l), _layer(gb, l),
        ],
        out_specs=[row(PA_W), chunked(LANES), chunked(2 * N_GATES)] + [row(w) for w, _ in out_w],
        out_shape=[jax.ShapeDtypeStruct((N, PA_W), f32),
                   jax.ShapeDtypeStruct((N // A_CHUNK, LANES, A_CHUNK), f32),
                   jax.ShapeDtypeStruct((N // A_CHUNK, 2 * N_GATES, A_CHUNK), f32)]
        + [jax.ShapeDtypeStruct((N, w), dt) for w, dt in out_w],
        compiler_params=_params(1),
        name="inproj",
    )(h, win, bias, tab, cqn, ckvn, wuq, wukv, hg, wkg, gb)


def _cummax_rows(x, rev):
    n = x.shape[0]
    row = lax.broadcasted_iota(jnp.int32, (n, 1), 0)
    sh = 1
    while sh < n:
        if rev:
            x = jnp.maximum(x, jnp.where(row < n - sh, pltpu.roll(x, n - sh, 0), -jnp.inf))
        else:
            x = jnp.maximum(x, jnp.where(row >= sh, pltpu.roll(x, sh, 0), -jnp.inf))
        sh *= 2
    return x


def _stack_heads(pieces):
    return jnp.concatenate(pieces, axis=0)


def _mlstm_kernel(*refs):
    streams = [(refs[0:7] + refs[14:15], False), (refs[7:14] + refs[15:16], True)]
    s_ref, ml_ref, ms_ref = refs[16:19]

    @pl.when(pl.program_id(1) == 0)
    def _():
        s_ref[...] = jnp.zeros_like(s_ref)
        ml_ref[...] = jnp.zeros_like(ml_ref)
        ms_ref[...] = jnp.zeros_like(ms_ref)

    L = A_CHUNK
    n_chunks = MLSTM_R // L
    heads = range(A_HEADS)
    ti = lax.broadcasted_iota(jnp.int32, (L, L), 0)
    si = lax.broadcasted_iota(jnp.int32, (L, L), 1)
    lane = lax.broadcasted_iota(jnp.int32, (1, LANES), 1)
    row8 = lax.broadcasted_iota(jnp.int32, (2 * A_HEADS, 1), 0)
    in_head = [(lane >= hh * A_DK) & (lane < (hh + 1) * A_DK) for hh in heads]
    ones_blk = jnp.ones((L, LANES), bf16)

    items = []
    for sidx, (srefs, rev) in enumerate(streams):
        q_ref, kt_ref, v_ref, ig_ref, lf_ref, gt_ref, gts_ref, h_ref = srefs
        attend = (si >= ti) if rev else (si <= ti)
        attend4 = _stack_heads([attend] * A_HEADS)
        cum_cols = attend.astype(bf16)
        cum_rows = ((ti >= si) if rev else (ti <= si)).astype(bf16)
        m_lane = ml_ref[sidx, 0:1, :]
        m_sub = ms_ref[sidx, :, 0:1]
        for cc in (range(n_chunks - 1, -1, -1) if rev else range(n_chunks)):
            rows = slice(cc * L, (cc + 1) * L)
            ig = ig_ref[rows, :]
            lf = lf_ref[rows, :]
            gt = gt_ref[cc]
            gts = gts_ref[cc]
            b_col = sum(_dot(cum_cols, piece) for piece in _split3(lf))
            r_col = ig - b_col
            big_m = jnp.maximum(m_lane, _cummax_rows(r_col, rev))
            mt_col = b_col + big_m
            b_last_l = jnp.sum(lf, axis=0, keepdims=True)
            m_new_l = jnp.maximum(m_lane, jnp.max(r_col, axis=0, keepdims=True)) + b_last_l
            b_rows = sum(_dot(piece, cum_rows) for piece in _split3(gts))
            live = row8 < A_HEADS
            r8 = jnp.where(live, gt - b_rows, 0.0)
            b_last_s = jnp.where(live, jnp.sum(gts, axis=-1, keepdims=True), 0.0)
            r_max_s = jnp.max(r8, axis=-1, keepdims=True)
            wg8 = jnp.exp(r8 - r_max_s)
            m_new_s = jnp.maximum(m_sub, r_max_s) + b_last_s
            decay_s = jnp.exp(b_last_s + m_sub - m_new_s)
            scale_s = jnp.exp(b_last_s + r_max_s - m_new_s)
            expand = lambda a, n: _stack_heads([jnp.broadcast_to(a[hh:hh + 1, :], (n, a.shape[1])) for hh in heads])
            q = q_ref[rows, :]
            big_m_b = _stack_heads([jnp.broadcast_to(big_m[:, hh:hh + 1], (L, LANES)) for hh in heads])
            mt_b = _stack_heads([jnp.broadcast_to(mt_col[:, hh:hh + 1], (L, LANES)) for hh in heads])
            m_old_b = _stack_heads([jnp.broadcast_to(m_lane[:, hh:hh + 1], (L, LANES)) for hh in heads])
            items.append(dict(
                sidx=sidx, rows=rows, h_ref=h_ref,
                qst=_stack_heads([jnp.where(in_head[hh], q, 0.0) for hh in heads]).astype(bf16),
                kt=kt_ref[cc].astype(bf16),
                kw=(kt_ref[cc] * expand(wg8, A_DK)).astype(bf16),
                vo=jnp.concatenate([v_ref[rows, :].astype(bf16), ones_blk], axis=1),
                w=jnp.exp(jnp.where(attend4, expand(r8, L) - big_m_b[:, 0:L], -jnp.inf)),
                a_inter=jnp.exp(m_old_b - big_m_b), floor=jnp.exp(-mt_b),
                decay=jnp.broadcast_to(expand(decay_s, A_DK), (LANES, LANES)),
                kv_scale=jnp.broadcast_to(expand(scale_s, A_DK), (LANES, LANES))))
            m_lane, m_sub = m_new_l, m_new_s
        ml_ref[sidx, 0:1, :] = m_lane
        ms_ref[sidx, :, 0:1] = m_sub

    for it in items:
        it["s"] = _dot(it["qst"], it["kt"])

    for it in items:
        it["kv"] = _dot(it["kw"], it["vo"])

    for it in items:
        it["p"] = (it["s"] * it["w"]).astype(bf16)

    state = [s_ref[0], s_ref[1]]
    tile3 = lambda a: jnp.concatenate([a] * 3, axis=1)
    for it in items:
        st = state[it["sidx"]]
        it["c_in"] = st.astype(bf16)
        state[it["sidx"]] = tile3(it["decay"]) * st + tile3(it["kv_scale"]) * it["kv"]
    s_ref[0] = state[0]
    s_ref[1] = state[1]

    nv = A_HEADS * A_DV
    for it in items:
        out = tile3(it["a_inter"]) * _dot(it["qst"], it["c_in"]) + _dot(it["p"], it["vo"])
        res = out[:, 0:nv] / tile3(jnp.maximum(jnp.abs(out[:, nv:]), it["floor"]))[:, 0:nv]
        for pair in range(A_HEADS // 2):
            sl = slice(LANES * pair, LANES * (pair + 1))
            even = res[L * 2 * pair:L * (2 * pair + 1), sl]
            odd = res[L * (2 * pair + 1):L * (2 * pair + 2), sl]
            it["h_ref"][it["rows"], sl] = jnp.where(lane < A_DV, even, odd)


def _mlstm(pa, kt, gt):
    nb = T // MLSTM_R
    nc = MLSTM_R // A_CHUNK
    assert CTX == MLSTM_R

    def rb(rev):
        def f(b, j):
            jj = j - 1
            return jnp.where(j == 0, NL // MLSTM_R + b, b * nb + (nb - 1 - jj if rev else jj))
        return f

    def stream_specs(rev):
        r = rb(rev)
        d = 2 if rev else 0
        col = lambda w, c: pl.BlockSpec((MLSTM_R, w), lambda b, j: (r(b, j), c))
        return [
            col(LANES, PA_Q // LANES),
            pl.BlockSpec((nc, LANES, A_CHUNK), lambda b, j: (r(b, j), 0, 0)),
            col(2 * LANES, PA_V // (2 * LANES)),
            col(LANES, PA_G // LANES + d), col(LANES, PA_G // LANES + d + 1),
            pl.BlockSpec((nc, 2 * A_HEADS, A_CHUNK), lambda b, j: (r(b, j), d // 2, 0)),
            pl.BlockSpec((nc, 2 * A_HEADS, A_CHUNK), lambda b, j: (r(b, j), 2 + d // 2, 0)),
        ]

    out_spec = lambda rev: pl.BlockSpec((MLSTM_R, A_HEADS * A_DV), lambda b, j: (rb(rev)(b, j), 0))
    return pl.pallas_call(
        _mlstm_kernel,
        grid=(B, nb + 1),
        in_specs=stream_specs(False) + stream_specs(True),
        out_specs=[out_spec(False), out_spec(True)],
        out_shape=[jax.ShapeDtypeStruct((N, A_HEADS * A_DV), f32)] * 2,
        scratch_shapes=[pltpu.VMEM((2, LANES, 3 * LANES), f32), pltpu.VMEM((2, 8, LANES), f32),
                        pltpu.VMEM((2, 8, LANES), f32)],
        compiler_params=_params(2),
        name="mlstm",
    )(*[pa, kt, pa, pa, pa, gt, gt] * 2)


def _tile_max(s, m128):
    for t in range(s.shape[1] // LANES):
        blk = s[:, LANES * t:LANES * (t + 1)]
        m128 = blk if m128 is None else jnp.maximum(m128, blk)
    return m128


def _mla_kernel(*refs, latent):
    if latent:
        q_ref, kc_ref, vc_ref, kl_ref, vl_ref, o_ref, s_ref = refs
        sources = [(kc_ref, vc_ref, 0, CTX)] + [(kl_ref, vl_ref, c, MLA_KC) for c in range(0, T, MLA_KC)]
    else:
        q_ref, kc_ref, vc_ref, o_ref, s_ref = refs
        sources = [(kc_ref, vc_ref, 0, CTX)]
    q = q_ref[...]
    lane = lax.broadcasted_iota(jnp.int32, (1, LANES), 1)
    row_max = []
    for hh in range(2):
        sl = slice(LANES * hh, LANES * (hh + 1))
        qh = q[:, sl]
        m128 = None
        off = 0
        for k_ref, _, r0, n in sources:
            s = _dot_nt(qh, k_ref[r0:r0 + n, sl])
            s_ref[hh, :, off:off + n] = s
            m128 = _tile_max(s, m128)
            off += n
        row_max.append(jnp.max(m128, axis=-1, keepdims=True))
    outs = []
    for hh in range(2):
        den_lane = B_DV if hh == 0 else 0
        acc = None
        off = 0
        for _, v_ref, r0, n in sources:
            p = jnp.exp2(s_ref[hh, :, off:off + n] - row_max[hh]).astype(bf16)
            vext = jnp.where(lane == den_lane, 1.0, v_ref[r0:r0 + n, :]).astype(bf16)
            part = _dot(p, vext)
            acc = part if acc is None else acc + part
            off += n
        outs.append(acc / acc[:, den_lane:den_lane + 1])
    o_ref[...] = jnp.where(lane < B_DV, outs[0], outs[1]).astype(o_ref.dtype)


def _mla(qb, kb, vb, latent):
    npair = B_HEADS // 2
    ctx_blk = NL // CTX
    kv_specs = [
        pl.BlockSpec((CTX, 2 * LANES), lambda b, p, i: (ctx_blk + b, p)),
        pl.BlockSpec((CTX, LANES), lambda b, p, i: (ctx_blk + b, p)),
    ]
    if latent:
        tq = MLA_TQ
        nq = T // tq
        qmap = omap = lambda b, p, i: (b * nq + i, p)
        kv_specs += [
            pl.BlockSpec((T, 2 * LANES), lambda b, p, i: (b, p)),
            pl.BlockSpec((T, LANES), lambda b, p, i: (b, p)),
        ]
        args = (qb, kb, vb, kb, vb)
        nkeys = CTX + T
    else:
        tq = CTX
        nq = 1
        qmap = lambda b, p, i: (ctx_blk + b, p)
        omap = lambda b, p, i: (b, p)
        args = (qb, kb, vb)
        nkeys = CTX
    return pl.pallas_call(
        functools.partial(_mla_kernel, latent=latent),
        grid=(B, npair, nq),
        in_specs=[pl.BlockSpec((tq, 2 * LANES), qmap)] + kv_specs,
        out_specs=pl.BlockSpec((tq, LANES), omap),
        out_shape=jax.ShapeDtypeStruct((NL if latent else NCX, B_HEADS * B_DV), bf16),
        scratch_shapes=[pltpu.VMEM((2, tq, nkeys), f32)],
        compiler_params=_params(3),
        name="mla_latent" if latent else "mla_context",
    )(*args)


def _gqa_kernel(sink_ref, *refs, latent):
    if latent:
        q_ref, kc_ref, vc_ref, kl_ref, vl_ref, o_ref = refs
    else:
        q_ref, kc_ref, vc_ref, o_ref = refs
    q = q_ref[...]
    tq = q.shape[0]
    lane = lax.broadcasted_iota(jnp.int32, (1, LANES), 1)
    keys = kc_ref[...]
    vals = vc_ref[...]
    valid = None
    if latent:
        n = pl.program_id(1)
        start = pl.multiple_of(jnp.clip(n * GQA_TQ - WINDOW, 0, T - GQA_BAND), WINDOW)
        keys = jnp.concatenate([keys, kl_ref[pl.ds(start, GQA_BAND), :]], axis=0)
        vals = jnp.concatenate([vals, vl_ref[pl.ds(start, GQA_BAND), :]], axis=0)
        qpos = n * GQA_TQ + lax.broadcasted_iota(jnp.int32, (tq, 1), 0)
        kidx = lax.broadcasted_iota(jnp.int32, (1, CTX + GQA_BAND), 1)
        valid = (kidx < CTX) | (jnp.abs(qpos - (start - CTX + kidx)) <= WINDOW)
    lo = lane < C_DH
    outs = [[], []]
    for kvh in range(C_KV_HEADS):
        mine = lo if kvh == 0 else ~lo
        qs = jnp.concatenate([jnp.where(mine, q[:, LANES * g:LANES * (g + 1)], 0) for g in range(C_GROUP)], axis=0)
        s_all = _dot_nt(qs, keys)
        den_lane = C_DH * (1 - kvh)
        vext = jnp.where(lane == den_lane, 1.0, vals).astype(bf16)
        for g in range(C_GROUP):
            s = s_all[g * tq:(g + 1) * tq, :]
            if latent:
                s = jnp.where(valid, s, -jnp.inf)
            sink = sink_ref[C_GROUP * kvh + g] * LOG2E
            m = jnp.maximum(sink, jnp.max(_tile_max(s, None), axis=-1, keepdims=True))
            acc = _dot(jnp.exp2(s - m).astype(bf16), vext)
            outs[kvh].append(acc / (jnp.exp2(sink - m) + acc[:, den_lane:den_lane + 1]))
    for g in range(C_GROUP):
        o_ref[:, LANES * g:LANES * (g + 1)] = jnp.where(lo, outs[0][g], outs[1][g]).astype(o_ref.dtype)


def _gqa(sink, qc, kc, vc, latent):
    ctx_blk = NL // CTX
    kv_specs = [
        pl.BlockSpec((CTX, LANES), lambda b, i: (ctx_blk + b, 0)),
        pl.BlockSpec((CTX, LANES), lambda b, i: (ctx_blk + b, 0)),
    ]
    if latent:
        tq = GQA_TQ
        nq = T // tq
        qmap = omap = lambda b, i: (b * nq + i, 0)
        kv_specs += [
            pl.BlockSpec((T, LANES), lambda b, i: (b, 0)),
            pl.BlockSpec((T, LANES), lambda b, i: (b, 0)),
        ]
        args = (sink, qc, kc, vc, kc, vc)
    else:
        tq = CTX
        nq = 1
        qmap = lambda b, i: (ctx_blk + b, 0)
        omap = lambda b, i: (b, 0)
        args = (sink, qc, kc, vc)
    return pl.pallas_call(
        functools.partial(_gqa_kernel, latent=latent),
        grid=(B, nq),
        in_specs=[pl.BlockSpec(memory_space=pltpu.SMEM), pl.BlockSpec((tq, C_HEADS * C_DH), qmap)] + kv_specs,
        out_specs=pl.BlockSpec((tq, C_HEADS * C_DH), omap),
        out_shape=jax.ShapeDtypeStruct((NL if latent else NCX, C_HEADS * C_DH), bf16),
        compiler_params=_params(2),
        name="gqa_latent" if latent else "gqa_context",
    )(*args)


def _pad_cols(w, width):
    return jnp.pad(w, ((0, 0), (0, width - w.shape[1])))


def _lane_runs(lane_map):
    runs = []
    for src in lane_map:
        src = int(src)
        if runs and ((src < 0 and runs[-1][0] < 0) or (src >= 0 and runs[-1][0] >= 0 and src == sum(runs[-1]))):
            runs[-1] = (runs[-1][0], runs[-1][1] + 1)
        else:
            runs.append((src, 1))
    return runs


def _place(w, lane_map):
    parts = [jnp.zeros(w.shape[:-1] + (n,), w.dtype) if s < 0 else w[..., s:s + n] for s, n in _lane_runs(lane_map)]
    return jnp.concatenate(parts, axis=-1)


def _place_heads(w, heads, lane_map):
    r = w.shape[0]
    return _place(w.reshape(r, heads, -1), lane_map).reshape(r, heads * LANES)


def _reorder_c_heads(w, axis):
    heads = jnp.split(w, C_HEADS, axis=axis)
    return jnp.concatenate([heads[h] for h in C_HEAD_ORDER], axis=axis)


def _arrange_w_in(w, map_b):
    o = np.cumsum((0, 128, 128, 256, 256, 16, 256, 128, 32, 384, 128, 128))
    part = lambda i: w[:, int(o[i]):int(o[i + 1])]
    kr = _place(part(7), np.where(map_b >= B_NOPE, map_b - B_NOPE, -1))
    return jnp.concatenate([
        part(0), part(2), part(3), _pad_cols(part(4), LANES),
        part(5), part(6), kr,
        _reorder_c_heads(part(8), 1), part(9), part(10),
    ], axis=1)


def _rope_tables(map_b, map_c):
    assert T == GRID_W * GRID_W
    pos = jnp.arange(GRID_W, dtype=f32)[:, None]
    small, by_col = [], []
    for lane_map, rope_start, half in ((map_b, B_NOPE, B_ROPE // 4), (map_c, 0, C_DH // 4)):
        rel = lane_map - rope_start
        in_rope = (lane_map >= 0) & (rel >= 0) & (rel < 4 * half)
        rel = np.where(in_rope, rel, 0)
        second = jnp.asarray((rel // half) % 2 == 1)[None, :]
        freq = ROPE_BASE ** (-jnp.asarray(rel % half, f32) / half)
        rot = jnp.asarray(in_rope)[None, :]
        ang = pos * freq[None, :]
        sin = jnp.sin(ang)
        small += [jnp.where(rot, jnp.cos(ang), 1.0), jnp.where(rot, jnp.where(second, sin, -sin), 0.0)]
        by_col += [rel >= 2 * half] * 2
    small = jnp.concatenate(small, axis=1)
    by_col = jnp.asarray(np.concatenate(by_col))[None, None, :]
    shape = (GRID_W, GRID_W, small.shape[1])
    full = jnp.where(by_col, jnp.broadcast_to(small[None], shape), jnp.broadcast_to(small[:, None], shape))
    return full.reshape(T, small.shape[1])


def kernel(x, c, ctx, c_ctx, ada_w, ada_b, norm_g, ffn1_wi, ffn1_wo, ffn2_wi, ffn2_wo, w_in, w_out,
           mlstm_gate_b, mlstm_out_norm, mla_cq_norm, mla_ckv_norm, mla_w_uq, mla_w_ukv, mla_q_norm, mla_k_norm,
           gqa_q_norm, gqa_k_norm, gqa_sink):
    map_b, map_c = _head_lane_map_b(), _head_lane_map_c()
    nope_map = np.where(map_b < B_NOPE, map_b, -1)
    q_scale_b, q_scale_c = B_DQK ** -0.5 * LOG2E, C_DH ** -0.5 * LOG2E

    def both_gate_orders(g):
        grp = [g[..., A_HEADS * i:A_HEADS * (i + 1)] for i in range(4)]
        return jnp.concatenate([g, grp[1], grp[0], grp[3], grp[2]], axis=-1)

    def arrange(w_in_l, gate_b_l, w_uq_l, w_ukv_l, bq_l, bk_l, cq_l, ck_l):
        ukv = w_ukv_l.reshape(B_KV_RANK, B_HEADS, B_NOPE + B_DV)
        gbq, gbk = _place(bq_l[None], map_b) * q_scale_b, _place(bk_l[None], map_b)
        gcq, gck = _place(cq_l[None], map_c) * q_scale_c, _place(ck_l[None], map_c)
        return dict(
            win=_arrange_w_in(w_in_l, map_b).astype(bf16),
            bias=_pad_cols(jnp.pad(gate_b_l[None], ((0, 0), (COL_AG, 0))), WP),
            wkg=jnp.concatenate([w_in_l[:, 128:256], both_gate_orders(w_in_l[:, 768:768 + N_GATES])],
                                axis=1).T.astype(bf16),
            gb=both_gate_orders(gate_b_l)[:, None],
            wuq=_place_heads(w_uq_l, B_HEADS, map_b).astype(bf16),
            wukv=jnp.concatenate([_place_heads(ukv[:, :, :B_NOPE].reshape(B_KV_RANK, -1), B_HEADS, nope_map),
                                  ukv[:, :, B_NOPE:].reshape(B_KV_RANK, -1)], axis=1).astype(bf16),
            hg=jnp.concatenate([jnp.concatenate(pair, axis=1)
                                for pair in ((gbq, gbq), (gbk, gbk), (gcq, gcq), (gcq, gck))]))

    pw = jax.vmap(arrange)(w_in, mlstm_gate_b, mla_w_uq, mla_w_ukv, mla_q_norm, mla_k_norm, gqa_q_norm, gqa_k_norm)
    wi1, wo1, wi2, wo2 = (w.astype(bf16) for w in (ffn1_wi, ffn1_wo, ffn2_wi, ffn2_wo))
    c_rows = A_HEADS * A_DV + B_HEADS * B_DV
    wout = jnp.concatenate([w_out[:, :c_rows], _reorder_c_heads(w_out[:, c_rows:], 1)], axis=1).astype(bf16)
    cqn, ckvn, onorm = mla_cq_norm[:, None, :], mla_ckv_norm[:, None, :], mlstm_out_norm[:, None, :]

    cc = jnp.concatenate([c, c_ctx[None, :], jnp.zeros((MOD_ROWS - B - 1, D), f32)], axis=0)
    mod = _ada(cc, ada_w, ada_b).reshape(DEPTH, MOD_ROWS, N_MOD, D)
    tab = _rope_tables(map_b, map_c)
    xs = (x.reshape(NL, D), ctx.reshape(NCX, D))

    for l in range(DEPTH):
        need_ctx = l < DEPTH - 1
        x1, h = _ffn1(l, xs, mod, norm_g, wi1, wo1)
        pa, kt, gt, qb, kb, vb, qc, kc, vc = _inproj(
            l, h, tab, pw["win"], pw["bias"], cqn, ckvn, pw["wuq"], pw["wukv"], pw["hg"], pw["wkg"], pw["gb"])
        hf, hb = _mlstm(pa, kt, gt)
        y_lat = (_mla(qb, kb, vb, True), _gqa(gqa_sink[l], qc, kc, vc, True))
        y_ctx = (_mla(qb, kb, vb, False), _gqa(gqa_sink[l], qc, kc, vc, False)) if need_ctx else None
        xs = _outproj_ffn(l, x1, mod, norm_g, hf, hb, pa, y_lat, y_ctx, onorm, wout, wi2, wo2)
    return xs.reshape(B, T, D)
```

```python
import functools
import math

import jax
import jax.numpy as jnp
import numpy as np
from jax import lax
from jax.experimental import pallas as pl
from jax.experimental.pallas import tpu as pltpu

f32 = jnp.float32
bf16 = jnp.bfloat16

D = 1024
B = 4
T = 4096
CTX = 256
DEPTH = 2
GRID_W = 64
ROPE_BASE = 10000.0
EPS = 1e-6
HALF = 0.5
N_MOD = 9
D_FF = 2816
A_HEADS, A_DK, A_DV, A_CHUNK = 4, 32, 64, 64
B_HEADS, B_Q_RANK, B_KV_RANK, B_NOPE, B_ROPE, B_DV = 6, 256, 128, 64, 32, 64
B_DQK = B_NOPE + B_ROPE
C_HEADS, C_KV_HEADS, C_DH, WINDOW = 6, 2, 64, 128
C_GROUP = C_HEADS // C_KV_HEADS

NL = B * T
NCX = B * CTX
N = NL + NCX

LANES = 128
HALF_LANES = LANES // 2
MOD_ROWS = 8
VMEM_LIMIT = 56 * 1024 * 1024
LOG2E = math.log2(math.e)

TM_FFN = 512
FF_CHUNK = 256
TM_IN = 512
ADA_TN = 1152
MLSTM_R = 256
MLA_TQ = 1024
MLA_KC = 512
GQA_TQ = 256
GQA_BAND = GQA_TQ + 2 * WINDOW

COL_AQ, COL_AV, COL_AO, COL_AG = 0, 128, 384, 640
COL_BCQ, COL_BCKV, COL_BKR = 768, 1024, 1152
COL_CQ, COL_CK, COL_CV = 1280, 1664, 1792
WP = 1920
C_HEAD_ORDER = tuple(h for g in range(C_GROUP) for h in (g, C_GROUP + g))
PA_V, PA_O, PA_Q, PA_G = 0, 256, 512, 640
PA_W = PA_G + 4 * LANES
N_GATES = 4 * A_HEADS


def _head_lane_map_b():
    m = -np.ones(LANES, np.int64)
    m[0:8], m[8:16], m[16:64] = np.arange(64, 72), np.arange(80, 88), np.arange(0, 48)
    m[64:72], m[72:80], m[80:96] = np.arange(72, 80), np.arange(88, 96), np.arange(48, 64)
    return m


def _head_lane_map_c():
    return np.concatenate([np.arange(C_DH), np.arange(C_DH)])


def _sigmoid(x):
    return 1.0 / (1.0 + jnp.exp(-x))


def _log_sigmoid(x):
    return jnp.minimum(x, 0.0) - jnp.log(1.0 + jnp.exp(-jnp.abs(x)))


def _rms(x, g):
    ms = jnp.mean(x * x, axis=-1, keepdims=True)
    return x * lax.rsqrt(ms + EPS) * g


def _dot(a, b):
    return jnp.dot(a, b, preferred_element_type=f32)


def _dot_nt(a, b):
    return lax.dot_general(a, b, (((1,), (1,)), ((), ())), preferred_element_type=f32)


def _split3(x):
    hi = x.astype(bf16)
    r1 = x - hi.astype(f32)
    mid = r1.astype(bf16)
    return hi, mid, (r1 - mid.astype(f32)).astype(bf16)


def _layer(arr, l):
    rest = (0,) * (arr.ndim - 1)
    return pl.BlockSpec((None,) + arr.shape[1:], lambda *_: (l,) + rest, pipeline_mode=pl.Buffered(1))


def _mod_spec(l, tm):
    tpb = T // tm
    return pl.BlockSpec((None, 1, N_MOD, D), lambda i: (l, i // tpb, 0, 0))


def _params(n_axes):
    return pltpu.CompilerParams(dimension_semantics=("arbitrary",) * n_axes, vmem_limit_bytes=VMEM_LIMIT)


def _ada_kernel(c_ref, w_ref, b_ref, o_ref):
    c = c_ref[...]
    s = c * _sigmoid(c)
    pieces = _split3(s)
    s3 = jnp.concatenate([piece.astype(f32) for piece in pieces], axis=0).astype(bf16)
    w = w_ref[0]
    w_hi = w.astype(bf16)
    w_lo = (w - w_hi.astype(f32)).astype(bf16)
    r = _dot(s3, w_hi)
    o_ref[0] = (r[0:MOD_ROWS] + r[MOD_ROWS:2 * MOD_ROWS] + r[2 * MOD_ROWS:]
                + _dot(pieces[0], w_lo) + b_ref[0])


def _ada(cc, ada_w, ada_b):
    nt = (N_MOD * D) // ADA_TN
    return pl.pallas_call(
        _ada_kernel,
        grid=(DEPTH, nt),
        in_specs=[
            pl.BlockSpec((MOD_ROWS, D), lambda l, j: (0, 0)),
            pl.BlockSpec((1, D, ADA_TN), lambda l, j: (l, 0, j)),
            pl.BlockSpec((1, 1, ADA_TN), lambda l, j: (l, 0, j)),
        ],
        out_specs=pl.BlockSpec((1, MOD_ROWS, ADA_TN), lambda l, j: (l, 0, j)),
        out_shape=jax.ShapeDtypeStruct((DEPTH, MOD_ROWS, N_MOD * D), f32),
        compiler_params=_params(2),
        name="ada_mod",
    )(cc, ada_w, ada_b.reshape(DEPTH, 1, N_MOD * D))


def _ffn(x, g, shift, scale, gate, wi_ref, wo_ref):
    h = (_rms(x, g) * (1.0 + scale) + shift).astype(bf16)
    acc = None
    for c in range(D_FF // FF_CHUNK):
        lo, hi = c * FF_CHUNK, (c + 1) * FF_CHUNK
        gt = _dot(h, wi_ref[:, lo:hi])
        up = _dot(h, wi_ref[:, D_FF + lo:D_FF + hi])
        a = (gt * _sigmoid(gt) * up).astype(bf16)
        part = _dot(a, wo_ref[lo:hi, :])
        acc = part if acc is None else acc + part
    return x + HALF * gate * acc


def _ffn1_kernel(*refs, split_input):
    if split_input:
        xl_ref, xc_ref, mod_ref, ng_ref, wi_ref, wo_ref, x1_ref, h_ref = refs
        x = jnp.where(pl.program_id(0) < NL // TM_FFN, xl_ref[...], xc_ref[...])
    else:
        x_ref, mod_ref, ng_ref, wi_ref, wo_ref, x1_ref, h_ref = refs
        x = x_ref[...]
    mod = mod_ref[0]
    x1 = _ffn(x, ng_ref[0:1, :], mod[0:1, :], mod[1:2, :], mod[2:3, :], wi_ref, wo_ref)
    x1_ref[...] = x1
    h_ref[...] = (_rms(x1, ng_ref[1:2, :]) * (1.0 + mod[4:5, :]) + mod[3:4, :]).astype(bf16)


def _ffn1(l, xs, mod, ng, wi, wo):
    split_input = isinstance(xs, tuple)
    nlt = NL // TM_FFN
    if split_input:
        assert NCX % TM_FFN == 0
        x_specs = [pl.BlockSpec((TM_FFN, D), lambda i: (jnp.minimum(i, nlt - 1), 0)),
                   pl.BlockSpec((TM_FFN, D), lambda i: (jnp.maximum(i - nlt, 0), 0))]
    else:
        xs = (xs,)
        x_specs = [pl.BlockSpec((TM_FFN, D), lambda i: (i, 0))]
    return pl.pallas_call(
        functools.partial(_ffn1_kernel, split_input=split_input),
        grid=(N // TM_FFN,),
        in_specs=x_specs + [_mod_spec(l, TM_FFN), _layer(ng, l), _layer(wi, l), _layer(wo, l)],
        out_specs=[pl.BlockSpec((TM_FFN, D), lambda i: (i, 0))] * 2,
        out_shape=[jax.ShapeDtypeStruct((N, D), f32), jax.ShapeDtypeStruct((N, D), bf16)],
        compiler_params=_params(1),
        name="ffn1",
    )(*xs, mod, ng, wi, wo)


def _outproj_ffn_kernel(*refs, with_ctx):
    if with_ctx:
        (x_ref, mod_ref, ng_ref, hf_ref, hb_ref, o_ref, ybl_ref, ycl_ref, ybc_ref, ycc_ref, on_ref, wout_ref,
         wi_ref, wo_ref, out_ref) = refs
        is_latent = pl.program_id(0) < NL // TM_FFN
        yb = jnp.where(is_latent, ybl_ref[...], ybc_ref[...])
        yc = jnp.where(is_latent, ycl_ref[...], ycc_ref[...])
    else:
        (x_ref, mod_ref, ng_ref, hf_ref, hb_ref, o_ref, yb_ref, yc_ref, on_ref, wout_ref,
         wi_ref, wo_ref, out_ref) = refs
        yb, yc = yb_ref[...], yc_ref[...]
    mod = mod_ref[0]
    hs = hf_ref[...] + hb_ref[...]
    sq = hs * hs
    head = lax.broadcasted_iota(jnp.int32, (1, A_HEADS * A_DV), 1) // A_DV
    ms = jnp.zeros_like(hs)
    for hh in range(A_HEADS):
        sel = head == hh
        ssh = jnp.sum(jnp.where(sel, sq, 0.0), axis=-1, keepdims=True) * (1.0 / A_DV)
        ms = jnp.where(sel, ssh, ms)
    ya = _sigmoid(o_ref[...]) * (hs * lax.rsqrt(ms + EPS) * on_ref[...])
    y = jnp.concatenate([ya.astype(bf16), yb, yc], axis=-1)
    x2 = x_ref[...] + mod[5:6, :] * _dot(y, wout_ref[...])
    out_ref[...] = _ffn(x2, ng_ref[2:3, :], mod[6:7, :], mod[7:8, :], mod[8:9, :], wi_ref, wo_ref)


def _outproj_ffn(l, x1, mod, ng, hf, hb, pa, y_lat, y_ctx, onorm, wout, wi, wo):
    with_ctx = y_ctx is not None
    rows = N if with_ctx else NL
    nlt = NL // TM_FFN
    row = lambda w, c=0: pl.BlockSpec((TM_FFN, w), lambda i: (i, c))
    lat = lambda w: pl.BlockSpec((TM_FFN, w), lambda i: (jnp.minimum(i, nlt - 1), 0))
    ctx = lambda w: pl.BlockSpec((TM_FFN, w), lambda i: (jnp.maximum(i - nlt, 0), 0))
    y_specs = [lat(B_HEADS * B_DV), lat(C_HEADS * C_DH)]
    if with_ctx:
        y_specs += [ctx(B_HEADS * B_DV), ctx(C_HEADS * C_DH)]
    return pl.pallas_call(
        functools.partial(_outproj_ffn_kernel, with_ctx=with_ctx),
        grid=(rows // TM_FFN,),
        in_specs=[
            row(D), _mod_spec(l, TM_FFN), _layer(ng, l),
            row(A_HEADS * A_DV), row(A_HEADS * A_DV), row(A_HEADS * A_DV, PA_O // (A_HEADS * A_DV)),
        ] + y_specs + [_layer(onorm, l), _layer(wout, l), _layer(wi, l), _layer(wo, l)],
        out_specs=row(D),
        out_shape=jax.ShapeDtypeStruct((rows, D), f32),
        compiler_params=_params(1),
        name="outproj_ffn2",
    )(x1, mod, ng, hf, hb, pa, *y_lat, *(y_ctx or ()), onorm, wout, wi, wo)


def _inproj_kernel(h_ref, win_ref, bias_ref, tab_ref, cqn_ref, ckvn_ref, wuq_ref, wukv_ref, hg_ref,
                   wkg_ref, gb_ref,
                   pa_ref, kt_ref, gt_ref, qb_ref, kb_ref, vb_ref, qc_ref, kc_ref, vc_ref):
    h = h_ref[...]
    p = _dot(h, win_ref[...]) + bias_ref[...]

    pa_ref[:, PA_V:PA_Q] = p[:, COL_AV:COL_AG]
    pa_ref[:, PA_Q:PA_G] = p[:, COL_AQ:COL_AV] * (A_DK ** -0.5)
    graw = p[:, COL_AG:COL_AG + LANES]
    lane = lax.broadcasted_iota(jnp.int32, (1, LANES), 1)
    for kk in range(4):
        gk = graw if kk == 0 else pltpu.roll(graw, LANES - A_HEADS * kk, 1)
        if kk % 2 == 1:
            gk = _log_sigmoid(gk)
        pa_ref[:, PA_G + LANES * kk:PA_G + LANES * (kk + 1)] = jnp.where(lane < A_HEADS, gk, 0.0)

    grow = lax.broadcasted_iota(jnp.int32, (2 * N_GATES, 1), 0)
    is_forget = ((grow // A_HEADS) % 2 == 1) == (grow < N_GATES)
    for c in range(TM_IN // A_CHUNK):
        t = _dot_nt(wkg_ref[...], h[c * A_CHUNK:(c + 1) * A_CHUNK, :])
        kt_ref[c] = t[0:LANES, :]
        g = t[LANES:LANES + 2 * N_GATES, :] + gb_ref[...]
        gt_ref[c] = jnp.where(is_forget, _log_sigmoid(g), g)

    is_latent = pl.program_id(0) < NL // TM_IN
    tab = tab_ref[...]
    tb = [jnp.where(is_latent, tab[:, LANES * i:LANES * (i + 1)], 1.0 - (i % 2)) for i in range(4)]
    hg = hg_ref[...]

    cq = _rms(p[:, COL_BCQ:COL_BCKV], cqn_ref[...]).astype(bf16)
    ckv = _rms(p[:, COL_BCKV:COL_BKR], ckvn_ref[...]).astype(bf16)
    kr = p[:, COL_BKR:COL_CQ]
    q = _dot(cq, wuq_ref[...])
    kv = _dot(ckv, wukv_ref[...])
    vb_ref[...] = kv[:, B_HEADS * LANES:].astype(bf16)
    vc_ref[...] = p[:, COL_CV:COL_CV + LANES].astype(bf16)

    two = lambda a: jnp.concatenate([a, a], axis=1)
    pair = lambda a, j, first=0: a[:, first + 2 * LANES * j:first + 2 * LANES * (j + 1)]
    cos_b, sin_b, cos_c, sin_c = (two(t) for t in tb)
    kr2 = two(kr)
    ri = lax.broadcasted_iota(jnp.int32, (2 * LANES, 2 * LANES), 0)
    ci = lax.broadcasted_iota(jnp.int32, (2 * LANES, 2 * LANES), 1)
    swap_b = ((ri // LANES == ci // LANES) & (ri % LANES == (ci + HALF_LANES) % LANES)).astype(bf16)
    half_c = C_DH // 2
    swap_c = ((ri // half_c == ci // half_c) & (ri % half_c == (ci + half_c // 2) % half_c)).astype(bf16)
    full = slice(0, 2 * LANES)
    jobs = []
    for j in range(B_HEADS // 2):
        dst = slice(2 * LANES * j, 2 * LANES * (j + 1))
        jobs.append((pair(q, j), hg[0:1, :], LANES, B_DQK, cos_b, sin_b, swap_b, [(qb_ref, dst, full)]))
        jobs.append((pair(kv, j) + kr2, hg[1:2, :], LANES, B_DQK, cos_b, sin_b, swap_b, [(kb_ref, dst, full)]))
    jobs.append((pair(p, 0, COL_CQ), hg[2:3, :], C_DH, C_DH, cos_c, sin_c, swap_c, [(qc_ref, full, full)]))
    jobs.append((pair(p, 1, COL_CQ), hg[3:4, :], C_DH, C_DH, cos_c, sin_c, swap_c,
                 [(qc_ref, slice(2 * LANES, 3 * LANES), slice(0, LANES)),
                  (kc_ref, slice(0, LANES), slice(LANES, 2 * LANES))]))
    lane = lax.broadcasted_iota(jnp.int32, (1, LANES), 1)
    lane2 = lax.broadcasted_iota(jnp.int32, (1, 2 * LANES), 1)
    sums = []
    for x, _, width, *_ in jobs:
        sq = x * x
        parts = []
        for t in range(2):
            blk = sq[:, LANES * t:LANES * (t + 1)]
            if width == LANES:
                parts.append(jnp.sum(blk, axis=-1, keepdims=True))
            else:
                parts.append(jnp.sum(jnp.where(lane < width, blk, 0.0), axis=-1, keepdims=True))
                parts.append(jnp.sum(jnp.where(lane < width, 0.0, blk), axis=-1, keepdims=True))
        ss = parts[-1]
        for k in range(len(parts) - 2, -1, -1):
            ss = jnp.where(lane2 < (k + 1) * width, parts[k], ss)
        sums.append(ss)
    normed = [x * lax.rsqrt(ss * (1.0 / n_real) + EPS) * gain
              for (x, gain, _, n_real, *_), ss in zip(jobs, sums)]
    rolled = [_dot(y.astype(bf16), job[6]) for job, y in zip(jobs, normed)]
    for (_, _, _, _, cos, sin, _, dests), y, yr in zip(jobs, normed, rolled):
        out = (y * cos + yr * sin).astype(bf16)
        for dst_ref, dst_cols, src_cols in dests:
            dst_ref[:, dst_cols] = out[:, src_cols]


def _inproj(l, h, tab, win, bias, cqn, ckvn, wuq, wukv, hg, wkg, gb):
    tpb = T // TM_IN
    row = lambda w: pl.BlockSpec((TM_IN, w), lambda i: (i, 0))
    chunked = lambda r: pl.BlockSpec((TM_IN // A_CHUNK, r, A_CHUNK), lambda i: (i, 0, 0))
    out_w = [(B_HEADS * LANES, bf16), (B_HEADS * LANES, bf16), (B_HEADS * B_DV, bf16),
             (C_HEADS * C_DH, bf16), (C_KV_HEADS * C_DH, bf16), (C_KV_HEADS * C_DH, bf16)]
    return pl.pallas_call(
        _inproj_kernel,
        grid=(N // TM_IN,),
        in_specs=[
            row(D), _layer(win, l), _layer(bias, l),
            pl.BlockSpec((TM_IN, 4 * LANES), lambda i: (jnp.where(i < NL // TM_IN, i % tpb, 0), 0)),
            _layer(cqn, l), _layer(ckvn, l), _layer(wuq, l), _layer(wukv, l),
            _layer(hg, l), _layer(wkg, l), _layer(gb, l),
        ],
        out_specs=[row(PA_W), chunked(LANES), chunked(2 * N_GATES)] + [row(w) for w, _ in out_w],
        out_shape=[jax.ShapeDtypeStruct((N, PA_W), f32),
                   jax.ShapeDtypeStruct((N // A_CHUNK, LANES, A_CHUNK), f32),
                   jax.ShapeDtypeStruct((N // A_CHUNK, 2 * N_GATES, A_CHUNK), f32)]
        + [jax.ShapeDtypeStruct((N, w), dt) for w, dt in out_w],
        compiler_params=_params(1),
        name="inproj",
    )(h, win, bias, tab, cqn, ckvn, wuq, wukv, hg, wkg, gb)


def _cummax_rows(x, rev):
    n = x.shape[0]
    row = lax.broadcasted_iota(jnp.int32, (n, 1), 0)
    sh = 1
    while sh < n:
        if rev:
            x = jnp.maximum(x, jnp.where(row < n - sh, pltpu.roll(x, n - sh, 0), -jnp.inf))
        else:
            x = jnp.maximum(x, jnp.where(row >= sh, pltpu.roll(x, sh, 0), -jnp.inf))
        sh *= 2
    return x


def _stack_heads(pieces):
    return jnp.concatenate(pieces, axis=0)


def _mlstm_kernel(*refs):
    streams = [(refs[0:7] + refs[14:15], False), (refs[7:14] + refs[15:16], True)]
    s_ref, ml_ref, ms_ref = refs[16:19]

    @pl.when(pl.program_id(1) == 0)
    def _():
        s_ref[...] = jnp.zeros_like(s_ref)
        ml_ref[...] = jnp.zeros_like(ml_ref)
        ms_ref[...] = jnp.zeros_like(ms_ref)

    L = A_CHUNK
    n_chunks = MLSTM_R // L
    heads = range(A_HEADS)
    ti = lax.broadcasted_iota(jnp.int32, (L, L), 0)
    si = lax.broadcasted_iota(jnp.int32, (L, L), 1)
    lane = lax.broadcasted_iota(jnp.int32, (1, LANES), 1)
    row8 = lax.broadcasted_iota(jnp.int32, (2 * A_HEADS, 1), 0)
    in_head = [(lane >= hh * A_DK) & (lane < (hh + 1) * A_DK) for hh in heads]
    ones_blk = jnp.ones((L, LANES), bf16)

    items = []
    for sidx, (srefs, rev) in enumerate(streams):
        q_ref, kt_ref, v_ref, ig_ref, lf_ref, gt_ref, gts_ref, h_ref = srefs
        attend = (si >= ti) if rev else (si <= ti)
        attend4 = _stack_heads([attend] * A_HEADS)
        cum_cols = attend.astype(bf16)
        cum_rows = ((ti >= si) if rev else (ti <= si)).astype(bf16)
        m_lane = ml_ref[sidx, 0:1, :]
        m_sub = ms_ref[sidx, :, 0:1]
        for cc in (range(n_chunks - 1, -1, -1) if rev else range(n_chunks)):
            rows = slice(cc * L, (cc + 1) * L)
            ig = ig_ref[rows, :]
            lf = lf_ref[rows, :]
            gt = gt_ref[cc]
            gts = gts_ref[cc]
            b_col = sum(_dot(cum_cols, piece) for piece in _split3(lf))
            r_col = ig - b_col
            big_m = jnp.maximum(m_lane, _cummax_rows(r_col, rev))
            mt_col = b_col + big_m
            b_last_l = jnp.sum(lf, axis=0, keepdims=True)
            m_new_l = jnp.maximum(m_lane, jnp.max(r_col, axis=0, keepdims=True)) + b_last_l
            b_rows = sum(_dot(piece, cum_rows) for piece in _split3(gts))
            live = row8 < A_HEADS
            r8 = jnp.where(live, gt - b_rows, 0.0)
            b_last_s = jnp.where(live, jnp.sum(gts, axis=-1, keepdims=True), 0.0)
            r_max_s = jnp.max(r8, axis=-1, keepdims=True)
            wg8 = jnp.exp(r8 - r_max_s)
            m_new_s = jnp.maximum(m_sub, r_max_s) + b_last_s
            decay_s = jnp.exp(b_last_s + m_sub - m_new_s)
            scale_s = jnp.exp(b_last_s + r_max_s - m_new_s)
            expand = lambda a, n: _stack_heads([jnp.broadcast_to(a[hh:hh + 1, :], (n, a.shape[1])) for hh in heads])
            q = q_ref[rows, :]
            big_m_b = _stack_heads([jnp.broadcast_to(big_m[:, hh:hh + 1], (L, LANES)) for hh in heads])
            mt_b = _stack_heads([jnp.broadcast_to(mt_col[:, hh:hh + 1], (L, LANES)) for hh in heads])
            m_old_b = _stack_heads([jnp.broadcast_to(m_lane[:, hh:hh + 1], (L, LANES)) for hh in heads])
            items.append(dict(
                sidx=sidx, rows=rows, h_ref=h_ref,
                qst=_stack_heads([jnp.where(in_head[hh], q, 0.0) for hh in heads]).astype(bf16),
                kt=kt_ref[cc].astype(bf16),
                kw=(kt_ref[cc] * expand(wg8, A_DK)).astype(bf16),
                vo=jnp.concatenate([v_ref[rows, :].astype(bf16), ones_blk], axis=1),
                w=jnp.exp(jnp.where(attend4, expand(r8, L) - big_m_b[:, 0:L], -jnp.inf)),
                a_inter=jnp.exp(m_old_b - big_m_b), floor=jnp.exp(-mt_b),
                decay=jnp.broadcast_to(expand(decay_s, A_DK), (LANES, LANES)),
                kv_scale=jnp.broadcast_to(expand(scale_s, A_DK), (LANES, LANES))))
            m_lane, m_sub = m_new_l, m_new_s
        ml_ref[sidx, 0:1, :] = m_lane
        ms_ref[sidx, :, 0:1] = m_sub

    for it in items:
        it["s"] = _dot(it["qst"], it["kt"])

    for it in items:
        it["kv"] = _dot(it["kw"], it["vo"])

    for it in items:
        it["p"] = (it["s"] * it["w"]).astype(bf16)

    state = [s_ref[0], s_ref[1]]
    tile3 = lambda a: jnp.concatenate([a] * 3, axis=1)
    for it in items:
        st = state[it["sidx"]]
        it["c_in"] = st.astype(bf16)
        state[it["sidx"]] = tile3(it["decay"]) * st + tile3(it["kv_scale"]) * it["kv"]
    s_ref[0] = state[0]
    s_ref[1] = state[1]

    nv = A_HEADS * A_DV
    for it in items:
        out = tile3(it["a_inter"]) * _dot(it["qst"], it["c_in"]) + _dot(it["p"], it["vo"])
        res = out[:, 0:nv] / tile3(jnp.maximum(jnp.abs(out[:, nv:]), it["floor"]))[:, 0:nv]
        for pair in range(A_HEADS // 2):
            sl = slice(LANES * pair, LANES * (pair + 1))
            even = res[L * 2 * pair:L * (2 * pair + 1), sl]
            odd = res[L * (2 * pair + 1):L * (2 * pair + 2), sl]
            it["h_ref"][it["rows"], sl] = jnp.where(lane < A_DV, even, odd)


def _mlstm(pa, kt, gt):
    nb = T // MLSTM_R
    nc = MLSTM_R // A_CHUNK
    assert CTX == MLSTM_R

    def rb(rev):
        def f(b, j):
            jj = j - 1
            return jnp.where(j == 0, NL // MLSTM_R + b, b * nb + (nb - 1 - jj if rev else jj))
        return f

    def stream_specs(rev):
        r = rb(rev)
        d = 2 if rev else 0
        col = lambda w, c: pl.BlockSpec((MLSTM_R, w), lambda b, j: (r(b, j), c))
        return [
            col(LANES, PA_Q // LANES),
            pl.BlockSpec((nc, LANES, A_CHUNK), lambda b, j: (r(b, j), 0, 0)),
            col(2 * LANES, PA_V // (2 * LANES)),
            col(LANES, PA_G // LANES + d), col(LANES, PA_G // LANES + d + 1),
            pl.BlockSpec((nc, 2 * A_HEADS, A_CHUNK), lambda b, j: (r(b, j), d // 2, 0)),
            pl.BlockSpec((nc, 2 * A_HEADS, A_CHUNK), lambda b, j: (r(b, j), 2 + d // 2, 0)),
        ]

    out_spec = lambda rev: pl.BlockSpec((MLSTM_R, A_HEADS * A_DV), lambda b, j: (rb(rev)(b, j), 0))
    return pl.pallas_call(
        _mlstm_kernel,
        grid=(B, nb + 1),
        in_specs=stream_specs(False) + stream_specs(True),
        out_specs=[out_spec(False), out_spec(True)],
        out_shape=[jax.ShapeDtypeStruct((N, A_HEADS * A_DV), f32)] * 2,
        scratch_shapes=[pltpu.VMEM((2, LANES, 3 * LANES), f32), pltpu.VMEM((2, 8, LANES), f32),
                        pltpu.VMEM((2, 8, LANES), f32)],
        compiler_params=_params(2),
        name="mlstm",
    )(*[pa, kt, pa, pa, pa, gt, gt] * 2)


def _tile_max(s, m128):
    for t in range(s.shape[1] // LANES):
        blk = s[:, LANES * t:LANES * (t + 1)]
        m128 = blk if m128 is None else jnp.maximum(m128, blk)
    return m128


def _mla_kernel(*refs, latent):
    if latent:
        q_ref, kc_ref, vc_ref, kl_ref, vl_ref, o_ref, s_ref = refs
        sources = [(kc_ref, vc_ref, 0, CTX)] + [(kl_ref, vl_ref, c, MLA_KC) for c in range(0, T, MLA_KC)]
    else:
        q_ref, kc_ref, vc_ref, o_ref, s_ref = refs
        sources = [(kc_ref, vc_ref, 0, CTX)]
    q = q_ref[...]
    lane = lax.broadcasted_iota(jnp.int32, (1, LANES), 1)
    row_max = []
    for hh in range(2):
        sl = slice(LANES * hh, LANES * (hh + 1))
        qh = q[:, sl]
        m128 = None
        off = 0
        for k_ref, _, r0, n in sources:
            s = _dot_nt(qh, k_ref[r0:r0 + n, sl])
            s_ref[hh, :, off:off + n] = s
            m128 = _tile_max(s, m128)
            off += n
        row_max.append(jnp.max(m128, axis=-1, keepdims=True))
    outs = []
    for hh in range(2):
        den_lane = B_DV if hh == 0 else 0
        acc = None
        off = 0
        for _, v_ref, r0, n in sources:
            p = jnp.exp2(s_ref[hh, :, off:off + n] - row_max[hh]).astype(bf16)
            vext = jnp.where(lane == den_lane, 1.0, v_ref[r0:r0 + n, :]).astype(bf16)
            part = _dot(p, vext)
            acc = part if acc is None else acc + part
            off += n
        outs.append(acc / acc[:, den_lane:den_lane + 1])
    o_ref[...] = jnp.where(lane < B_DV, outs[0], outs[1]).astype(o_ref.dtype)


def _mla(qb, kb, vb, latent):
    npair = B_HEADS // 2
    ctx_blk = NL // CTX
    kv_specs = [
        pl.BlockSpec((CTX, 2 * LANES), lambda b, p, i: (ctx_blk + b, p)),
        pl.BlockSpec((CTX, LANES), lambda b, p, i: (ctx_blk + b, p)),
    ]
    if latent:
        tq = MLA_TQ
        nq = T // tq
        qmap = omap = lambda b, p, i: (b * nq + i, p)
        kv_specs += [
            pl.BlockSpec((T, 2 * LANES), lambda b, p, i: (b, p)),
            pl.BlockSpec((T, LANES), lambda b, p, i: (b, p)),
        ]
        args = (qb, kb, vb, kb, vb)
        nkeys = CTX + T
    else:
        tq = CTX
        nq = 1
        qmap = lambda b, p, i: (ctx_blk + b, p)
        omap = lambda b, p, i: (b, p)
        args = (qb, kb, vb)
        nkeys = CTX
    return pl.pallas_call(
        functools.partial(_mla_kernel, latent=latent),
        grid=(B, npair, nq),
        in_specs=[pl.BlockSpec((tq, 2 * LANES), qmap)] + kv_specs,
        out_specs=pl.BlockSpec((tq, LANES), omap),
        out_shape=jax.ShapeDtypeStruct((NL if latent else NCX, B_HEADS * B_DV), bf16),
        scratch_shapes=[pltpu.VMEM((2, tq, nkeys), f32)],
        compiler_params=_params(3),
        name="mla_latent" if latent else "mla_context",
    )(*args)


def _gqa_kernel(sink_ref, *refs, latent):
    if latent:
        q_ref, kc_ref, vc_ref, kl_ref, vl_ref, o_ref = refs
    else:
        q_ref, kc_ref, vc_ref, o_ref = refs
    q = q_ref[...]
    tq = q.shape[0]
    lane = lax.broadcasted_iota(jnp.int32, (1, LANES), 1)
    keys = kc_ref[...]
    vals = vc_ref[...]
    valid = None
    if latent:
        n = pl.program_id(1)
        start = pl.multiple_of(jnp.clip(n * GQA_TQ - WINDOW, 0, T - GQA_BAND), WINDOW)
        keys = jnp.concatenate([keys, kl_ref[pl.ds(start, GQA_BAND), :]], axis=0)
        vals = jnp.concatenate([vals, vl_ref[pl.ds(start, GQA_BAND), :]], axis=0)
        qpos = n * GQA_TQ + lax.broadcasted_iota(jnp.int32, (tq, 1), 0)
        kidx = lax.broadcasted_iota(jnp.int32, (1, CTX + GQA_BAND), 1)
        valid = (kidx < CTX) | (jnp.abs(qpos - (start - CTX + kidx)) <= WINDOW)
    lo = lane < C_DH
    outs = [[], []]
    for kvh in range(C_KV_HEADS):
        mine = lo if kvh == 0 else ~lo
        qs = jnp.concatenate([jnp.where(mine, q[:, LANES * g:LANES * (g + 1)], 0) for g in range(C_GROUP)], axis=0)
        s_all = _dot_nt(qs, keys)
        den_lane = C_DH * (1 - kvh)
        vext = jnp.where(lane == den_lane, 1.0, vals).astype(bf16)
        for g in range(C_GROUP):
            s = s_all[g * tq:(g + 1) * tq, :]
            if latent:
                s = jnp.where(valid, s, -jnp.inf)
            sink = sink_ref[C_GROUP * kvh + g] * LOG2E
            m = jnp.maximum(sink, jnp.max(_tile_max(s, None), axis=-1, keepdims=True))
            acc = _dot(jnp.exp2(s - m).astype(bf16), vext)
            outs[kvh].append(acc / (jnp.exp2(sink - m) + acc[:, den_lane:den_lane + 1]))
    for g in range(C_GROUP):
        o_ref[:, LANES * g:LANES * (g + 1)] = jnp.where(lo, outs[0][g], outs[1][g]).astype(o_ref.dtype)


def _gqa(sink, qc, kc, vc, latent):
    ctx_blk = NL // CTX
    kv_specs = [
        pl.BlockSpec((CTX, LANES), lambda b, i: (ctx_blk + b, 0)),
        pl.BlockSpec((CTX, LANES), lambda b, i: (ctx_blk + b, 0)),
    ]
    if latent:
        tq = GQA_TQ
        nq = T // tq
        qmap = omap = lambda b, i: (b * nq + i, 0)
        kv_specs += [
            pl.BlockSpec((T, LANES), lambda b, i: (b, 0)),
            pl.BlockSpec((T, LANES), lambda b, i: (b, 0)),
        ]
        args = (sink, qc, kc, vc, kc, vc)
    else:
        tq = CTX
        nq = 1
        qmap = lambda b, i: (ctx_blk + b, 0)
        omap = lambda b, i: (b, 0)
        args = (sink, qc, kc, vc)
    return pl.pallas_call(
        functools.partial(_gqa_kernel, latent=latent),
        grid=(B, nq),
        in_specs=[pl.BlockSpec(memory_space=pltpu.SMEM), pl.BlockSpec((tq, C_HEADS * C_DH), qmap)] + kv_specs,
        out_specs=pl.BlockSpec((tq, C_HEADS * C_DH), omap),
        out_shape=jax.ShapeDtypeStruct((NL if latent else NCX, C_HEADS * C_DH), bf16),
        compiler_params=_params(2),
        name="gqa_latent" if latent else "gqa_context",
    )(*args)


def _pad_cols(w, width):
    return jnp.pad(w, ((0, 0), (0, width - w.shape[1])))


def _lane_runs(lane_map):
    runs = []
    for src in lane_map:
        src = int(src)
        if runs and ((src < 0 and runs[-1][0] < 0) or (src >= 0 and runs[-1][0] >= 0 and src == sum(runs[-1]))):
            runs[-1] = (runs[-1][0], runs[-1][1] + 1)
        else:
            runs.append((src, 1))
    return runs


def _place(w, lane_map):
    parts = [jnp.zeros(w.shape[:-1] + (n,), w.dtype) if s < 0 else w[..., s:s + n] for s, n in _lane_runs(lane_map)]
    return jnp.concatenate(parts, axis=-1)


def _place_heads(w, heads, lane_map):
    r = w.shape[0]
    return _place(w.reshape(r, heads, -1), lane_map).reshape(r, heads * LANES)


def _reorder_c_heads(w, axis):
    heads = jnp.split(w, C_HEADS, axis=axis)
    return jnp.concatenate([heads[h] for h in C_HEAD_ORDER], axis=axis)


def _arrange_w_in(w, map_b):
    o = np.cumsum((0, 128, 128, 256, 256, 16, 256, 128, 32, 384, 128, 128))
    part = lambda i: w[:, int(o[i]):int(o[i + 1])]
    kr = _place(part(7), np.where(map_b >= B_NOPE, map_b - B_NOPE, -1))
    return jnp.concatenate([
        part(0), part(2), part(3), _pad_cols(part(4), LANES),
        part(5), part(6), kr,
        _reorder_c_heads(part(8), 1), part(9), part(10),
    ], axis=1)


def _rope_tables(map_b, map_c):
    assert T == GRID_W * GRID_W
    pos = jnp.arange(GRID_W, dtype=f32)[:, None]
    small, by_col = [], []
    for lane_map, rope_start, half in ((map_b, B_NOPE, B_ROPE // 4), (map_c, 0, C_DH // 4)):
        rel = lane_map - rope_start
        in_rope = (lane_map >= 0) & (rel >= 0) & (rel < 4 * half)
        rel = np.where(in_rope, rel, 0)
        second = jnp.asarray((rel // half) % 2 == 1)[None, :]
        freq = ROPE_BASE ** (-jnp.asarray(rel % half, f32) / half)
        rot = jnp.asarray(in_rope)[None, :]
        ang = pos * freq[None, :]
        sin = jnp.sin(ang)
        small += [jnp.where(rot, jnp.cos(ang), 1.0), jnp.where(rot, jnp.where(second, sin, -sin), 0.0)]
        by_col += [rel >= 2 * half] * 2
    small = jnp.concatenate(small, axis=1)
    by_col = jnp.asarray(np.concatenate(by_col))[None, None, :]
    shape = (GRID_W, GRID_W, small.shape[1])
    full = jnp.where(by_col, jnp.broadcast_to(small[None], shape), jnp.broadcast_to(small[:, None], shape))
    return full.reshape(T, small.shape[1])


def kernel(x, c, ctx, c_ctx, ada_w, ada_b, norm_g, ffn1_wi, ffn1_wo, ffn2_wi, ffn2_wo, w_in, w_out,
           mlstm_gate_b, mlstm_out_norm, mla_cq_norm, mla_ckv_norm, mla_w_uq, mla_w_ukv, mla_q_norm, mla_k_norm,
           gqa_q_norm, gqa_k_norm, gqa_sink):
    map_b, map_c = _head_lane_map_b(), _head_lane_map_c()
    nope_map = np.where(map_b < B_NOPE, map_b, -1)
    q_scale_b, q_scale_c = B_DQK ** -0.5 * LOG2E, C_DH ** -0.5 * LOG2E

    def both_gate_orders(g):
        grp = [g[..., A_HEADS * i:A_HEADS * (i + 1)] for i in range(4)]
        return jnp.concatenate([g, grp[1], grp[0], grp[3], grp[2]], axis=-1)

    def arrange(w_in_l, gate_b_l, w_uq_l, w_ukv_l, bq_l, bk_l, cq_l, ck_l):
        ukv = w_ukv_l.reshape(B_KV_RANK, B_HEADS, B_NOPE + B_DV)
        gbq, gbk = _place(bq_l[None], map_b) * q_scale_b, _place(bk_l[None], map_b)
        gcq, gck = _place(cq_l[None], map_c) * q_scale_c, _place(ck_l[None], map_c)
        return dict(
            win=_arrange_w_in(w_in_l, map_b).astype(bf16),
            bias=_pad_cols(jnp.pad(gate_b_l[None], ((0, 0), (COL_AG, 0))), WP),
            wkg=jnp.concatenate([w_in_l[:, 128:256], both_gate_orders(w_in_l[:, 768:768 + N_GATES])],
                                axis=1).T.astype(bf16),
            gb=both_gate_orders(gate_b_l)[:, None],
            wuq=_place_heads(w_uq_l, B_HEADS, map_b).astype(bf16),
            wukv=jnp.concatenate([_place_heads(ukv[:, :, :B_NOPE].reshape(B_KV_RANK, -1), B_HEADS, nope_map),
                                  ukv[:, :, B_NOPE:].reshape(B_KV_RANK, -1)], axis=1).astype(bf16),
            hg=jnp.concatenate([jnp.concatenate(pair, axis=1)
                                for pair in ((gbq, gbq), (gbk, gbk), (gcq, gcq), (gcq, gck))]))

    pw = jax.vmap(arrange)(w_in, mlstm_gate_b, mla_w_uq, mla_w_ukv, mla_q_norm, mla_k_norm, gqa_q_norm, gqa_k_norm)
    wi1, wo1, wi2, wo2 = (w.astype(bf16) for w in (ffn1_wi, ffn1_wo, ffn2_wi, ffn2_wo))
    c_rows = A_HEADS * A_DV + B_HEADS * B_DV
    wout = jnp.concatenate([w_out[:, :c_rows], _reorder_c_heads(w_out[:, c_rows:], 1)], axis=1).astype(bf16)
    cqn, ckvn, onorm = mla_cq_norm[:, None, :], mla_ckv_norm[:, None, :], mlstm_out_norm[:, None, :]

    cc = jnp.concatenate([c, c_ctx[None, :], jnp.zeros((MOD_ROWS - B - 1, D), f32)], axis=0)
    mod = _ada(cc, ada_w, ada_b).reshape(DEPTH, MOD_ROWS, N_MOD, D)
    tab = _rope_tables(map_b, map_c)
    xs = (x.reshape(NL, D), ctx.reshape(NCX, D))

    for l in range(DEPTH):
        need_ctx = l < DEPTH - 1
        x1, h = _ffn1(l, xs, mod, norm_g, wi1, wo1)
        pa, kt, gt, qb, kb, vb, qc, kc, vc = _inproj(
            l, h, tab, pw["win"], pw["bias"], cqn, ckvn, pw["wuq"], pw["wukv"], pw["hg"], pw["wkg"], pw["gb"])
        hf, hb = _mlstm(pa, kt, gt)
        y_lat = (_mla(qb, kb, vb, True), _gqa(gqa_sink[l], qc, kc, vc, True))
        y_ctx = (_mla(qb, kb, vb, False), _gqa(gqa_sink[l], qc, kc, vc, False)) if need_ctx else None
        xs = _outproj_ffn(l, x1, mod, norm_g, hf, hb, pa, y_lat, y_ctx, onorm, wout, wi2, wo2)
    return xs.reshape(B, T, D)
```

```python
import functools
import math

import jax
import jax.numpy as jnp
import numpy as np
from jax import lax
from jax.experimental import pallas as pl
from jax.experimental.pallas import tpu as pltpu

f32 = jnp.float32
bf16 = jnp.bfloat16

D = 1024
B = 4
T = 4096
CTX = 256
DEPTH = 2
GRID_W = 64
ROPE_BASE = 10000.0
EPS = 1e-6
HALF = 0.5
N_MOD = 9
D_FF = 2816
A_HEADS, A_DK, A_DV, A_CHUNK = 4, 32, 64, 64
B_HEADS, B_Q_RANK, B_KV_RANK, B_NOPE, B_ROPE, B_DV = 6, 256, 128, 64, 32, 64
B_DQK = B_NOPE + B_ROPE
C_HEADS, C_KV_HEADS, C_DH, WINDOW = 6, 2, 64, 128
C_GROUP = C_HEADS // C_KV_HEADS

NL = B * T
NCX = B * CTX
N = NL + NCX

LANES = 128
HALF_LANES = LANES // 2
MOD_ROWS = 8
VMEM_LIMIT = 56 * 1024 * 1024
LOG2E = math.log2(math.e)

TM_FFN = 512
FF_CHUNK = 256
TM_IN = 512
ADA_TN = 1152
MLSTM_R = 256
MLA_TQ = 1024
MLA_KC = 512
GQA_TQ = 256
GQA_BAND = GQA_TQ + 2 * WINDOW

COL_AQ, COL_AV, COL_AO, COL_AG = 0, 128, 384, 640
COL_BCQ, COL_BCKV, COL_BKR = 768, 1024, 1152
COL_CQ, COL_CK, COL_CV = 1280, 1664, 1792
COL_AK = 1920
WP = 2048
C_HEAD_ORDER = tuple(h for g in range(C_GROUP) for h in (g, C_GROUP + g))
PA_V, PA_O, PA_Q, PA_G = 0, 256, 512, 640
PA_W = PA_G + 4 * LANES
N_GATES = 4 * A_HEADS


def _head_lane_map_b():
    m = -np.ones(LANES, np.int64)
    m[0:8], m[8:16], m[16:64] = np.arange(64, 72), np.arange(80, 88), np.arange(0, 48)
    m[64:72], m[72:80], m[80:96] = np.arange(72, 80), np.arange(88, 96), np.arange(48, 64)
    return m


def _head_lane_map_c():
    return np.concatenate([np.arange(C_DH), np.arange(C_DH)])


def _sigmoid(x):
    return 1.0 / (1.0 + jnp.exp(-x))


def _log_sigmoid(x):
    return jnp.minimum(x, 0.0) - jnp.log(1.0 + jnp.exp(-jnp.abs(x)))


def _rms(x, g):
    ms = jnp.mean(x * x, axis=-1, keepdims=True)
    return x * lax.rsqrt(ms + EPS) * g


def _dot(a, b):
    return jnp.dot(a, b, preferred_element_type=f32)


def _dot_nt(a, b):
    return lax.dot_general(a, b, (((1,), (1,)), ((), ())), preferred_element_type=f32)


def _split3(x):
    hi = x.astype(bf16)
    r1 = x - hi.astype(f32)
    mid = r1.astype(bf16)
    return hi, mid, (r1 - mid.astype(f32)).astype(bf16)


def _layer(arr, l):
    rest = (0,) * (arr.ndim - 1)
    return pl.BlockSpec((None,) + arr.shape[1:], lambda *_: (l,) + rest, pipeline_mode=pl.Buffered(1))


def _mod_spec(l, tm):
    tpb = T // tm
    return pl.BlockSpec((None, 1, N_MOD, D), lambda i: (l, i // tpb, 0, 0))


def _params(n_axes):
    return pltpu.CompilerParams(dimension_semantics=("arbitrary",) * n_axes, vmem_limit_bytes=VMEM_LIMIT)


def _ada_kernel(c_ref, w_ref, b_ref, o_ref):
    c = c_ref[...]
    s = c * _sigmoid(c)
    pieces = _split3(s)
    s3 = jnp.concatenate([piece.astype(f32) for piece in pieces], axis=0).astype(bf16)
    w = w_ref[0]
    w_hi = w.astype(bf16)
    w_lo = (w - w_hi.astype(f32)).astype(bf16)
    r = _dot(s3, w_hi)
    o_ref[0] = (r[0:MOD_ROWS] + r[MOD_ROWS:2 * MOD_ROWS] + r[2 * MOD_ROWS:]
                + _dot(pieces[0], w_lo) + b_ref[0])


def _ada(cc, ada_w, ada_b):
    nt = (N_MOD * D) // ADA_TN
    return pl.pallas_call(
        _ada_kernel,
        grid=(DEPTH, nt),
        in_specs=[
            pl.BlockSpec((MOD_ROWS, D), lambda l, j: (0, 0)),
            pl.BlockSpec((1, D, ADA_TN), lambda l, j: (l, 0, j)),
            pl.BlockSpec((1, 1, ADA_TN), lambda l, j: (l, 0, j)),
        ],
        out_specs=pl.BlockSpec((1, MOD_ROWS, ADA_TN), lambda l, j: (l, 0, j)),
        out_shape=jax.ShapeDtypeStruct((DEPTH, MOD_ROWS, N_MOD * D), f32),
        compiler_params=_params(2),
        name="ada_mod",
    )(cc, ada_w, ada_b.reshape(DEPTH, 1, N_MOD * D))


def _ffn(x, g, shift, scale, gate, wi_ref, wo_ref):
    h = (_rms(x, g) * (1.0 + scale) + shift).astype(bf16)
    acc = None
    for c in range(D_FF // FF_CHUNK):
        lo, hi = c * FF_CHUNK, (c + 1) * FF_CHUNK
        gt = _dot(h, wi_ref[:, lo:hi])
        up = _dot(h, wi_ref[:, D_FF + lo:D_FF + hi])
        a = (gt * _sigmoid(gt) * up).astype(bf16)
        part = _dot(a, wo_ref[lo:hi, :])
        acc = part if acc is None else acc + part
    return x + HALF * gate * acc


def _ffn1_kernel(*refs, split_input):
    if split_input:
        xl_ref, xc_ref, mod_ref, ng_ref, wi_ref, wo_ref, x1_ref, h_ref = refs
        x = jnp.where(pl.program_id(0) < NL // TM_FFN, xl_ref[...], xc_ref[...])
    else:
        x_ref, mod_ref, ng_ref, wi_ref, wo_ref, x1_ref, h_ref = refs
        x = x_ref[...]
    mod = mod_ref[0]
    x1 = _ffn(x, ng_ref[0:1, :], mod[0:1, :], mod[1:2, :], mod[2:3, :], wi_ref, wo_ref)
    x1_ref[...] = x1
    h_ref[...] = (_rms(x1, ng_ref[1:2, :]) * (1.0 + mod[4:5, :]) + mod[3:4, :]).astype(bf16)


def _ffn1(l, xs, mod, ng, wi, wo):
    split_input = isinstance(xs, tuple)
    nlt = NL // TM_FFN
    if split_input:
        assert NCX % TM_FFN == 0
        x_specs = [pl.BlockSpec((TM_FFN, D), lambda i: (jnp.minimum(i, nlt - 1), 0)),
                   pl.BlockSpec((TM_FFN, D), lambda i: (jnp.maximum(i - nlt, 0), 0))]
    else:
        xs = (xs,)
        x_specs = [pl.BlockSpec((TM_FFN, D), lambda i: (i, 0))]
    return pl.pallas_call(
        functools.partial(_ffn1_kernel, split_input=split_input),
        grid=(N // TM_FFN,),
        in_specs=x_specs + [_mod_spec(l, TM_FFN), _layer(ng, l), _layer(wi, l), _layer(wo, l)],
        out_specs=[pl.BlockSpec((TM_FFN, D), lambda i: (i, 0))] * 2,
        out_shape=[jax.ShapeDtypeStruct((N, D), f32), jax.ShapeDtypeStruct((N, D), bf16)],
        compiler_params=_params(1),
        name="ffn1",
    )(*xs, mod, ng, wi, wo)


def _outproj_ffn_kernel(*refs, with_ctx):
    if with_ctx:
        (x_ref, mod_ref, ng_ref, hf_ref, hb_ref, o_ref, ybl_ref, ycl_ref, ybc_ref, ycc_ref, on_ref, wout_ref,
         wi_ref, wo_ref, out_ref) = refs
        is_latent = pl.program_id(0) < NL // TM_FFN
        yb = jnp.where(is_latent, ybl_ref[...], ybc_ref[...])
        yc = jnp.where(is_latent, ycl_ref[...], ycc_ref[...])
    else:
        (x_ref, mod_ref, ng_ref, hf_ref, hb_ref, o_ref, yb_ref, yc_ref, on_ref, wout_ref,
         wi_ref, wo_ref, out_ref) = refs
        yb, yc = yb_ref[...], yc_ref[...]
    mod = mod_ref[0]
    hs = hf_ref[...] + hb_ref[...]
    sq = hs * hs
    head = lax.broadcasted_iota(jnp.int32, (1, A_HEADS * A_DV), 1) // A_DV
    ms = jnp.zeros_like(hs)
    for hh in range(A_HEADS):
        sel = head == hh
        ssh = jnp.sum(jnp.where(sel, sq, 0.0), axis=-1, keepdims=True) * (1.0 / A_DV)
        ms = jnp.where(sel, ssh, ms)
    ya = _sigmoid(o_ref[...]) * (hs * lax.rsqrt(ms + EPS) * on_ref[...])
    y = jnp.concatenate([ya.astype(bf16), yb, yc], axis=-1)
    x2 = x_ref[...] + mod[5:6, :] * _dot(y, wout_ref[...])
    out_ref[...] = _ffn(x2, ng_ref[2:3, :], mod[6:7, :], mod[7:8, :], mod[8:9, :], wi_ref, wo_ref)


def _outproj_ffn(l, x1, mod, ng, hf, hb, pa, y_lat, y_ctx, onorm, wout, wi, wo):
    with_ctx = y_ctx is not None
    rows = N if with_ctx else NL
    nlt = NL // TM_FFN
    row = lambda w, c=0: pl.BlockSpec((TM_FFN, w), lambda i: (i, c))
    lat = lambda w: pl.BlockSpec((TM_FFN, w), lambda i: (jnp.minimum(i, nlt - 1), 0))
    ctx = lambda w: pl.BlockSpec((TM_FFN, w), lambda i: (jnp.maximum(i - nlt, 0), 0))
    y_specs = [lat(B_HEADS * B_DV), lat(C_HEADS * C_DH)]
    if with_ctx:
        y_specs += [ctx(B_HEADS * B_DV), ctx(C_HEADS * C_DH)]
    return pl.pallas_call(
        functools.partial(_outproj_ffn_kernel, with_ctx=with_ctx),
        grid=(rows // TM_FFN,),
        in_specs=[
            row(D), _mod_spec(l, TM_FFN), _layer(ng, l),
            row(A_HEADS * A_DV), row(A_HEADS * A_DV), row(A_HEADS * A_DV, PA_O // (A_HEADS * A_DV)),
        ] + y_specs + [_layer(onorm, l), _layer(wout, l), _layer(wi, l), _layer(wo, l)],
        out_specs=row(D),
        out_shape=jax.ShapeDtypeStruct((rows, D), f32),
        compiler_params=_params(1),
        name="outproj_ffn2",
    )(x1, mod, ng, hf, hb, pa, *y_lat, *(y_ctx or ()), onorm, wout, wi, wo)


def _inproj_kernel(h_ref, win_ref, bias_ref, tab_ref, cqn_ref, ckvn_ref, wuq_ref, wukv_ref, hg_ref,
                   pa_ref, kt_ref, gt_ref, qb_ref, kb_ref, vb_ref, qc_ref, kc_ref, vc_ref):
    h = h_ref[...]
    p = _dot(h, win_ref[...]) + bias_ref[...]

    pa_ref[:, PA_V:PA_Q] = p[:, COL_AV:COL_AG]
    pa_ref[:, PA_Q:PA_G] = p[:, COL_AQ:COL_AV] * (A_DK ** -0.5)
    graw = p[:, COL_AG:COL_AG + LANES]
    lane = lax.broadcasted_iota(jnp.int32, (1, LANES), 1)
    for kk in range(4):
        gk = graw if kk == 0 else pltpu.roll(graw, LANES - A_HEADS * kk, 1)
        if kk % 2 == 1:
            gk = _log_sigmoid(gk)
        pa_ref[:, PA_G + LANES * kk:PA_G + LANES * (kk + 1)] = jnp.where(lane < A_HEADS, gk, 0.0)

    grow = lax.broadcasted_iota(jnp.int32, (2 * N_GATES, 1), 0)
    is_forget = ((grow // A_HEADS) % 2 == 1) == (grow < N_GATES)
    gate = grow % N_GATES
    source_lane = jnp.where(grow < N_GATES, gate, ((gate // A_HEADS) ^ 1) * A_HEADS + gate % A_HEADS)
    pick_gates = (lane == source_lane).astype(bf16)
    eye = (lax.broadcasted_iota(jnp.int32, (LANES, LANES), 0)
           == lax.broadcasted_iota(jnp.int32, (LANES, LANES), 1)).astype(bf16)
    keys = p[:, COL_AK:COL_AK + LANES].astype(bf16)
    gate_pieces = _split3(graw)
    for c in range(TM_IN // A_CHUNK):
        rows = slice(c * A_CHUNK, (c + 1) * A_CHUNK)
        kt_ref[c] = _dot_nt(eye, keys[rows, :])
        g = sum(_dot_nt(pick_gates, piece[rows, :]) for piece in gate_pieces)
        gt_ref[c] = jnp.where(is_forget, _log_sigmoid(g), g)

    is_latent = pl.program_id(0) < NL // TM_IN
    tab = tab_ref[...]
    tb = [jnp.where(is_latent, tab[:, LANES * i:LANES * (i + 1)], 1.0 - (i % 2)) for i in range(4)]
    hg = hg_ref[...]

    cq = _rms(p[:, COL_BCQ:COL_BCKV], cqn_ref[...]).astype(bf16)
    ckv = _rms(p[:, COL_BCKV:COL_BKR], ckvn_ref[...]).astype(bf16)
    kr = p[:, COL_BKR:COL_CQ]
    q = _dot(cq, wuq_ref[...])
    kv = _dot(ckv, wukv_ref[...])
    vb_ref[...] = kv[:, B_HEADS * LANES:].astype(bf16)
    vc_ref[...] = p[:, COL_CV:COL_CV + LANES].astype(bf16)

    two = lambda a: jnp.concatenate([a, a], axis=1)
    pair = lambda a, j, first=0: a[:, first + 2 * LANES * j:first + 2 * LANES * (j + 1)]
    cos_b, sin_b, cos_c, sin_c = (two(t) for t in tb)
    kr2 = two(kr)
    ri = lax.broadcasted_iota(jnp.int32, (2 * LANES, 2 * LANES), 0)
    ci = lax.broadcasted_iota(jnp.int32, (2 * LANES, 2 * LANES), 1)
    swap_b = ((ri // LANES == ci // LANES) & (ri % LANES == (ci + HALF_LANES) % LANES)).astype(bf16)
    half_c = C_DH // 2
    swap_c = ((ri // half_c == ci // half_c) & (ri % half_c == (ci + half_c // 2) % half_c)).astype(bf16)
    full = slice(0, 2 * LANES)
    jobs = []
    for j in range(B_HEADS // 2):
        dst = slice(2 * LANES * j, 2 * LANES * (j + 1))
        jobs.append((pair(q, j), hg[0:1, :], LANES, B_DQK, cos_b, sin_b, swap_b, [(qb_ref, dst, full)]))
        jobs.append((pair(kv, j) + kr2, hg[1:2, :], LANES, B_DQK, cos_b, sin_b, swap_b, [(kb_ref, dst, full)]))
    jobs.append((pair(p, 0, COL_CQ), hg[2:3, :], C_DH, C_DH, cos_c, sin_c, swap_c, [(qc_ref, full, full)]))
    jobs.append((pair(p, 1, COL_CQ), hg[3:4, :], C_DH, C_DH, cos_c, sin_c, swap_c,
                 [(qc_ref, slice(2 * LANES, 3 * LANES), slice(0, LANES)),
                  (kc_ref, slice(0, LANES), slice(LANES, 2 * LANES))]))
    gained = [x * gain for x, gain, *_ in jobs]
    rolled = [_dot(xg.astype(bf16), job[6]) for job, xg in zip(jobs, gained)]
    lane = lax.broadcasted_iota(jnp.int32, (1, LANES), 1)
    lane2 = lax.broadcasted_iota(jnp.int32, (1, 2 * LANES), 1)
    sums = []
    for x, _, width, *_ in jobs:
        sq = x * x
        parts = []
        for t in range(2):
            blk = sq[:, LANES * t:LANES * (t + 1)]
            if width == LANES:
                parts.append(jnp.sum(blk, axis=-1, keepdims=True))
            else:
                parts.append(jnp.sum(jnp.where(lane < width, blk, 0.0), axis=-1, keepdims=True))
                parts.append(jnp.sum(jnp.where(lane < width, 0.0, blk), axis=-1, keepdims=True))
        ss = parts[-1]
        for k in range(len(parts) - 2, -1, -1):
            ss = jnp.where(lane2 < (k + 1) * width, parts[k], ss)
        sums.append(ss)
    for (_, _, _, n_real, cos, sin, _, dests), xg, xr, ss in zip(jobs, gained, rolled, sums):
        out = ((xg * cos + xr * sin) * lax.rsqrt(ss * (1.0 / n_real) + EPS)).astype(bf16)
        for dst_ref, dst_cols, src_cols in dests:
            dst_ref[:, dst_cols] = out[:, src_cols]


def _inproj(l, h, tab, win, bias, cqn, ckvn, wuq, wukv, hg):
    tpb = T // TM_IN
    row = lambda w: pl.BlockSpec((TM_IN, w), lambda i: (i, 0))
    chunked = lambda r: pl.BlockSpec((TM_IN // A_CHUNK, r, A_CHUNK), lambda i: (i, 0, 0))
    out_w = [(B_HEADS * LANES, bf16), (B_HEADS * LANES, bf16), (B_HEADS * B_DV, bf16),
             (C_HEADS * C_DH, bf16), (C_KV_HEADS * C_DH, bf16), (C_KV_HEADS * C_DH, bf16)]
    return pl.pallas_call(
        _inproj_kernel,
        grid=(N // TM_IN,),
        in_specs=[
            row(D), _layer(win, l), _layer(bias, l),
            pl.BlockSpec((TM_IN, 4 * LANES), lambda i: (jnp.where(i < NL // TM_IN, i % tpb, 0), 0)),
            _layer(cqn, l), _layer(ckvn, l), _layer(wuq, l), _layer(wukv, l),
            _layer(hg, l),
        ],
        out_specs=[row(PA_W), chunked(LANES), chunked(2 * N_GATES)] + [row(w) for w, _ in out_w],
        out_shape=[jax.ShapeDtypeStruct((N, PA_W), f32),
                   jax.ShapeDtypeStruct((N // A_CHUNK, LANES, A_CHUNK), f32),
                   jax.ShapeDtypeStruct((N // A_CHUNK, 2 * N_GATES, A_CHUNK), f32)]
        + [jax.ShapeDtypeStruct((N, w), dt) for w, dt in out_w],
        compiler_params=_params(1),
        name="inproj",
    )(h, win, bias, tab, cqn, ckvn, wuq, wukv, hg)


def _cummax_rows(x, rev):
    n = x.shape[0]
    row = lax.broadcasted_iota(jnp.int32, (n, 1), 0)
    sh = 1
    while sh < n:
        if rev:
            x = jnp.maximum(x, jnp.where(row < n - sh, pltpu.roll(x, n - sh, 0), -jnp.inf))
        else:
            x = jnp.maximum(x, jnp.where(row >= sh, pltpu.roll(x, sh, 0), -jnp.inf))
        sh *= 2
    return x


def _stack_heads(pieces):
    return jnp.concatenate(pieces, axis=0)


def _mlstm_kernel(*refs):
    streams = [(refs[0:7] + refs[14:15], False), (refs[7:14] + refs[15:16], True)]
    s_ref, ml_ref, ms_ref = refs[16:19]

    @pl.when(pl.program_id(1) == 0)
    def _():
        s_ref[...] = jnp.zeros_like(s_ref)
        ml_ref[...] = jnp.zeros_like(ml_ref)
        ms_ref[...] = jnp.zeros_like(ms_ref)

    L = A_CHUNK
    n_chunks = MLSTM_R // L
    heads = range(A_HEADS)
    ti = lax.broadcasted_iota(jnp.int32, (L, L), 0)
    si = lax.broadcasted_iota(jnp.int32, (L, L), 1)
    lane = lax.broadcasted_iota(jnp.int32, (1, LANES), 1)
    row8 = lax.broadcasted_iota(jnp.int32, (2 * A_HEADS, 1), 0)
    in_head = [(lane >= hh * A_DK) & (lane < (hh + 1) * A_DK) for hh in heads]
    ones_blk = jnp.ones((L, LANES), bf16)

    items = []
    for sidx, (srefs, rev) in enumerate(streams):
        q_ref, kt_ref, v_ref, ig_ref, lf_ref, gt_ref, gts_ref, h_ref = srefs
        attend = (si >= ti) if rev else (si <= ti)
        attend4 = _stack_heads([attend] * A_HEADS)
        cum_cols = attend.astype(bf16)
        cum_rows = ((ti >= si) if rev else (ti <= si)).astype(bf16)
        m_lane = ml_ref[sidx, 0:1, :]
        m_sub = ms_ref[sidx, :, 0:1]
        for cc in (range(n_chunks - 1, -1, -1) if rev else range(n_chunks)):
            rows = slice(cc * L, (cc + 1) * L)
            ig = ig_ref[rows, :]
            lf = lf_ref[rows, :]
            gt = gt_ref[cc]
            gts = gts_ref[cc]
            b_col = sum(_dot(cum_cols, piece) for piece in _split3(lf))
            r_col = ig - b_col
            big_m = jnp.maximum(m_lane, _cummax_rows(r_col, rev))
            mt_col = b_col + big_m
            b_last_l = jnp.sum(lf, axis=0, keepdims=True)
            m_new_l = jnp.maximum(m_lane, jnp.max(r_col, axis=0, keepdims=True)) + b_last_l
            b_rows = sum(_dot(piece, cum_rows) for piece in _split3(gts))
            live = row8 < A_HEADS
            r8 = jnp.where(live, gt - b_rows, 0.0)
            b_last_s = jnp.where(live, jnp.sum(gts, axis=-1, keepdims=True), 0.0)
            r_max_s = jnp.max(r8, axis=-1, keepdims=True)
            wg8 = jnp.exp(r8 - r_max_s)
            m_new_s = jnp.maximum(m_sub, r_max_s) + b_last_s
            decay_s = jnp.exp(b_last_s + m_sub - m_new_s)
            scale_s = jnp.exp(b_last_s + r_max_s - m_new_s)
            expand = lambda a, n: _stack_heads([jnp.broadcast_to(a[hh:hh + 1, :], (n, a.shape[1])) for hh in heads])
            q = q_ref[rows, :]
            big_m_b = _stack_heads([jnp.broadcast_to(big_m[:, hh:hh + 1], (L, LANES)) for hh in heads])
            mt_b = _stack_heads([jnp.broadcast_to(mt_col[:, hh:hh + 1], (L, LANES)) for hh in heads])
            m_old_b = _stack_heads([jnp.broadcast_to(m_lane[:, hh:hh + 1], (L, LANES)) for hh in heads])
            items.append(dict(
                sidx=sidx, rows=rows, h_ref=h_ref,
                qst=_stack_heads([jnp.where(in_head[hh], q, 0.0) for hh in heads]).astype(bf16),
                kt=kt_ref[cc].astype(bf16),
                kw=(kt_ref[cc] * expand(wg8, A_DK)).astype(bf16),
                vo=jnp.concatenate([v_ref[rows, :].astype(bf16), ones_blk], axis=1),
                w=jnp.exp(jnp.where(attend4, expand(r8, L) - big_m_b[:, 0:L], -jnp.inf)),
                a_inter=jnp.exp(m_old_b - big_m_b), floor=jnp.exp(-mt_b),
                decay=jnp.broadcast_to(expand(decay_s, A_DK), (LANES, LANES)),
                kv_scale=jnp.broadcast_to(expand(scale_s, A_DK), (LANES, LANES))))
            m_lane, m_sub = m_new_l, m_new_s
        ml_ref[sidx, 0:1, :] = m_lane
        ms_ref[sidx, :, 0:1] = m_sub

    for it in items:
        it["s"] = _dot(it["qst"], it["kt"])

    for it in items:
        it["kv"] = _dot(it["kw"], it["vo"])

    for it in items:
        it["p"] = (it["s"] * it["w"]).astype(bf16)

    state = [s_ref[0], s_ref[1]]
    tile3 = lambda a: jnp.concatenate([a] * 3, axis=1)
    for it in items:
        st = state[it["sidx"]]
        it["c_in"] = st.astype(bf16)
        state[it["sidx"]] = tile3(it["decay"]) * st + tile3(it["kv_scale"]) * it["kv"]
    s_ref[0] = state[0]
    s_ref[1] = state[1]

    nv = A_HEADS * A_DV
    for it in items:
        out = tile3(it["a_inter"]) * _dot(it["qst"], it["c_in"]) + _dot(it["p"], it["vo"])
        res = out[:, 0:nv] / tile3(jnp.maximum(jnp.abs(out[:, nv:]), it["floor"]))[:, 0:nv]
        for pair in range(A_HEADS // 2):
            sl = slice(LANES * pair, LANES * (pair + 1))
            even = res[L * 2 * pair:L * (2 * pair + 1), sl]
            odd = res[L * (2 * pair + 1):L * (2 * pair + 2), sl]
            it["h_ref"][it["rows"], sl] = jnp.where(lane < A_DV, even, odd)


def _mlstm(pa, kt, gt):
    nb = T // MLSTM_R
    nc = MLSTM_R // A_CHUNK
    assert CTX == MLSTM_R

    def rb(rev):
        def f(b, j):
            jj = j - 1
            return jnp.where(j == 0, NL // MLSTM_R + b, b * nb + (nb - 1 - jj if rev else jj))
        return f

    def stream_specs(rev):
        r = rb(rev)
        d = 2 if rev else 0
        col = lambda w, c: pl.BlockSpec((MLSTM_R, w), lambda b, j: (r(b, j), c))
        return [
            col(LANES, PA_Q // LANES),
            pl.BlockSpec((nc, LANES, A_CHUNK), lambda b, j: (r(b, j), 0, 0)),
            col(2 * LANES, PA_V // (2 * LANES)),
            col(LANES, PA_G // LANES + d), col(LANES, PA_G // LANES + d + 1),
            pl.BlockSpec((nc, 2 * A_HEADS, A_CHUNK), lambda b, j: (r(b, j), d // 2, 0)),
            pl.BlockSpec((nc, 2 * A_HEADS, A_CHUNK), lambda b, j: (r(b, j), 2 + d // 2, 0)),
        ]

    out_spec = lambda rev: pl.BlockSpec((MLSTM_R, A_HEADS * A_DV), lambda b, j: (rb(rev)(b, j), 0))
    return pl.pallas_call(
        _mlstm_kernel,
        grid=(B, nb + 1),
        in_specs=stream_specs(False) + stream_specs(True),
        out_specs=[out_spec(False), out_spec(True)],
        out_shape=[jax.ShapeDtypeStruct((N, A_HEADS * A_DV), f32)] * 2,
        scratch_shapes=[pltpu.VMEM((2, LANES, 3 * LANES), f32), pltpu.VMEM((2, 8, LANES), f32),
                        pltpu.VMEM((2, 8, LANES), f32)],
        compiler_params=_params(2),
        name="mlstm",
    )(*[pa, kt, pa, pa, pa, gt, gt] * 2)


def _tile_max(s, m128):
    for t in range(s.shape[1] // LANES):
        blk = s[:, LANES * t:LANES * (t + 1)]
        m128 = blk if m128 is None else jnp.maximum(m128, blk)
    return m128


def _mla_kernel(*refs, latent):
    if latent:
        q_ref, kc_ref, vc_ref, kl_ref, vl_ref, o_ref, s_ref = refs
        sources = [(kc_ref, vc_ref, 0, CTX)] + [(kl_ref, vl_ref, c, MLA_KC) for c in range(0, T, MLA_KC)]
    else:
        q_ref, kc_ref, vc_ref, o_ref, s_ref = refs
        sources = [(kc_ref, vc_ref, 0, CTX)]
    q = q_ref[...]
    lane = lax.broadcasted_iota(jnp.int32, (1, LANES), 1)
    row_max = []
    for hh in range(2):
        sl = slice(LANES * hh, LANES * (hh + 1))
        qh = q[:, sl]
        m128 = None
        off = 0
        for k_ref, _, r0, n in sources:
            s = _dot_nt(qh, k_ref[r0:r0 + n, sl])
            s_ref[hh, :, off:off + n] = s
            m128 = _tile_max(s, m128)
            off += n
        row_max.append(jnp.max(m128, axis=-1, keepdims=True))
    outs = []
    for hh in range(2):
        den_lane = B_DV if hh == 0 else 0
        acc = None
        off = 0
        for _, v_ref, r0, n in sources:
            p = jnp.exp2(s_ref[hh, :, off:off + n] - row_max[hh]).astype(bf16)
            vext = jnp.where(lane == den_lane, 1.0, v_ref[r0:r0 + n, :]).astype(bf16)
            part = _dot(p, vext)
            acc = part if acc is None else acc + part
            off += n
        outs.append(acc / acc[:, den_lane:den_lane + 1])
    o_ref[...] = jnp.where(lane < B_DV, outs[0], outs[1]).astype(o_ref.dtype)


def _mla(qb, kb, vb, latent):
    npair = B_HEADS // 2
    ctx_blk = NL // CTX
    kv_specs = [
        pl.BlockSpec((CTX, 2 * LANES), lambda b, p, i: (ctx_blk + b, p)),
        pl.BlockSpec((CTX, LANES), lambda b, p, i: (ctx_blk + b, p)),
    ]
    if latent:
        tq = MLA_TQ
        nq = T // tq
        qmap = omap = lambda b, p, i: (b * nq + i, p)
        kv_specs += [
            pl.BlockSpec((T, 2 * LANES), lambda b, p, i: (b, p)),
            pl.BlockSpec((T, LANES), lambda b, p, i: (b, p)),
        ]
        args = (qb, kb, vb, kb, vb)
        nkeys = CTX + T
    else:
        tq = CTX
        nq = 1
        qmap = lambda b, p, i: (ctx_blk + b, p)
        omap = lambda b, p, i: (b, p)
        args = (qb, kb, vb)
        nkeys = CTX
    return pl.pallas_call(
        functools.partial(_mla_kernel, latent=latent),
        grid=(B, npair, nq),
        in_specs=[pl.BlockSpec((tq, 2 * LANES), qmap)] + kv_specs,
        out_specs=pl.BlockSpec((tq, LANES), omap),
        out_shape=jax.ShapeDtypeStruct((NL if latent else NCX, B_HEADS * B_DV), bf16),
        scratch_shapes=[pltpu.VMEM((2, tq, nkeys), f32)],
        compiler_params=_params(3),
        name="mla_latent" if latent else "mla_context",
    )(*args)


def _gqa_kernel(sink_ref, *refs, latent):
    if latent:
        q_ref, kc_ref, vc_ref, kl_ref, vl_ref, o_ref = refs
    else:
        q_ref, kc_ref, vc_ref, o_ref = refs
    q = q_ref[...]
    tq = q.shape[0]
    lane = lax.broadcasted_iota(jnp.int32, (1, LANES), 1)
    keys = kc_ref[...]
    vals = vc_ref[...]
    valid = None
    if latent:
        n = pl.program_id(1)
        start = pl.multiple_of(jnp.clip(n * GQA_TQ - WINDOW, 0, T - GQA_BAND), WINDOW)
        keys = jnp.concatenate([keys, kl_ref[pl.ds(start, GQA_BAND), :]], axis=0)
        vals = jnp.concatenate([vals, vl_ref[pl.ds(start, GQA_BAND), :]], axis=0)
        qpos = n * GQA_TQ + lax.broadcasted_iota(jnp.int32, (tq, 1), 0)
        kidx = lax.broadcasted_iota(jnp.int32, (1, CTX + GQA_BAND), 1)
        valid = (kidx < CTX) | (jnp.abs(qpos - (start - CTX + kidx)) <= WINDOW)
    lo = lane < C_DH
    outs = [[], []]
    for kvh in range(C_KV_HEADS):
        mine = lo if kvh == 0 else ~lo
        qs = jnp.concatenate([jnp.where(mine, q[:, LANES * g:LANES * (g + 1)], 0) for g in range(C_GROUP)], axis=0)
        s_all = _dot_nt(qs, keys)
        den_lane = C_DH * (1 - kvh)
        vext = jnp.where(lane == den_lane, 1.0, vals).astype(bf16)
        for g in range(C_GROUP):
            s = s_all[g * tq:(g + 1) * tq, :]
            if latent:
                s = jnp.where(valid, s, -jnp.inf)
            sink = sink_ref[C_GROUP * kvh + g] * LOG2E
            m = jnp.maximum(sink, jnp.max(_tile_max(s, None), axis=-1, keepdims=True))
            acc = _dot(jnp.exp2(s - m).astype(bf16), vext)
            outs[kvh].append(acc / (jnp.exp2(sink - m) + acc[:, den_lane:den_lane + 1]))
    for g in range(C_GROUP):
        o_ref[:, LANES * g:LANES * (g + 1)] = jnp.where(lo, outs[0][g], outs[1][g]).astype(o_ref.dtype)


def _gqa(sink, qc, kc, vc, latent):
    ctx_blk = NL // CTX
    kv_specs = [
        pl.BlockSpec((CTX, LANES), lambda b, i: (ctx_blk + b, 0)),
        pl.BlockSpec((CTX, LANES), lambda b, i: (ctx_blk + b, 0)),
    ]
    if latent:
        tq = GQA_TQ
        nq = T // tq
        qmap = omap = lambda b, i: (b * nq + i, 0)
        kv_specs += [
            pl.BlockSpec((T, LANES), lambda b, i: (b, 0)),
            pl.BlockSpec((T, LANES), lambda b, i: (b, 0)),
        ]
        args = (sink, qc, kc, vc, kc, vc)
    else:
        tq = CTX
        nq = 1
        qmap = lambda b, i: (ctx_blk + b, 0)
        omap = lambda b, i: (b, 0)
        args = (sink, qc, kc, vc)
    return pl.pallas_call(
        functools.partial(_gqa_kernel, latent=latent),
        grid=(B, nq),
        in_specs=[pl.BlockSpec(memory_space=pltpu.SMEM), pl.BlockSpec((tq, C_HEADS * C_DH), qmap)] + kv_specs,
        out_specs=pl.BlockSpec((tq, C_HEADS * C_DH), omap),
        out_shape=jax.ShapeDtypeStruct((NL if latent else NCX, C_HEADS * C_DH), bf16),
        compiler_params=_params(2),
        name="gqa_latent" if latent else "gqa_context",
    )(*args)


def _pad_cols(w, width):
    return jnp.pad(w, ((0, 0), (0, width - w.shape[1])))


def _lane_runs(lane_map):
    runs = []
    for src in lane_map:
        src = int(src)
        if runs and ((src < 0 and runs[-1][0] < 0) or (src >= 0 and runs[-1][0] >= 0 and src == sum(runs[-1]))):
            runs[-1] = (runs[-1][0], runs[-1][1] + 1)
        else:
            runs.append((src, 1))
    return runs


def _place(w, lane_map):
    parts = [jnp.zeros(w.shape[:-1] + (n,), w.dtype) if s < 0 else w[..., s:s + n] for s, n in _lane_runs(lane_map)]
    return jnp.concatenate(parts, axis=-1)


def _place_heads(w, heads, lane_map):
    r = w.shape[0]
    return _place(w.reshape(r, heads, -1), lane_map).reshape(r, heads * LANES)


def _reorder_c_heads(w, axis):
    heads = jnp.split(w, C_HEADS, axis=axis)
    return jnp.concatenate([heads[h] for h in C_HEAD_ORDER], axis=axis)


IN_SIZES = (A_HEADS * A_DK, A_HEADS * A_DK, A_HEADS * A_DV, A_HEADS * A_DV, N_GATES,
            B_Q_RANK, B_KV_RANK, B_ROPE, C_HEADS * C_DH, C_KV_HEADS * C_DH, C_KV_HEADS * C_DH)
IN_OFFSETS = tuple(int(v) for v in np.cumsum((0,) + IN_SIZES))


def _in_part(w, i):
    return w[..., IN_OFFSETS[i]:IN_OFFSETS[i + 1]]


def _arrange_w_in(w, map_b):
    part = lambda i: _in_part(w, i)
    kr = _place(part(7), np.where(map_b >= B_NOPE, map_b - B_NOPE, -1))
    return jnp.concatenate([
        part(0), part(2), part(3), _pad_cols(part(4), LANES),
        part(5), part(6), kr,
        _reorder_c_heads(part(8), 1), part(9), part(10), part(1),
    ], axis=1)


def _rope_tables(map_b, map_c):
    assert T == GRID_W * GRID_W
    pos = jnp.arange(GRID_W, dtype=f32)[:, None]
    small, by_col = [], []
    for lane_map, rope_start, half in ((map_b, B_NOPE, B_ROPE // 4), (map_c, 0, C_DH // 4)):
        rel = lane_map - rope_start
        in_rope = (lane_map >= 0) & (rel >= 0) & (rel < 4 * half)
        rel = np.where(in_rope, rel, 0)
        second = jnp.asarray((rel // half) % 2 == 1)[None, :]
        freq = ROPE_BASE ** (-jnp.asarray(rel % half, f32) / half)
        rot = jnp.asarray(in_rope)[None, :]
        ang = pos * freq[None, :]
        sin = jnp.sin(ang)
        small += [jnp.where(rot, jnp.cos(ang), 1.0), jnp.where(rot, jnp.where(second, sin, -sin), 0.0)]
        by_col += [rel >= 2 * half] * 2
    small = jnp.concatenate(small, axis=1)
    by_col = jnp.asarray(np.concatenate(by_col))[None, None, :]
    shape = (GRID_W, GRID_W, small.shape[1])
    full = jnp.where(by_col, jnp.broadcast_to(small[None], shape), jnp.broadcast_to(small[:, None], shape))
    return full.reshape(T, small.shape[1])


def kernel(x, c, ctx, c_ctx, ada_w, ada_b, norm_g, ffn1_wi, ffn1_wo, ffn2_wi, ffn2_wo, w_in, w_out,
           mlstm_gate_b, mlstm_out_norm, mla_cq_norm, mla_ckv_norm, mla_w_uq, mla_w_ukv, mla_q_norm, mla_k_norm,
           gqa_q_norm, gqa_k_norm, gqa_sink):
    map_b, map_c = _head_lane_map_b(), _head_lane_map_c()
    nope_map = np.where(map_b < B_NOPE, map_b, -1)
    q_scale_b, q_scale_c = B_DQK ** -0.5 * LOG2E, C_DH ** -0.5 * LOG2E

    def arrange(w_in_l, gate_b_l, w_uq_l, w_ukv_l, bq_l, bk_l, cq_l, ck_l):
        ukv = w_ukv_l.reshape(B_KV_RANK, B_HEADS, B_NOPE + B_DV)
        gbq, gbk = _place(bq_l[None], map_b) * q_scale_b, _place(bk_l[None], map_b)
        gcq, gck = _place(cq_l[None], map_c) * q_scale_c, _place(ck_l[None], map_c)
        return dict(
            win=_arrange_w_in(w_in_l, map_b).astype(bf16),
            bias=_pad_cols(jnp.pad(gate_b_l[None], ((0, 0), (COL_AG, 0))), WP),
            wuq=_place_heads(w_uq_l, B_HEADS, map_b).astype(bf16),
            wukv=jnp.concatenate([_place_heads(ukv[:, :, :B_NOPE].reshape(B_KV_RANK, -1), B_HEADS, nope_map),
                                  ukv[:, :, B_NOPE:].reshape(B_KV_RANK, -1)], axis=1).astype(bf16),
            hg=jnp.concatenate([jnp.concatenate(pair, axis=1)
                                for pair in ((gbq, gbq), (gbk, gbk), (gcq, gcq), (gcq, gck))]))

    pw = jax.vmap(arrange)(w_in, mlstm_gate_b, mla_w_uq, mla_w_ukv, mla_q_norm, mla_k_norm, gqa_q_norm, gqa_k_norm)
    wi1, wo1, wi2, wo2 = (w.astype(bf16) for w in (ffn1_wi, ffn1_wo, ffn2_wi, ffn2_wo))
    c_rows = A_HEADS * A_DV + B_HEADS * B_DV
    wout = jnp.concatenate([w_out[:, :c_rows], _reorder_c_heads(w_out[:, c_rows:], 1)], axis=1).astype(bf16)
    cqn, ckvn, onorm = mla_cq_norm[:, None, :], mla_ckv_norm[:, None, :], mlstm_out_norm[:, None, :]

    cc = jnp.concatenate([c, c_ctx[None, :], jnp.zeros((MOD_ROWS - B - 1, D), f32)], axis=0)
    mod = _ada(cc, ada_w, ada_b).reshape(DEPTH, MOD_ROWS, N_MOD, D)
    tab = _rope_tables(map_b, map_c)
    xs = (x.reshape(NL, D), ctx.reshape(NCX, D))

    for l in range(DEPTH):
        need_ctx = l < DEPTH - 1
        x1, h = _ffn1(l, xs, mod, norm_g, wi1, wo1)
        pa, kt, gt, qb, kb, vb, qc, kc, vc = _inproj(
            l, h, tab, pw["win"], pw["bias"], cqn, ckvn, pw["wuq"], pw["wukv"], pw["hg"])
        hf, hb = _mlstm(pa, kt, gt)
        y_lat = (_mla(qb, kb, vb, True), _gqa(gqa_sink[l], qc, kc, vc, True))
        y_ctx = (_mla(qb, kb, vb, False), _gqa(gqa_sink[l], qc, kc, vc, False)) if need_ctx else None
        xs = _outproj_ffn(l, x1, mod, norm_g, hf, hb, pa, y_lat, y_ctx, onorm, wout, wi2, wo2)
    return xs.reshape(B, T, D)
```

```python
import functools
import math

import jax
import jax.numpy as jnp
import numpy as np
from jax import lax
from jax.experimental import pallas as pl
from jax.experimental.pallas import tpu as pltpu

f32 = jnp.float32
bf16 = jnp.bfloat16

D = 1024
B = 4
T = 4096
CTX = 256
DEPTH = 2
GRID_W = 64
ROPE_BASE = 10000.0
EPS = 1e-6
HALF = 0.5
N_MOD = 9
D_FF = 2816
A_HEADS, A_DK, A_DV, A_CHUNK = 4, 32, 64, 64
B_HEADS, B_Q_RANK, B_KV_RANK, B_NOPE, B_ROPE, B_DV = 6, 256, 128, 64, 32, 64
B_DQK = B_NOPE + B_ROPE
C_HEADS, C_KV_HEADS, C_DH, WINDOW = 6, 2, 64, 128
C_GROUP = C_HEADS // C_KV_HEADS

NL = B * T
NCX = B * CTX
N = NL + NCX

LANES = 128
HALF_LANES = LANES // 2
MOD_ROWS = 8
VMEM_LIMIT = 56 * 1024 * 1024
LOG2E = math.log2(math.e)

TM_FFN = 512
FF_CHUNK = 256
TM_IN = 512
ADA_TN = 1152
MLSTM_R = 256
MLA_TQ = 1024
MLA_KC = 512
GQA_TQ = 256
GQA_BAND = GQA_TQ + 2 * WINDOW

COL_AQ, COL_AV, COL_AO, COL_AG = 0, 128, 384, 640
COL_BCQ, COL_BCKV, COL_BKR = 768, 1024, 1152
COL_CQ, COL_CK, COL_CV = 1280, 1664, 1792
COL_AK = 1920
WP = 2048
C_HEAD_ORDER = tuple(h for g in range(C_GROUP) for h in (g, C_GROUP + g))
PA_V, PA_O, PA_Q, PA_G = 0, 256, 512, 640
PA_W = PA_G + 4 * LANES
N_GATES = 4 * A_HEADS


def _head_lane_map_b():
    m = -np.ones(LANES, np.int64)
    m[0:8], m[8:16], m[16:64] = np.arange(64, 72), np.arange(80, 88), np.arange(0, 48)
    m[64:72], m[72:80], m[80:96] = np.arange(72, 80), np.arange(88, 96), np.arange(48, 64)
    return m


def _head_lane_map_c():
    return np.concatenate([np.arange(C_DH), np.arange(C_DH)])


def _sigmoid(x):
    return 1.0 / (1.0 + jnp.exp(-x))


def _log_sigmoid(x):
    return jnp.minimum(x, 0.0) - jnp.log(1.0 + jnp.exp(-jnp.abs(x)))


def _rms(x, g):
    ms = jnp.mean(x * x, axis=-1, keepdims=True)
    return x * lax.rsqrt(ms + EPS) * g


def _dot(a, b):
    return jnp.dot(a, b, preferred_element_type=f32)


def _dot_nt(a, b):
    return lax.dot_general(a, b, (((1,), (1,)), ((), ())), preferred_element_type=f32)


def _split3(x):
    hi = x.astype(bf16)
    r1 = x - hi.astype(f32)
    mid = r1.astype(bf16)
    return hi, mid, (r1 - mid.astype(f32)).astype(bf16)


def _layer(arr, l):
    rest = (0,) * (arr.ndim - 1)
    return pl.BlockSpec((None,) + arr.shape[1:], lambda *_: (l,) + rest, pipeline_mode=pl.Buffered(1))


def _mod_spec(l, tm):
    tpb = T // tm
    return pl.BlockSpec((None, 1, N_MOD, D), lambda i: (l, i // tpb, 0, 0))


def _params(n_axes):
    return pltpu.CompilerParams(dimension_semantics=("arbitrary",) * n_axes, vmem_limit_bytes=VMEM_LIMIT)


def _ada_kernel(c_ref, w_ref, b_ref, o_ref):
    c = c_ref[...]
    s = c * _sigmoid(c)
    pieces = _split3(s)
    s3 = jnp.concatenate([piece.astype(f32) for piece in pieces], axis=0).astype(bf16)
    w = w_ref[0]
    w_hi = w.astype(bf16)
    w_lo = (w - w_hi.astype(f32)).astype(bf16)
    r = _dot(s3, w_hi)
    o_ref[0] = (r[0:MOD_ROWS] + r[MOD_ROWS:2 * MOD_ROWS] + r[2 * MOD_ROWS:]
                + _dot(pieces[0], w_lo) + b_ref[0])


def _ada(cc, ada_w, ada_b):
    nt = (N_MOD * D) // ADA_TN
    return pl.pallas_call(
        _ada_kernel,
        grid=(DEPTH, nt),
        in_specs=[
            pl.BlockSpec((MOD_ROWS, D), lambda l, j: (0, 0)),
            pl.BlockSpec((1, D, ADA_TN), lambda l, j: (l, 0, j)),
            pl.BlockSpec((1, 1, ADA_TN), lambda l, j: (l, 0, j)),
        ],
        out_specs=pl.BlockSpec((1, MOD_ROWS, ADA_TN), lambda l, j: (l, 0, j)),
        out_shape=jax.ShapeDtypeStruct((DEPTH, MOD_ROWS, N_MOD * D), f32),
        compiler_params=_params(2),
        name="ada_mod",
    )(cc, ada_w, ada_b.reshape(DEPTH, 1, N_MOD * D))


def _ffn(x, g, shift, scale, gate, wi_ref, wo_ref):
    h = (_rms(x, g) * (1.0 + scale) + shift).astype(bf16)
    acc = None
    for c in range(D_FF // FF_CHUNK):
        lo, hi = c * FF_CHUNK, (c + 1) * FF_CHUNK
        gt = _dot(h, wi_ref[:, lo:hi])
        up = _dot(h, wi_ref[:, D_FF + lo:D_FF + hi])
        a = (gt * _sigmoid(gt) * up).astype(bf16)
        part = _dot(a, wo_ref[lo:hi, :])
        acc = part if acc is None else acc + part
    return x + HALF * gate * acc


def _ffn1_kernel(*refs, split_input):
    if split_input:
        xl_ref, xc_ref, mod_ref, ng_ref, wi_ref, wo_ref, x1_ref, h_ref = refs
        x = jnp.where(pl.program_id(0) < NL // TM_FFN, xl_ref[...], xc_ref[...])
    else:
        x_ref, mod_ref, ng_ref, wi_ref, wo_ref, x1_ref, h_ref = refs
        x = x_ref[...]
    mod = mod_ref[0]
    x1 = _ffn(x, ng_ref[0:1, :], mod[0:1, :], mod[1:2, :], mod[2:3, :], wi_ref, wo_ref)
    x1_ref[...] = x1
    h_ref[...] = (_rms(x1, ng_ref[1:2, :]) * (1.0 + mod[4:5, :]) + mod[3:4, :]).astype(bf16)


def _ffn1(l, xs, mod, ng, wi, wo):
    split_input = isinstance(xs, tuple)
    nlt = NL // TM_FFN
    if split_input:
        assert NCX % TM_FFN == 0
        x_specs = [pl.BlockSpec((TM_FFN, D), lambda i: (jnp.minimum(i, nlt - 1), 0)),
                   pl.BlockSpec((TM_FFN, D), lambda i: (jnp.maximum(i - nlt, 0), 0))]
    else:
        xs = (xs,)
        x_specs = [pl.BlockSpec((TM_FFN, D), lambda i: (i, 0))]
    return pl.pallas_call(
        functools.partial(_ffn1_kernel, split_input=split_input),
        grid=(N // TM_FFN,),
        in_specs=x_specs + [_mod_spec(l, TM_FFN), _layer(ng, l), _layer(wi, l), _layer(wo, l)],
        out_specs=[pl.BlockSpec((TM_FFN, D), lambda i: (i, 0))] * 2,
        out_shape=[jax.ShapeDtypeStruct((N, D), f32), jax.ShapeDtypeStruct((N, D), bf16)],
        compiler_params=_params(1),
        name="ffn1",
    )(*xs, mod, ng, wi, wo)


def _outproj_ffn_kernel(*refs, with_ctx):
    if with_ctx:
        (x_ref, mod_ref, ng_ref, hf_ref, hb_ref, o_ref, ybl_ref, ycl_ref, ybc_ref, ycc_ref, on_ref, wout_ref,
         wi_ref, wo_ref, out_ref) = refs
        is_latent = pl.program_id(0) < NL // TM_FFN
        yb = jnp.where(is_latent, ybl_ref[...], ybc_ref[...])
        yc = jnp.where(is_latent, ycl_ref[...], ycc_ref[...])
    else:
        (x_ref, mod_ref, ng_ref, hf_ref, hb_ref, o_ref, yb_ref, yc_ref, on_ref, wout_ref,
         wi_ref, wo_ref, out_ref) = refs
        yb, yc = yb_ref[...], yc_ref[...]
    mod = mod_ref[0]
    hs = hf_ref[...] + hb_ref[...]
    sq = hs * hs
    head = lax.broadcasted_iota(jnp.int32, (1, A_HEADS * A_DV), 1) // A_DV
    ms = jnp.zeros_like(hs)
    for hh in range(A_HEADS):
        sel = head == hh
        ssh = jnp.sum(jnp.where(sel, sq, 0.0), axis=-1, keepdims=True) * (1.0 / A_DV)
        ms = jnp.where(sel, ssh, ms)
    ya = _sigmoid(o_ref[...]) * (hs * lax.rsqrt(ms + EPS) * on_ref[...])
    y = jnp.concatenate([ya.astype(bf16), yb, yc], axis=-1)
    x2 = x_ref[...] + mod[5:6, :] * _dot(y, wout_ref[...])
    out_ref[...] = _ffn(x2, ng_ref[2:3, :], mod[6:7, :], mod[7:8, :], mod[8:9, :], wi_ref, wo_ref)


def _outproj_ffn(l, x1, mod, ng, hf, hb, pa, y_lat, y_ctx, onorm, wout, wi, wo):
    with_ctx = y_ctx is not None
    rows = N if with_ctx else NL
    nlt = NL // TM_FFN
    row = lambda w, c=0: pl.BlockSpec((TM_FFN, w), lambda i: (i, c))
    lat = lambda w: pl.BlockSpec((TM_FFN, w), lambda i: (jnp.minimum(i, nlt - 1), 0))
    ctx = lambda w: pl.BlockSpec((TM_FFN, w), lambda i: (jnp.maximum(i - nlt, 0), 0))
    y_specs = [lat(B_HEADS * B_DV), lat(C_HEADS * C_DH)]
    if with_ctx:
        y_specs += [ctx(B_HEADS * B_DV), ctx(C_HEADS * C_DH)]
    return pl.pallas_call(
        functools.partial(_outproj_ffn_kernel, with_ctx=with_ctx),
        grid=(rows // TM_FFN,),
        in_specs=[
            row(D), _mod_spec(l, TM_FFN), _layer(ng, l),
            row(A_HEADS * A_DV), row(A_HEADS * A_DV), row(A_HEADS * A_DV, PA_O // (A_HEADS * A_DV)),
        ] + y_specs + [_layer(onorm, l), _layer(wout, l), _layer(wi, l), _layer(wo, l)],
        out_specs=row(D),
        out_shape=jax.ShapeDtypeStruct((rows, D), f32),
        compiler_params=_params(1),
        name="outproj_ffn2",
    )(x1, mod, ng, hf, hb, pa, *y_lat, *(y_ctx or ()), onorm, wout, wi, wo)


def _inproj_kernel(h_ref, win_ref, bias_ref, tab_ref, cqn_ref, ckvn_ref, wuq_ref, wukv_ref, hg_ref,
                   pa_ref, kt_ref, gt_ref, qb_ref, kb_ref, vb_ref, qc_ref, kc_ref, vc_ref):
    h = h_ref[...]
    p = _dot(h, win_ref[...]) + bias_ref[...]

    pa_ref[:, PA_V:PA_Q] = p[:, COL_AV:COL_AG]
    pa_ref[:, PA_Q:PA_G] = p[:, COL_AQ:COL_AV] * (A_DK ** -0.5)
    graw = p[:, COL_AG:COL_AG + LANES]
    lane = lax.broadcasted_iota(jnp.int32, (1, LANES), 1)
    for kk in range(4):
        gk = graw if kk == 0 else pltpu.roll(graw, LANES - A_HEADS * kk, 1)
        if kk % 2 == 1:
            gk = _log_sigmoid(gk)
        pa_ref[:, PA_G + LANES * kk:PA_G + LANES * (kk + 1)] = jnp.where(lane < A_HEADS, gk, 0.0)

    grow = lax.broadcasted_iota(jnp.int32, (2 * N_GATES, 1), 0)
    is_forget = ((grow // A_HEADS) % 2 == 1) == (grow < N_GATES)
    gate = grow % N_GATES
    source_lane = jnp.where(grow < N_GATES, gate, ((gate // A_HEADS) ^ 1) * A_HEADS + gate % A_HEADS)
    pick_gates = (lane == source_lane).astype(bf16)
    eye = (lax.broadcasted_iota(jnp.int32, (LANES, LANES), 0)
           == lax.broadcasted_iota(jnp.int32, (LANES, LANES), 1)).astype(bf16)
    keys = p[:, COL_AK:COL_AK + LANES].astype(bf16)
    gate_pieces = _split3(graw)
    for c in range(TM_IN // A_CHUNK):
        rows = slice(c * A_CHUNK, (c + 1) * A_CHUNK)
        kt_ref[c] = _dot_nt(eye, keys[rows, :])
        g = sum(_dot_nt(pick_gates, piece[rows, :]) for piece in gate_pieces)
        gt_ref[c] = jnp.where(is_forget, _log_sigmoid(g), g)

    is_latent = pl.program_id(0) < NL // TM_IN
    tab = tab_ref[...]
    tb = [jnp.where(is_latent, tab[:, LANES * i:LANES * (i + 1)], 1.0 - (i % 2)) for i in range(4)]
    hg = hg_ref[...]

    cq = _rms(p[:, COL_BCQ:COL_BCKV], cqn_ref[...]).astype(bf16)
    ckv = _rms(p[:, COL_BCKV:COL_BKR], ckvn_ref[...]).astype(bf16)
    kr = p[:, COL_BKR:COL_CQ]
    q = _dot(cq, wuq_ref[...])
    kv = _dot(ckv, wukv_ref[...])
    vb_ref[...] = kv[:, B_HEADS * LANES:].astype(bf16)
    vc_ref[...] = p[:, COL_CV:COL_CV + LANES].astype(bf16)

    two = lambda a: jnp.concatenate([a, a], axis=1)
    pair = lambda a, j, first=0: a[:, first + 2 * LANES * j:first + 2 * LANES * (j + 1)]
    cos_b, sin_b, cos_c, sin_c = (two(t) for t in tb)
    kr2 = two(kr)
    ri = lax.broadcasted_iota(jnp.int32, (2 * LANES, 2 * LANES), 0)
    ci = lax.broadcasted_iota(jnp.int32, (2 * LANES, 2 * LANES), 1)
    swap_b = ((ri // LANES == ci // LANES) & (ri % LANES == (ci + HALF_LANES) % LANES)).astype(bf16)
    half_c = C_DH // 2
    swap_c = ((ri // half_c == ci // half_c) & (ri % half_c == (ci + half_c // 2) % half_c)).astype(bf16)
    full = slice(0, 2 * LANES)
    jobs = []
    for j in range(B_HEADS // 2):
        dst = slice(2 * LANES * j, 2 * LANES * (j + 1))
        jobs.append((pair(q, j), hg[0:1, :], LANES, B_DQK, cos_b, sin_b, swap_b, [(qb_ref, dst, full)]))
        jobs.append((pair(kv, j) + kr2, hg[1:2, :], LANES, B_DQK, cos_b, sin_b, swap_b, [(kb_ref, dst, full)]))
    jobs.append((pair(p, 0, COL_CQ), hg[2:3, :], C_DH, C_DH, cos_c, sin_c, swap_c, [(qc_ref, full, full)]))
    jobs.append((pair(p, 1, COL_CQ), hg[3:4, :], C_DH, C_DH, cos_c, sin_c, swap_c,
                 [(qc_ref, slice(2 * LANES, 3 * LANES), slice(0, LANES)),
                  (kc_ref, slice(0, LANES), slice(LANES, 2 * LANES))]))
    gained = [x * gain for x, gain, *_ in jobs]
    rolled = [_dot(xg.astype(bf16), job[6]) for job, xg in zip(jobs, gained)]
    lane = lax.broadcasted_iota(jnp.int32, (1, LANES), 1)
    lane2 = lax.broadcasted_iota(jnp.int32, (1, 2 * LANES), 1)
    sums = []
    for x, _, width, *_ in jobs:
        sq = x * x
        parts = []
        for t in range(2):
            blk = sq[:, LANES * t:LANES * (t + 1)]
            if width == LANES:
                parts.append(jnp.sum(blk, axis=-1, keepdims=True))
            else:
                parts.append(jnp.sum(jnp.where(lane < width, blk, 0.0), axis=-1, keepdims=True))
                parts.append(jnp.sum(jnp.where(lane < width, 0.0, blk), axis=-1, keepdims=True))
        ss = parts[-1]
        for k in range(len(parts) - 2, -1, -1):
            ss = jnp.where(lane2 < (k + 1) * width, parts[k], ss)
        sums.append(ss)
    for (_, _, _, n_real, cos, sin, _, dests), xg, xr, ss in zip(jobs, gained, rolled, sums):
        out = ((xg * cos + xr * sin) * lax.rsqrt(ss * (1.0 / n_real) + EPS)).astype(bf16)
        for dst_ref, dst_cols, src_cols in dests:
            dst_ref[:, dst_cols] = out[:, src_cols]


def _inproj(l, h, tab, win, bias, cqn, ckvn, wuq, wukv, hg):
    tpb = T // TM_IN
    row = lambda w: pl.BlockSpec((TM_IN, w), lambda i: (i, 0))
    chunked = lambda r: pl.BlockSpec((TM_IN // A_CHUNK, r, A_CHUNK), lambda i: (i, 0, 0))
    out_w = [(B_HEADS * LANES, bf16), (B_HEADS * LANES, bf16), (B_HEADS * B_DV, bf16),
             (C_HEADS * C_DH, bf16), (C_KV_HEADS * C_DH, bf16), (C_KV_HEADS * C_DH, bf16)]
    return pl.pallas_call(
        _inproj_kernel,
        grid=(N // TM_IN,),
        in_specs=[
            row(D), _layer(win, l), _layer(bias, l),
            pl.BlockSpec((TM_IN, 4 * LANES), lambda i: (jnp.where(i < NL // TM_IN, i % tpb, 0), 0)),
            _layer(cqn, l), _layer(ckvn, l), _layer(wuq, l), _layer(wukv, l),
            _layer(hg, l),
        ],
        out_specs=[row(PA_W), chunked(LANES), chunked(2 * N_GATES)] + [row(w) for w, _ in out_w],
        out_shape=[jax.ShapeDtypeStruct((N, PA_W), f32),
                   jax.ShapeDtypeStruct((N // A_CHUNK, LANES, A_CHUNK), f32),
                   jax.ShapeDtypeStruct((N // A_CHUNK, 2 * N_GATES, A_CHUNK), f32)]
        + [jax.ShapeDtypeStruct((N, w), dt) for w, dt in out_w],
        compiler_params=_params(1),
        name="inproj",
    )(h, win, bias, tab, cqn, ckvn, wuq, wukv, hg)


def _cummax_rows(x, rev):
    n = x.shape[0]
    row = lax.broadcasted_iota(jnp.int32, (n, 1), 0)
    sh = 1
    while sh < n:
        if rev:
            x = jnp.maximum(x, jnp.where(row < n - sh, pltpu.roll(x, n - sh, 0), -jnp.inf))
        else:
            x = jnp.maximum(x, jnp.where(row >= sh, pltpu.roll(x, sh, 0), -jnp.inf))
        sh *= 2
    return x


def _stack_heads(pieces):
    return jnp.concatenate(pieces, axis=0)


def _mlstm_kernel(*refs):
    streams = [(refs[0:7] + refs[14:15], False), (refs[7:14] + refs[15:16], True)]
    s_ref, ml_ref, ms_ref = refs[16:19]

    @pl.when(pl.program_id(1) == 0)
    def _():
        s_ref[...] = jnp.zeros_like(s_ref)
        ml_ref[...] = jnp.zeros_like(ml_ref)
        ms_ref[...] = jnp.zeros_like(ms_ref)

    L = A_CHUNK
    n_chunks = MLSTM_R // L
    heads = range(A_HEADS)
    ti = lax.broadcasted_iota(jnp.int32, (L, L), 0)
    si = lax.broadcasted_iota(jnp.int32, (L, L), 1)
    lane = lax.broadcasted_iota(jnp.int32, (1, LANES), 1)
    row8 = lax.broadcasted_iota(jnp.int32, (2 * A_HEADS, 1), 0)
    in_head = [(lane >= hh * A_DK) & (lane < (hh + 1) * A_DK) for hh in heads]
    ones_blk = jnp.ones((L, LANES), bf16)

    items = []
    for sidx, (srefs, rev) in enumerate(streams):
        q_ref, kt_ref, v_ref, ig_ref, lf_ref, gt_ref, gts_ref, h_ref = srefs
        attend = (si >= ti) if rev else (si <= ti)
        attend4 = _stack_heads([attend] * A_HEADS)
        cum_cols = attend.astype(bf16)
        cum_rows = ((ti >= si) if rev else (ti <= si)).astype(bf16)
        m_lane = ml_ref[sidx, 0:1, :]
        m_sub = ms_ref[sidx, :, 0:1]
        for cc in (range(n_chunks - 1, -1, -1) if rev else range(n_chunks)):
            rows = slice(cc * L, (cc + 1) * L)
            ig = ig_ref[rows, :]
            lf = lf_ref[rows, :]
            gt = gt_ref[cc]
            gts = gts_ref[cc]
            b_col = sum(_dot(cum_cols, piece) for piece in _split3(lf))
            r_col = ig - b_col
            big_m = jnp.maximum(m_lane, _cummax_rows(r_col, rev))
            mt_col = b_col + big_m
            b_last_l = jnp.sum(lf, axis=0, keepdims=True)
            m_new_l = jnp.maximum(m_lane, jnp.max(r_col, axis=0, keepdims=True)) + b_last_l
            b_rows = sum(_dot(piece, cum_rows) for piece in _split3(gts))
            live = row8 < A_HEADS
            r8 = jnp.where(live, gt - b_rows, 0.0)
            b_last_s = jnp.where(live, jnp.sum(gts, axis=-1, keepdims=True), 0.0)
            r_max_s = jnp.max(r8, axis=-1, keepdims=True)
            wg8 = jnp.exp(r8 - r_max_s)
            m_new_s = jnp.maximum(m_sub, r_max_s) + b_last_s
            decay_s = jnp.exp(b_last_s + m_sub - m_new_s)
            scale_s = jnp.exp(b_last_s + r_max_s - m_new_s)
            expand = lambda a, n: _stack_heads([jnp.broadcast_to(a[hh:hh + 1, :], (n, a.shape[1])) for hh in heads])
            q = q_ref[rows, :]
            big_m_b = _stack_heads([jnp.broadcast_to(big_m[:, hh:hh + 1], (L, LANES)) for hh in heads])
            mt_b = _stack_heads([jnp.broadcast_to(mt_col[:, hh:hh + 1], (L, LANES)) for hh in heads])
            m_old_b = _stack_heads([jnp.broadcast_to(m_lane[:, hh:hh + 1], (L, LANES)) for hh in heads])
            q_stack = _stack_heads([jnp.where(in_head[hh], q, 0.0) for hh in heads])
            items.append(dict(
                sidx=sidx, rows=rows, h_ref=h_ref,
                qst=q_stack.astype(bf16),
                qa=(q_stack * jnp.exp(m_old_b - big_m_b)).astype(bf16),
                kt=kt_ref[cc].astype(bf16),
                kw=(kt_ref[cc] * expand(wg8, A_DK)).astype(bf16),
                vo=jnp.concatenate([v_ref[rows, :].astype(bf16), ones_blk], axis=1),
                w=jnp.exp(jnp.where(attend4, expand(r8, L) - big_m_b[:, 0:L], -jnp.inf)),
                floor=jnp.exp(-mt_b),
                decay=jnp.broadcast_to(expand(decay_s, A_DK), (LANES, LANES)),
                kv_scale=jnp.broadcast_to(expand(scale_s, A_DK), (LANES, LANES))))
            m_lane, m_sub = m_new_l, m_new_s
        ml_ref[sidx, 0:1, :] = m_lane
        ms_ref[sidx, :, 0:1] = m_sub

    for it in items:
        it["s"] = _dot(it["qst"], it["kt"])

    for it in items:
        it["kv"] = _dot(it["kw"], it["vo"])

    for it in items:
        it["p"] = (it["s"] * it["w"]).astype(bf16)

    state = [s_ref[0], s_ref[1]]
    tile3 = lambda a: jnp.concatenate([a] * 3, axis=1)
    for it in items:
        st = state[it["sidx"]]
        it["c_in"] = st.astype(bf16)
        state[it["sidx"]] = tile3(it["decay"]) * st + tile3(it["kv_scale"]) * it["kv"]
    s_ref[0] = state[0]
    s_ref[1] = state[1]

    nv = A_HEADS * A_DV
    for it in items:
        out = _dot(jnp.concatenate([it["qa"], it["p"]], axis=1),
                   jnp.concatenate([it["c_in"], it["vo"]], axis=0))
        den = jnp.maximum(jnp.abs(out[:, nv:]), it["floor"])
        for pair in range(A_HEADS // 2):
            sl = slice(LANES * pair, LANES * (pair + 1))
            even, odd = (slice(L * hh, L * (hh + 1)) for hh in (2 * pair, 2 * pair + 1))
            it["h_ref"][it["rows"], sl] = jnp.where(lane < A_DV, out[even, sl] / den[even, :], out[odd, sl] / den[odd, :])


def _mlstm(pa, kt, gt):
    nb = T // MLSTM_R
    nc = MLSTM_R // A_CHUNK
    assert CTX == MLSTM_R

    def rb(rev):
        def f(b, j):
            jj = j - 1
            return jnp.where(j == 0, NL // MLSTM_R + b, b * nb + (nb - 1 - jj if rev else jj))
        return f

    def stream_specs(rev):
        r = rb(rev)
        d = 2 if rev else 0
        col = lambda w, c: pl.BlockSpec((MLSTM_R, w), lambda b, j: (r(b, j), c))
        return [
            col(LANES, PA_Q // LANES),
            pl.BlockSpec((nc, LANES, A_CHUNK), lambda b, j: (r(b, j), 0, 0)),
            col(2 * LANES, PA_V // (2 * LANES)),
            col(LANES, PA_G // LANES + d), col(LANES, PA_G // LANES + d + 1),
            pl.BlockSpec((nc, 2 * A_HEADS, A_CHUNK), lambda b, j: (r(b, j), d // 2, 0)),
            pl.BlockSpec((nc, 2 * A_HEADS, A_CHUNK), lambda b, j: (r(b, j), 2 + d // 2, 0)),
        ]

    out_spec = lambda rev: pl.BlockSpec((MLSTM_R, A_HEADS * A_DV), lambda b, j: (rb(rev)(b, j), 0))
    return pl.pallas_call(
        _mlstm_kernel,
        grid=(B, nb + 1),
        in_specs=stream_specs(False) + stream_specs(True),
        out_specs=[out_spec(False), out_spec(True)],
        out_shape=[jax.ShapeDtypeStruct((N, A_HEADS * A_DV), f32)] * 2,
        scratch_shapes=[pltpu.VMEM((2, LANES, 3 * LANES), f32), pltpu.VMEM((2, 8, LANES), f32),
                        pltpu.VMEM((2, 8, LANES), f32)],
        compiler_params=_params(2),
        name="mlstm",
    )(*[pa, kt, pa, pa, pa, gt, gt] * 2)


def _tile_max(s, m128):
    for t in range(s.shape[1] // LANES):
        blk = s[:, LANES * t:LANES * (t + 1)]
        m128 = blk if m128 is None else jnp.maximum(m128, blk)
    return m128


def _mla_kernel(*refs, latent):
    if latent:
        q_ref, kc_ref, vc_ref, kl_ref, vl_ref, o_ref, s_ref = refs
        sources = [(kc_ref, vc_ref, 0, CTX)] + [(kl_ref, vl_ref, c, MLA_KC) for c in range(0, T, MLA_KC)]
    else:
        q_ref, kc_ref, vc_ref, o_ref, s_ref = refs
        sources = [(kc_ref, vc_ref, 0, CTX)]
    q = q_ref[...]
    lane = lax.broadcasted_iota(jnp.int32, (1, LANES), 1)
    row_max = []
    for hh in range(2):
        sl = slice(LANES * hh, LANES * (hh + 1))
        qh = q[:, sl]
        m128 = None
        off = 0
        for k_ref, _, r0, n in sources:
            s = _dot_nt(qh, k_ref[r0:r0 + n, sl])
            s_ref[hh, :, off:off + n] = s
            m128 = _tile_max(s, m128)
            off += n
        row_max.append(jnp.max(m128, axis=-1, keepdims=True))
    outs = []
    for hh in range(2):
        den_lane = B_DV if hh == 0 else 0
        acc = None
        off = 0
        for _, v_ref, r0, n in sources:
            p = jnp.exp2(s_ref[hh, :, off:off + n] - row_max[hh]).astype(bf16)
            vext = jnp.where(lane == den_lane, 1.0, v_ref[r0:r0 + n, :]).astype(bf16)
            part = _dot(p, vext)
            acc = part if acc is None else acc + part
            off += n
        outs.append(acc / acc[:, den_lane:den_lane + 1])
    o_ref[...] = jnp.where(lane < B_DV, outs[0], outs[1]).astype(o_ref.dtype)


def _mla(qb, kb, vb, latent):
    npair = B_HEADS // 2
    ctx_blk = NL // CTX
    kv_specs = [
        pl.BlockSpec((CTX, 2 * LANES), lambda b, p, i: (ctx_blk + b, p)),
        pl.BlockSpec((CTX, LANES), lambda b, p, i: (ctx_blk + b, p)),
    ]
    if latent:
        tq = MLA_TQ
        nq = T // tq
        qmap = omap = lambda b, p, i: (b * nq + i, p)
        kv_specs += [
            pl.BlockSpec((T, 2 * LANES), lambda b, p, i: (b, p)),
            pl.BlockSpec((T, LANES), lambda b, p, i: (b, p)),
        ]
        args = (qb, kb, vb, kb, vb)
        nkeys = CTX + T
    else:
        tq = CTX
        nq = 1
        qmap = lambda b, p, i: (ctx_blk + b, p)
        omap = lambda b, p, i: (b, p)
        args = (qb, kb, vb)
        nkeys = CTX
    return pl.pallas_call(
        functools.partial(_mla_kernel, latent=latent),
        grid=(B, npair, nq),
        in_specs=[pl.BlockSpec((tq, 2 * LANES), qmap)] + kv_specs,
        out_specs=pl.BlockSpec((tq, LANES), omap),
        out_shape=jax.ShapeDtypeStruct((NL if latent else NCX, B_HEADS * B_DV), bf16),
        scratch_shapes=[pltpu.VMEM((2, tq, nkeys), f32)],
        compiler_params=_params(3),
        name="mla_latent" if latent else "mla_context",
    )(*args)


def _gqa_kernel(sink_ref, *refs, latent):
    if latent:
        q_ref, kc_ref, vc_ref, kl_ref, vl_ref, o_ref = refs
    else:
        q_ref, kc_ref, vc_ref, o_ref = refs
    q = q_ref[...]
    tq = q.shape[0]
    lane = lax.broadcasted_iota(jnp.int32, (1, LANES), 1)
    keys = kc_ref[...]
    vals = vc_ref[...]
    valid = None
    if latent:
        n = pl.program_id(1)
        start = pl.multiple_of(jnp.clip(n * GQA_TQ - WINDOW, 0, T - GQA_BAND), WINDOW)
        keys = jnp.concatenate([keys, kl_ref[pl.ds(start, GQA_BAND), :]], axis=0)
        vals = jnp.concatenate([vals, vl_ref[pl.ds(start, GQA_BAND), :]], axis=0)
        qpos = n * GQA_TQ + lax.broadcasted_iota(jnp.int32, (tq, 1), 0)
        kidx = lax.broadcasted_iota(jnp.int32, (1, CTX + GQA_BAND), 1)
        valid = (kidx < CTX) | (jnp.abs(qpos - (start - CTX + kidx)) <= WINDOW)
    lo = lane < C_DH
    outs = [[], []]
    for kvh in range(C_KV_HEADS):
        mine = lo if kvh == 0 else ~lo
        qs = jnp.concatenate([jnp.where(mine, q[:, LANES * g:LANES * (g + 1)], 0) for g in range(C_GROUP)], axis=0)
        s_all = _dot_nt(qs, keys)
        den_lane = C_DH * (1 - kvh)
        vext = jnp.where(lane == den_lane, 1.0, vals).astype(bf16)
        for g in range(C_GROUP):
            s = s_all[g * tq:(g + 1) * tq, :]
            if latent:
                s = jnp.where(valid, s, -jnp.inf)
            sink = sink_ref[C_GROUP * kvh + g] * LOG2E
            m = jnp.maximum(sink, jnp.max(_tile_max(s, None), axis=-1, keepdims=True))
            acc = _dot(jnp.exp2(s - m).astype(bf16), vext)
            outs[kvh].append(acc / (jnp.exp2(sink - m) + acc[:, den_lane:den_lane + 1]))
    for g in range(C_GROUP):
        o_ref[:, LANES * g:LANES * (g + 1)] = jnp.where(lo, outs[0][g], outs[1][g]).astype(o_ref.dtype)


def _gqa(sink, qc, kc, vc, latent):
    ctx_blk = NL // CTX
    kv_specs = [
        pl.BlockSpec((CTX, LANES), lambda b, i: (ctx_blk + b, 0)),
        pl.BlockSpec((CTX, LANES), lambda b, i: (ctx_blk + b, 0)),
    ]
    if latent:
        tq = GQA_TQ
        nq = T // tq
        qmap = omap = lambda b, i: (b * nq + i, 0)
        kv_specs += [
            pl.BlockSpec((T, LANES), lambda b, i: (b, 0)),
            pl.BlockSpec((T, LANES), lambda b, i: (b, 0)),
        ]
        args = (sink, qc, kc, vc, kc, vc)
    else:
        tq = CTX
        nq = 1
        qmap = lambda b, i: (ctx_blk + b, 0)
        omap = lambda b, i: (b, 0)
        args = (sink, qc, kc, vc)
    return pl.pallas_call(
        functools.partial(_gqa_kernel, latent=latent),
        grid=(B, nq),
        in_specs=[pl.BlockSpec(memory_space=pltpu.SMEM), pl.BlockSpec((tq, C_HEADS * C_DH), qmap)] + kv_specs,
        out_specs=pl.BlockSpec((tq, C_HEADS * C_DH), omap),
        out_shape=jax.ShapeDtypeStruct((NL if latent else NCX, C_HEADS * C_DH), bf16),
        compiler_params=_params(2),
        name="gqa_latent" if latent else "gqa_context",
    )(*args)


def _pad_cols(w, width):
    return jnp.pad(w, ((0, 0), (0, width - w.shape[1])))


def _lane_runs(lane_map):
    runs = []
    for src in lane_map:
        src = int(src)
        if runs and ((src < 0 and runs[-1][0] < 0) or (src >= 0 and runs[-1][0] >= 0 and src == sum(runs[-1]))):
            runs[-1] = (runs[-1][0], runs[-1][1] + 1)
        else:
            runs.append((src, 1))
    return runs


def _place(w, lane_map):
    parts = [jnp.zeros(w.shape[:-1] + (n,), w.dtype) if s < 0 else w[..., s:s + n] for s, n in _lane_runs(lane_map)]
    return jnp.concatenate(parts, axis=-1)


def _place_heads(w, heads, lane_map):
    r = w.shape[0]
    return _place(w.reshape(r, heads, -1), lane_map).reshape(r, heads * LANES)


def _reorder_c_heads(w, axis):
    heads = jnp.split(w, C_HEADS, axis=axis)
    return jnp.concatenate([heads[h] for h in C_HEAD_ORDER], axis=axis)


IN_SIZES = (A_HEADS * A_DK, A_HEADS * A_DK, A_HEADS * A_DV, A_HEADS * A_DV, N_GATES,
            B_Q_RANK, B_KV_RANK, B_ROPE, C_HEADS * C_DH, C_KV_HEADS * C_DH, C_KV_HEADS * C_DH)
IN_OFFSETS = tuple(int(v) for v in np.cumsum((0,) + IN_SIZES))


def _in_part(w, i):
    return w[..., IN_OFFSETS[i]:IN_OFFSETS[i + 1]]


def _arrange_w_in(w, map_b):
    part = lambda i: _in_part(w, i)
    kr = _place(part(7), np.where(map_b >= B_NOPE, map_b - B_NOPE, -1))
    return jnp.concatenate([
        part(0), part(2), part(3), _pad_cols(part(4), LANES),
        part(5), part(6), kr,
        _reorder_c_heads(part(8), 1), part(9), part(10), part(1),
    ], axis=1)


def _rope_tables(map_b, map_c):
    assert T == GRID_W * GRID_W
    pos = jnp.arange(GRID_W, dtype=f32)[:, None]
    small, by_col = [], []
    for lane_map, rope_start, half in ((map_b, B_NOPE, B_ROPE // 4), (map_c, 0, C_DH // 4)):
        rel = lane_map - rope_start
        in_rope = (lane_map >= 0) & (rel >= 0) & (rel < 4 * half)
        rel = np.where(in_rope, rel, 0)
        second = jnp.asarray((rel // half) % 2 == 1)[None, :]
        freq = ROPE_BASE ** (-jnp.asarray(rel % half, f32) / half)
        rot = jnp.asarray(in_rope)[None, :]
        ang = pos * freq[None, :]
        sin = jnp.sin(ang)
        small += [jnp.where(rot, jnp.cos(ang), 1.0), jnp.where(rot, jnp.where(second, sin, -sin), 0.0)]
        by_col += [rel >= 2 * half] * 2
    small = jnp.concatenate(small, axis=1)
    by_col = jnp.asarray(np.concatenate(by_col))[None, None, :]
    shape = (GRID_W, GRID_W, small.shape[1])
    full = jnp.where(by_col, jnp.broadcast_to(small[None], shape), jnp.broadcast_to(small[:, None], shape))
    return full.reshape(T, small.shape[1])


def kernel(x, c, ctx, c_ctx, ada_w, ada_b, norm_g, ffn1_wi, ffn1_wo, ffn2_wi, ffn2_wo, w_in, w_out,
           mlstm_gate_b, mlstm_out_norm, mla_cq_norm, mla_ckv_norm, mla_w_uq, mla_w_ukv, mla_q_norm, mla_k_norm,
           gqa_q_norm, gqa_k_norm, gqa_sink):
    map_b, map_c = _head_lane_map_b(), _head_lane_map_c()
    nope_map = np.where(map_b < B_NOPE, map_b, -1)
    q_scale_b, q_scale_c = B_DQK ** -0.5 * LOG2E, C_DH ** -0.5 * LOG2E

    def arrange(w_in_l, gate_b_l, w_uq_l, w_ukv_l, bq_l, bk_l, cq_l, ck_l):
        ukv = w_ukv_l.reshape(B_KV_RANK, B_HEADS, B_NOPE + B_DV)
        gbq, gbk = _place(bq_l[None], map_b) * q_scale_b, _place(bk_l[None], map_b)
        gcq, gck = _place(cq_l[None], map_c) * q_scale_c, _place(ck_l[None], map_c)
        return dict(
            win=_arrange_w_in(w_in_l, map_b).astype(bf16),
            bias=_pad_cols(jnp.pad(gate_b_l[None], ((0, 0), (COL_AG, 0))), WP),
            wuq=_place_heads(w_uq_l, B_HEADS, map_b).astype(bf16),
            wukv=jnp.concatenate([_place_heads(ukv[:, :, :B_NOPE].reshape(B_KV_RANK, -1), B_HEADS, nope_map),
                                  ukv[:, :, B_NOPE:].reshape(B_KV_RANK, -1)], axis=1).astype(bf16),
            hg=jnp.concatenate([jnp.concatenate(pair, axis=1)
                                for pair in ((gbq, gbq), (gbk, gbk), (gcq, gcq), (gcq, gck))]))

    pw = jax.vmap(arrange)(w_in, mlstm_gate_b, mla_w_uq, mla_w_ukv, mla_q_norm, mla_k_norm, gqa_q_norm, gqa_k_norm)
    wi1, wo1, wi2, wo2 = (w.astype(bf16) for w in (ffn1_wi, ffn1_wo, ffn2_wi, ffn2_wo))
    c_rows = A_HEADS * A_DV + B_HEADS * B_DV
    wout = jnp.concatenate([w_out[:, :c_rows], _reorder_c_heads(w_out[:, c_rows:], 1)], axis=1).astype(bf16)
    cqn, ckvn, onorm = mla_cq_norm[:, None, :], mla_ckv_norm[:, None, :], mlstm_out_norm[:, None, :]

    cc = jnp.concatenate([c, c_ctx[None, :], jnp.zeros((MOD_ROWS - B - 1, D), f32)], axis=0)
    mod = _ada(cc, ada_w, ada_b).reshape(DEPTH, MOD_ROWS, N_MOD, D)
    tab = _rope_tables(map_b, map_c)
    xs = (x.reshape(NL, D), ctx.reshape(NCX, D))

    for l in range(DEPTH):
        need_ctx = l < DEPTH - 1
        x1, h = _ffn1(l, xs, mod, norm_g, wi1, wo1)
        pa, kt, gt, qb, kb, vb, qc, kc, vc = _inproj(
            l, h, tab, pw["win"], pw["bias"], cqn, ckvn, pw["wuq"], pw["wukv"], pw["hg"])
        hf, hb = _mlstm(pa, kt, gt)
        y_lat = (_mla(qb, kb, vb, True), _gqa(gqa_sink[l], qc, kc, vc, True))
        y_ctx = (_mla(qb, kb, vb, False), _gqa(gqa_sink[l], qc, kc, vc, False)) if need_ctx else None
        xs = _outproj_ffn(l, x1, mod, norm_g, hf, hb, pa, y_lat, y_ctx, onorm, wout, wi2, wo2)
    return xs.reshape(B, T, D)
```

```python
import functools
import math

import jax
import jax.numpy as jnp
import numpy as np
from jax import lax
from jax.experimental import pallas as pl
from jax.experimental.pallas import tpu as pltpu

f32 = jnp.float32
bf16 = jnp.bfloat16

D = 1024
B = 4
T = 4096
CTX = 256
DEPTH = 2
GRID_W = 64
ROPE_BASE = 10000.0
EPS = 1e-6
HALF = 0.5
N_MOD = 9
D_FF = 2816
A_HEADS, A_DK, A_DV, A_CHUNK = 4, 32, 64, 64
B_HEADS, B_Q_RANK, B_KV_RANK, B_NOPE, B_ROPE, B_DV = 6, 256, 128, 64, 32, 64
B_DQK = B_NOPE + B_ROPE
C_HEADS, C_KV_HEADS, C_DH, WINDOW = 6, 2, 64, 128
C_GROUP = C_HEADS // C_KV_HEADS

NL = B * T
NCX = B * CTX
N = NL + NCX

LANES = 128
HALF_LANES = LANES // 2
MOD_ROWS = 8
VMEM_LIMIT = 56 * 1024 * 1024
LOG2E = math.log2(math.e)

TM_FFN = 512
FF_CHUNK = 256
TM_IN = 512
ADA_TN = 1152
MLSTM_R = 256
MLA_TQ = 1024
MLA_KC = 512
GQA_TQ = 256
GQA_BAND = GQA_TQ + 2 * WINDOW

COL_AQ, COL_AV, COL_AO, COL_AG = 0, 128, 384, 640
COL_BCQ, COL_BCKV, COL_BKR = 768, 1024, 1152
COL_CQ, COL_CK, COL_CV = 1280, 1664, 1792
COL_AK = 1920
WP = 2048
C_HEAD_ORDER = tuple(h for g in range(C_GROUP) for h in (g, C_GROUP + g))
PA_V, PA_O, PA_Q, PA_G = 0, 256, 512, 640
PA_W = PA_G + 4 * LANES
N_GATES = 4 * A_HEADS


def _head_lane_map_b():
    m = -np.ones(LANES, np.int64)
    m[0:8], m[8:16], m[16:64] = np.arange(64, 72), np.arange(80, 88), np.arange(0, 48)
    m[64:72], m[72:80], m[80:96] = np.arange(72, 80), np.arange(88, 96), np.arange(48, 64)
    return m


def _head_lane_map_c():
    return np.concatenate([np.arange(C_DH), np.arange(C_DH)])


def _sigmoid(x):
    return 1.0 / (1.0 + jnp.exp(-x))


def _log_sigmoid(x):
    return jnp.minimum(x, 0.0) - jnp.log(1.0 + jnp.exp(-jnp.abs(x)))


def _rms(x, g):
    ms = jnp.mean(x * x, axis=-1, keepdims=True)
    return x * lax.rsqrt(ms + EPS) * g


def _dot(a, b):
    return jnp.dot(a, b, preferred_element_type=f32)


def _dot_nt(a, b):
    return lax.dot_general(a, b, (((1,), (1,)), ((), ())), preferred_element_type=f32)


def _split3(x):
    hi = x.astype(bf16)
    r1 = x - hi.astype(f32)
    mid = r1.astype(bf16)
    return hi, mid, (r1 - mid.astype(f32)).astype(bf16)


def _layer(arr, l):
    rest = (0,) * (arr.ndim - 1)
    return pl.BlockSpec((None,) + arr.shape[1:], lambda *_: (l,) + rest, pipeline_mode=pl.Buffered(1))


def _mod_spec(l, tm):
    tpb = T // tm
    return pl.BlockSpec((None, 1, N_MOD, D), lambda i: (l, i // tpb, 0, 0))


def _params(n_axes):
    return pltpu.CompilerParams(dimension_semantics=("arbitrary",) * n_axes, vmem_limit_bytes=VMEM_LIMIT)


def _ada_kernel(c_ref, w_ref, b_ref, o_ref):
    c = c_ref[...]
    s = c * _sigmoid(c)
    pieces = _split3(s)
    s3 = jnp.concatenate([piece.astype(f32) for piece in pieces], axis=0).astype(bf16)
    w = w_ref[0]
    w_hi = w.astype(bf16)
    w_lo = (w - w_hi.astype(f32)).astype(bf16)
    r = _dot(s3, w_hi)
    o_ref[0] = (r[0:MOD_ROWS] + r[MOD_ROWS:2 * MOD_ROWS] + r[2 * MOD_ROWS:]
                + _dot(pieces[0], w_lo) + b_ref[0])


def _ada(cc, ada_w, ada_b):
    nt = (N_MOD * D) // ADA_TN
    return pl.pallas_call(
        _ada_kernel,
        grid=(DEPTH, nt),
        in_specs=[
            pl.BlockSpec((MOD_ROWS, D), lambda l, j: (0, 0)),
            pl.BlockSpec((1, D, ADA_TN), lambda l, j: (l, 0, j)),
            pl.BlockSpec((1, 1, ADA_TN), lambda l, j: (l, 0, j)),
        ],
        out_specs=pl.BlockSpec((1, MOD_ROWS, ADA_TN), lambda l, j: (l, 0, j)),
        out_shape=jax.ShapeDtypeStruct((DEPTH, MOD_ROWS, N_MOD * D), f32),
        compiler_params=_params(2),
        name="ada_mod",
    )(cc, ada_w, ada_b.reshape(DEPTH, 1, N_MOD * D))


def _ffn(x, g, shift, scale, gate, wi_ref, wo_ref):
    h = (_rms(x, g) * (1.0 + scale) + shift).astype(bf16)
    acc = None
    for c in range(D_FF // FF_CHUNK):
        lo, hi = c * FF_CHUNK, (c + 1) * FF_CHUNK
        gt = _dot(h, wi_ref[:, lo:hi])
        up = _dot(h, wi_ref[:, D_FF + lo:D_FF + hi])
        a = (gt * _sigmoid(gt) * up).astype(bf16)
        part = _dot(a, wo_ref[lo:hi, :])
        acc = part if acc is None else acc + part
    return x + HALF * gate * acc


def _ffn1_kernel(*refs, split_input):
    if split_input:
        xl_ref, xc_ref, mod_ref, ng_ref, wi_ref, wo_ref, x1_ref, h_ref = refs
        x = jnp.where(pl.program_id(0) < NL // TM_FFN, xl_ref[...], xc_ref[...])
    else:
        x_ref, mod_ref, ng_ref, wi_ref, wo_ref, x1_ref, h_ref = refs
        x = x_ref[...]
    mod = mod_ref[0]
    x1 = _ffn(x, ng_ref[0:1, :], mod[0:1, :], mod[1:2, :], mod[2:3, :], wi_ref, wo_ref)
    x1_ref[...] = x1
    h_ref[...] = (_rms(x1, ng_ref[1:2, :]) * (1.0 + mod[4:5, :]) + mod[3:4, :]).astype(bf16)


def _ffn1(l, xs, mod, ng, wi, wo):
    split_input = isinstance(xs, tuple)
    nlt = NL // TM_FFN
    if split_input:
        assert NCX % TM_FFN == 0
        x_specs = [pl.BlockSpec((TM_FFN, D), lambda i: (jnp.minimum(i, nlt - 1), 0)),
                   pl.BlockSpec((TM_FFN, D), lambda i: (jnp.maximum(i - nlt, 0), 0))]
    else:
        xs = (xs,)
        x_specs = [pl.BlockSpec((TM_FFN, D), lambda i: (i, 0))]
    return pl.pallas_call(
        functools.partial(_ffn1_kernel, split_input=split_input),
        grid=(N // TM_FFN,),
        in_specs=x_specs + [_mod_spec(l, TM_FFN), _layer(ng, l), _layer(wi, l), _layer(wo, l)],
        out_specs=[pl.BlockSpec((TM_FFN, D), lambda i: (i, 0))] * 2,
        out_shape=[jax.ShapeDtypeStruct((N, D), f32), jax.ShapeDtypeStruct((N, D), bf16)],
        compiler_params=_params(1),
        name="ffn1",
    )(*xs, mod, ng, wi, wo)


def _outproj_ffn_kernel(*refs, with_ctx):
    if with_ctx:
        (x_ref, mod_ref, ng_ref, hf_ref, hb_ref, o_ref, ybl_ref, ycl_ref, ybc_ref, ycc_ref, on_ref, wout_ref,
         wi_ref, wo_ref, out_ref) = refs
        is_latent = pl.program_id(0) < NL // TM_FFN
        yb = jnp.where(is_latent, ybl_ref[...], ybc_ref[...])
        yc = jnp.where(is_latent, ycl_ref[...], ycc_ref[...])
    else:
        (x_ref, mod_ref, ng_ref, hf_ref, hb_ref, o_ref, yb_ref, yc_ref, on_ref, wout_ref,
         wi_ref, wo_ref, out_ref) = refs
        yb, yc = yb_ref[...], yc_ref[...]
    mod = mod_ref[0]
    hs = hf_ref[...] + hb_ref[...]
    sq = hs * hs
    head = lax.broadcasted_iota(jnp.int32, (1, A_HEADS * A_DV), 1) // A_DV
    ms = jnp.zeros_like(hs)
    for hh in range(A_HEADS):
        sel = head == hh
        ssh = jnp.sum(jnp.where(sel, sq, 0.0), axis=-1, keepdims=True) * (1.0 / A_DV)
        ms = jnp.where(sel, ssh, ms)
    ya = _sigmoid(o_ref[...]) * (hs * lax.rsqrt(ms + EPS) * on_ref[...])
    y = jnp.concatenate([ya.astype(bf16), yb, yc], axis=-1)
    x2 = x_ref[...] + mod[5:6, :] * _dot(y, wout_ref[...])
    out_ref[...] = _ffn(x2, ng_ref[2:3, :], mod[6:7, :], mod[7:8, :], mod[8:9, :], wi_ref, wo_ref)


def _outproj_ffn(l, x1, mod, ng, hf, hb, pa, y_lat, y_ctx, onorm, wout, wi, wo):
    with_ctx = y_ctx is not None
    rows = N if with_ctx else NL
    nlt = NL // TM_FFN
    row = lambda w, c=0: pl.BlockSpec((TM_FFN, w), lambda i: (i, c))
    lat = lambda w: pl.BlockSpec((TM_FFN, w), lambda i: (jnp.minimum(i, nlt - 1), 0))
    ctx = lambda w: pl.BlockSpec((TM_FFN, w), lambda i: (jnp.maximum(i - nlt, 0), 0))
    y_specs = [lat(B_HEADS * B_DV), lat(C_HEADS * C_DH)]
    if with_ctx:
        y_specs += [ctx(B_HEADS * B_DV), ctx(C_HEADS * C_DH)]
    return pl.pallas_call(
        functools.partial(_outproj_ffn_kernel, with_ctx=with_ctx),
        grid=(rows // TM_FFN,),
        in_specs=[
            row(D), _mod_spec(l, TM_FFN), _layer(ng, l),
            row(A_HEADS * A_DV), row(A_HEADS * A_DV), row(A_HEADS * A_DV, PA_O // (A_HEADS * A_DV)),
        ] + y_specs + [_layer(onorm, l), _layer(wout, l), _layer(wi, l), _layer(wo, l)],
        out_specs=row(D),
        out_shape=jax.ShapeDtypeStruct((rows, D), f32),
        compiler_params=_params(1),
        name="outproj_ffn2",
    )(x1, mod, ng, hf, hb, pa, *y_lat, *(y_ctx or ()), onorm, wout, wi, wo)


def _inproj_kernel(h_ref, win_ref, bias_ref, tab_ref, cqn_ref, ckvn_ref, wuq_ref, wukv_ref, hg_ref,
                   pa_ref, kt_ref, gt_ref, qb_ref, kb_ref, vb_ref, qc_ref, kc_ref, vc_ref):
    h = h_ref[...]
    p = _dot(h, win_ref[...]) + bias_ref[...]

    pa_ref[:, PA_V:PA_Q] = p[:, COL_AV:COL_AG]
    pa_ref[:, PA_Q:PA_G] = p[:, COL_AQ:COL_AV] * (A_DK ** -0.5)
    graw = p[:, COL_AG:COL_AG + LANES]
    lane = lax.broadcasted_iota(jnp.int32, (1, LANES), 1)
    for kk in range(4):
        gk = graw if kk == 0 else pltpu.roll(graw, LANES - A_HEADS * kk, 1)
        if kk % 2 == 1:
            gk = _log_sigmoid(gk)
        pa_ref[:, PA_G + LANES * kk:PA_G + LANES * (kk + 1)] = jnp.where(lane < A_HEADS, gk, 0.0)

    grow = lax.broadcasted_iota(jnp.int32, (2 * N_GATES, 1), 0)
    is_forget = ((grow // A_HEADS) % 2 == 1) == (grow < N_GATES)
    gate = grow % N_GATES
    source_lane = jnp.where(grow < N_GATES, gate, ((gate // A_HEADS) ^ 1) * A_HEADS + gate % A_HEADS)
    pick_gates = (lane == source_lane).astype(bf16)
    eye = (lax.broadcasted_iota(jnp.int32, (LANES, LANES), 0)
           == lax.broadcasted_iota(jnp.int32, (LANES, LANES), 1)).astype(bf16)
    keys = p[:, COL_AK:COL_AK + LANES].astype(bf16)
    gate_pieces = _split3(graw)
    for c in range(TM_IN // A_CHUNK):
        rows = slice(c * A_CHUNK, (c + 1) * A_CHUNK)
        kt_ref[c] = _dot_nt(eye, keys[rows, :])
        g = sum(_dot_nt(pick_gates, piece[rows, :]) for piece in gate_pieces)
        gt_ref[c] = jnp.where(is_forget, _log_sigmoid(g), g)

    is_latent = pl.program_id(0) < NL // TM_IN
    tab = tab_ref[...]
    tb = [jnp.where(is_latent, tab[:, LANES * i:LANES * (i + 1)], 1.0 - (i % 2)) for i in range(4)]
    hg = hg_ref[...]

    cq = _rms(p[:, COL_BCQ:COL_BCKV], cqn_ref[...]).astype(bf16)
    ckv = _rms(p[:, COL_BCKV:COL_BKR], ckvn_ref[...]).astype(bf16)
    kr = p[:, COL_BKR:COL_CQ]
    q = _dot(cq, wuq_ref[...])
    kv = _dot(ckv, wukv_ref[...])
    vb_ref[...] = kv[:, B_HEADS * LANES:].astype(bf16)
    vc_ref[...] = p[:, COL_CV:COL_CV + LANES].astype(bf16)

    two = lambda a: jnp.concatenate([a, a], axis=1)
    pair = lambda a, j, first=0: a[:, first + 2 * LANES * j:first + 2 * LANES * (j + 1)]
    cos_b, sin_b, cos_c, sin_c = (two(t) for t in tb)
    kr2 = two(kr)
    ri = lax.broadcasted_iota(jnp.int32, (2 * LANES, 2 * LANES), 0)
    ci = lax.broadcasted_iota(jnp.int32, (2 * LANES, 2 * LANES), 1)
    swap_b = ((ri // LANES == ci // LANES) & (ri % LANES == (ci + HALF_LANES) % LANES)).astype(bf16)
    half_c = C_DH // 2
    swap_c = ((ri // half_c == ci // half_c) & (ri % half_c == (ci + half_c // 2) % half_c)).astype(bf16)
    full = slice(0, 2 * LANES)
    jobs = []
    for j in range(B_HEADS // 2):
        dst = slice(2 * LANES * j, 2 * LANES * (j + 1))
        jobs.append((pair(q, j), hg[0:1, :], LANES, B_DQK, cos_b, sin_b, swap_b, [(qb_ref, dst, full)]))
        jobs.append((pair(kv, j) + kr2, hg[1:2, :], LANES, B_DQK, cos_b, sin_b, swap_b, [(kb_ref, dst, full)]))
    jobs.append((pair(p, 0, COL_CQ), hg[2:3, :], C_DH, C_DH, cos_c, sin_c, swap_c, [(qc_ref, full, full)]))
    jobs.append((pair(p, 1, COL_CQ), hg[3:4, :], C_DH, C_DH, cos_c, sin_c, swap_c,
                 [(qc_ref, slice(2 * LANES, 3 * LANES), slice(0, LANES)),
                  (kc_ref, slice(0, LANES), slice(LANES, 2 * LANES))]))
    gained = [x * gain for x, gain, *_ in jobs]
    rolled = [_dot(xg.astype(bf16), job[6]) for job, xg in zip(jobs, gained)]
    lane = lax.broadcasted_iota(jnp.int32, (1, LANES), 1)
    lane2 = lax.broadcasted_iota(jnp.int32, (1, 2 * LANES), 1)
    sums = []
    for x, _, width, *_ in jobs:
        sq = x * x
        parts = []
        for t in range(2):
            blk = sq[:, LANES * t:LANES * (t + 1)]
            if width == LANES:
                parts.append(jnp.sum(blk, axis=-1, keepdims=True))
            else:
                parts.append(jnp.sum(jnp.where(lane < width, blk, 0.0), axis=-1, keepdims=True))
                parts.append(jnp.sum(jnp.where(lane < width, 0.0, blk), axis=-1, keepdims=True))
        ss = parts[-1]
        for k in range(len(parts) - 2, -1, -1):
            ss = jnp.where(lane2 < (k + 1) * width, parts[k], ss)
        sums.append(ss)
    for (_, _, _, n_real, cos, sin, _, dests), xg, xr, ss in zip(jobs, gained, rolled, sums):
        out = ((xg * cos + xr * sin) * lax.rsqrt(ss * (1.0 / n_real) + EPS)).astype(bf16)
        for dst_ref, dst_cols, src_cols in dests:
            dst_ref[:, dst_cols] = out[:, src_cols]


def _inproj(l, h, tab, win, bias, cqn, ckvn, wuq, wukv, hg):
    tpb = T // TM_IN
    row = lambda w: pl.BlockSpec((TM_IN, w), lambda i: (i, 0))
    chunked = lambda r: pl.BlockSpec((TM_IN // A_CHUNK, r, A_CHUNK), lambda i: (i, 0, 0))
    out_w = [(B_HEADS * LANES, bf16), (B_HEADS * LANES, bf16), (B_HEADS * B_DV, bf16),
             (C_HEADS * C_DH, bf16), (C_KV_HEADS * C_DH, bf16), (C_KV_HEADS * C_DH, bf16)]
    return pl.pallas_call(
        _inproj_kernel,
        grid=(N // TM_IN,),
        in_specs=[
            row(D), _layer(win, l), _layer(bias, l),
            pl.BlockSpec((TM_IN, 4 * LANES), lambda i: (jnp.where(i < NL // TM_IN, i % tpb, 0), 0)),
            _layer(cqn, l), _layer(ckvn, l), _layer(wuq, l), _layer(wukv, l),
            _layer(hg, l),
        ],
        out_specs=[row(PA_W), chunked(LANES), chunked(2 * N_GATES)] + [row(w) for w, _ in out_w],
        out_shape=[jax.ShapeDtypeStruct((N, PA_W), f32),
                   jax.ShapeDtypeStruct((N // A_CHUNK, LANES, A_CHUNK), f32),
                   jax.ShapeDtypeStruct((N // A_CHUNK, 2 * N_GATES, A_CHUNK), f32)]
        + [jax.ShapeDtypeStruct((N, w), dt) for w, dt in out_w],
        compiler_params=_params(1),
        name="inproj",
    )(h, win, bias, tab, cqn, ckvn, wuq, wukv, hg)


def _cummax_rows(x, rev):
    n = x.shape[0]
    row = lax.broadcasted_iota(jnp.int32, (n, 1), 0)
    sh = 1
    while sh < n:
        if rev:
            x = jnp.maximum(x, jnp.where(row < n - sh, pltpu.roll(x, n - sh, 0), -jnp.inf))
        else:
            x = jnp.maximum(x, jnp.where(row >= sh, pltpu.roll(x, sh, 0), -jnp.inf))
        sh *= 2
    return x


def _stack_heads(pieces):
    return jnp.concatenate(pieces, axis=0)


def _mlstm_kernel(*refs):
    streams = [(refs[0:7] + refs[14:15], False), (refs[7:14] + refs[15:16], True)]
    s_ref, ml_ref, ms_ref = refs[16:19]

    @pl.when(pl.program_id(1) == 0)
    def _():
        s_ref[...] = jnp.zeros_like(s_ref)
        ml_ref[...] = jnp.zeros_like(ml_ref)
        ms_ref[...] = jnp.zeros_like(ms_ref)

    L = A_CHUNK
    n_chunks = MLSTM_R // L
    heads = range(A_HEADS)
    ti = lax.broadcasted_iota(jnp.int32, (L, L), 0)
    si = lax.broadcasted_iota(jnp.int32, (L, L), 1)
    lane = lax.broadcasted_iota(jnp.int32, (1, LANES), 1)
    row8 = lax.broadcasted_iota(jnp.int32, (2 * A_HEADS, 1), 0)
    in_head = [(lane >= hh * A_DK) & (lane < (hh + 1) * A_DK) for hh in heads]
    ones_blk = jnp.ones((L, LANES), bf16)

    items = []
    for sidx, (srefs, rev) in enumerate(streams):
        q_ref, kt_ref, v_ref, ig_ref, lf_ref, gt_ref, gts_ref, h_ref = srefs
        attend = (si >= ti) if rev else (si <= ti)
        attend4 = _stack_heads([attend] * A_HEADS)
        cum_cols = attend.astype(bf16)
        cum_rows = ((ti >= si) if rev else (ti <= si)).astype(bf16)
        m_lane = ml_ref[sidx, 0:1, :]
        m_sub = ms_ref[sidx, :, 0:1]
        for cc in (range(n_chunks - 1, -1, -1) if rev else range(n_chunks)):
            rows = slice(cc * L, (cc + 1) * L)
            ig = ig_ref[rows, :]
            lf = lf_ref[rows, :]
            gt = gt_ref[cc]
            gts = gts_ref[cc]
            b_col = sum(_dot(cum_cols, piece) for piece in _split3(lf))
            r_col = ig - b_col
            big_m = jnp.maximum(m_lane, _cummax_rows(r_col, rev))
            mt_col = b_col + big_m
            b_last_l = jnp.sum(lf, axis=0, keepdims=True)
            m_new_l = jnp.maximum(m_lane, jnp.max(r_col, axis=0, keepdims=True)) + b_last_l
            b_rows = sum(_dot(piece, cum_rows) for piece in _split3(gts))
            live = row8 < A_HEADS
            r8 = jnp.where(live, gt - b_rows, 0.0)
            b_last_s = jnp.where(live, jnp.sum(gts, axis=-1, keepdims=True), 0.0)
            r_max_s = jnp.max(r8, axis=-1, keepdims=True)
            wg8 = jnp.exp(r8 - r_max_s)
            m_new_s = jnp.maximum(m_sub, r_max_s) + b_last_s
            decay_s = jnp.exp(b_last_s + m_sub - m_new_s)
            scale_s = jnp.exp(b_last_s + r_max_s - m_new_s)
            expand = lambda a, n: _stack_heads([jnp.broadcast_to(a[hh:hh + 1, :], (n, a.shape[1])) for hh in heads])
            q = q_ref[rows, :]
            big_m_b = _stack_heads([jnp.broadcast_to(big_m[:, hh:hh + 1], (L, LANES)) for hh in heads])
            mt_b = _stack_heads([jnp.broadcast_to(mt_col[:, hh:hh + 1], (L, LANES)) for hh in heads])
            m_old_b = _stack_heads([jnp.broadcast_to(m_lane[:, hh:hh + 1], (L, LANES)) for hh in heads])
            q_stack = _stack_heads([jnp.where(in_head[hh], q, 0.0) for hh in heads])
            items.append(dict(
                sidx=sidx, rows=rows, h_ref=h_ref,
                qst=q_stack.astype(bf16),
                qa=(q_stack * jnp.exp(m_old_b - big_m_b)).astype(bf16),
                kt=kt_ref[cc].astype(bf16),
                kw=(kt_ref[cc] * expand(wg8, A_DK)).astype(bf16),
                vo=jnp.concatenate([v_ref[rows, :].astype(bf16), ones_blk], axis=1),
                w=jnp.exp(jnp.where(attend4, expand(r8, L) - big_m_b[:, 0:L], -jnp.inf)),
                floor=jnp.exp(-mt_b),
                decay=jnp.broadcast_to(expand(decay_s, A_DK), (LANES, LANES)),
                kv_scale=jnp.broadcast_to(expand(scale_s, A_DK), (LANES, LANES))))
            m_lane, m_sub = m_new_l, m_new_s
        ml_ref[sidx, 0:1, :] = m_lane
        ms_ref[sidx, :, 0:1] = m_sub

    for it in items:
        it["s"] = _dot(it["qst"], it["kt"])

    for it in items:
        it["kv"] = _dot(it["kw"], it["vo"])

    for it in items:
        it["p"] = (it["s"] * it["w"]).astype(bf16)

    state = [s_ref[0], s_ref[1]]
    tile3 = lambda a: jnp.concatenate([a] * 3, axis=1)
    for it in items:
        st = state[it["sidx"]]
        it["c_in"] = st.astype(bf16)
        state[it["sidx"]] = tile3(it["decay"]) * st + tile3(it["kv_scale"]) * it["kv"]
    s_ref[0] = state[0]
    s_ref[1] = state[1]

    nv = A_HEADS * A_DV
    for it in items:
        out = _dot(jnp.concatenate([it["qa"], it["p"]], axis=1),
                   jnp.concatenate([it["c_in"], it["vo"]], axis=0))
        den = jnp.maximum(jnp.abs(out[:, nv:]), it["floor"])
        for pair in range(A_HEADS // 2):
            sl = slice(LANES * pair, LANES * (pair + 1))
            even, odd = (slice(L * hh, L * (hh + 1)) for hh in (2 * pair, 2 * pair + 1))
            it["h_ref"][it["rows"], sl] = jnp.where(lane < A_DV, out[even, sl] / den[even, :], out[odd, sl] / den[odd, :])


def _mlstm(pa, kt, gt):
    nb = T // MLSTM_R
    nc = MLSTM_R // A_CHUNK
    assert CTX == MLSTM_R

    def rb(rev):
        def f(b, j):
            jj = j - 1
            return jnp.where(j == 0, NL // MLSTM_R + b, b * nb + (nb - 1 - jj if rev else jj))
        return f

    def stream_specs(rev):
        r = rb(rev)
        d = 2 if rev else 0
        col = lambda w, c: pl.BlockSpec((MLSTM_R, w), lambda b, j: (r(b, j), c))
        return [
            col(LANES, PA_Q // LANES),
            pl.BlockSpec((nc, LANES, A_CHUNK), lambda b, j: (r(b, j), 0, 0)),
            col(2 * LANES, PA_V // (2 * LANES)),
            col(LANES, PA_G // LANES + d), col(LANES, PA_G // LANES + d + 1),
            pl.BlockSpec((nc, 2 * A_HEADS, A_CHUNK), lambda b, j: (r(b, j), d // 2, 0)),
            pl.BlockSpec((nc, 2 * A_HEADS, A_CHUNK), lambda b, j: (r(b, j), 2 + d // 2, 0)),
        ]

    out_spec = lambda rev: pl.BlockSpec((MLSTM_R, A_HEADS * A_DV), lambda b, j: (rb(rev)(b, j), 0))
    return pl.pallas_call(
        _mlstm_kernel,
        grid=(B, nb + 1),
        in_specs=stream_specs(False) + stream_specs(True),
        out_specs=[out_spec(False), out_spec(True)],
        out_shape=[jax.ShapeDtypeStruct((N, A_HEADS * A_DV), f32)] * 2,
        scratch_shapes=[pltpu.VMEM((2, LANES, 3 * LANES), f32), pltpu.VMEM((2, 8, LANES), f32),
                        pltpu.VMEM((2, 8, LANES), f32)],
        compiler_params=_params(2),
        name="mlstm",
    )(*[pa, kt, pa, pa, pa, gt, gt] * 2)


def _tile_max(s, m128):
    for t in range(s.shape[1] // LANES):
        blk = s[:, LANES * t:LANES * (t + 1)]
        m128 = blk if m128 is None else jnp.maximum(m128, blk)
    return m128


def _mla_kernel(*refs, latent):
    if latent:
        q_ref, kc_ref, vc_ref, kl_ref, vl_ref, o_ref, s_ref = refs
        sources = [(kc_ref, vc_ref, 0, CTX)] + [(kl_ref, vl_ref, c, MLA_KC) for c in range(0, T, MLA_KC)]
    else:
        q_ref, kc_ref, vc_ref, o_ref, s_ref = refs
        sources = [(kc_ref, vc_ref, 0, CTX)]
    q = q_ref[...]
    lane = lax.broadcasted_iota(jnp.int32, (1, LANES), 1)
    row_max = []
    for hh in range(2):
        sl = slice(LANES * hh, LANES * (hh + 1))
        qh = q[:, sl]
        m128 = None
        off = 0
        for k_ref, _, r0, n in sources:
            s = _dot_nt(qh, k_ref[r0:r0 + n, sl])
            s_ref[hh, :, off:off + n] = s
            m128 = _tile_max(s, m128)
            off += n
        row_max.append(jnp.max(m128, axis=-1, keepdims=True))
    outs = []
    for hh in range(2):
        den_lane = B_DV if hh == 0 else 0
        acc = None
        off = 0
        for _, v_ref, r0, n in sources:
            p = jnp.exp2(s_ref[hh, :, off:off + n] - row_max[hh]).astype(bf16)
            vext = jnp.where(lane == den_lane, 1.0, v_ref[r0:r0 + n, :]).astype(bf16)
            part = _dot(p, vext)
            acc = part if acc is None else acc + part
            off += n
        outs.append(acc / acc[:, den_lane:den_lane + 1])
    o_ref[...] = jnp.where(lane < B_DV, outs[0], outs[1]).astype(o_ref.dtype)


def _mla(qb, kb, vb, latent):
    npair = B_HEADS // 2
    ctx_blk = NL // CTX
    kv_specs = [
        pl.BlockSpec((CTX, 2 * LANES), lambda b, p, i: (ctx_blk + b, p)),
        pl.BlockSpec((CTX, LANES), lambda b, p, i: (ctx_blk + b, p)),
    ]
    if latent:
        tq = MLA_TQ
        nq = T // tq
        qmap = omap = lambda b, p, i: (b * nq + i, p)
        kv_specs += [
            pl.BlockSpec((T, 2 * LANES), lambda b, p, i: (b, p)),
            pl.BlockSpec((T, LANES), lambda b, p, i: (b, p)),
        ]
        args = (qb, kb, vb, kb, vb)
        nkeys = CTX + T
    else:
        tq = CTX
        nq = 1
        qmap = lambda b, p, i: (ctx_blk + b, p)
        omap = lambda b, p, i: (b, p)
        args = (qb, kb, vb)
        nkeys = CTX
    return pl.pallas_call(
        functools.partial(_mla_kernel, latent=latent),
        grid=(B, npair, nq),
        in_specs=[pl.BlockSpec((tq, 2 * LANES), qmap)] + kv_specs,
        out_specs=pl.BlockSpec((tq, LANES), omap),
        out_shape=jax.ShapeDtypeStruct((NL if latent else NCX, B_HEADS * B_DV), bf16),
        scratch_shapes=[pltpu.VMEM((2, tq, nkeys), f32)],
        compiler_params=_params(3),
        name="mla_latent" if latent else "mla_context",
    )(*args)


def _gqa_kernel(sink_ref, *refs, latent):
    if latent:
        q_ref, kc_ref, vc_ref, kl_ref, vl_ref, o_ref = refs
    else:
        q_ref, kc_ref, vc_ref, o_ref = refs
    q = q_ref[...]
    tq = q.shape[0]
    lane = lax.broadcasted_iota(jnp.int32, (1, LANES), 1)
    keys = kc_ref[...]
    vals = vc_ref[...]
    valid = None
    if latent:
        n = pl.program_id(1)
        start = pl.multiple_of(jnp.clip(n * GQA_TQ - WINDOW, 0, T - GQA_BAND), WINDOW)
        keys = jnp.concatenate([keys, kl_ref[pl.ds(start, GQA_BAND), :]], axis=0)
        vals = jnp.concatenate([vals, vl_ref[pl.ds(start, GQA_BAND), :]], axis=0)
        qpos = n * GQA_TQ + lax.broadcasted_iota(jnp.int32, (tq, 1), 0)
        kidx = lax.broadcasted_iota(jnp.int32, (1, CTX + GQA_BAND), 1)
        valid = (kidx < CTX) | (jnp.abs(qpos - (start - CTX + kidx)) <= WINDOW)
    lo = lane < C_DH
    outs = [[], []]
    scores = []
    for kvh in range(C_KV_HEADS):
        mine = lo if kvh == 0 else ~lo
        qs = jnp.concatenate([jnp.where(mine, q[:, LANES * g:LANES * (g + 1)], 0) for g in range(C_GROUP)], axis=0)
        scores.append(_dot_nt(qs, keys))
    for kvh, s_all in enumerate(scores):
        den_lane = C_DH * (1 - kvh)
        vext = jnp.where(lane == den_lane, 1.0, vals).astype(bf16)
        for g in range(C_GROUP):
            s = s_all[g * tq:(g + 1) * tq, :]
            if latent:
                s = jnp.where(valid, s, -jnp.inf)
            sink = sink_ref[C_GROUP * kvh + g] * LOG2E
            m = jnp.maximum(sink, jnp.max(_tile_max(s, None), axis=-1, keepdims=True))
            acc = _dot(jnp.exp2(s - m).astype(bf16), vext)
            outs[kvh].append(acc / (jnp.exp2(sink - m) + acc[:, den_lane:den_lane + 1]))
    for g in range(C_GROUP):
        o_ref[:, LANES * g:LANES * (g + 1)] = jnp.where(lo, outs[0][g], outs[1][g]).astype(o_ref.dtype)


def _gqa(sink, qc, kc, vc, latent):
    ctx_blk = NL // CTX
    kv_specs = [
        pl.BlockSpec((CTX, LANES), lambda b, i: (ctx_blk + b, 0)),
        pl.BlockSpec((CTX, LANES), lambda b, i: (ctx_blk + b, 0)),
    ]
    if latent:
        tq = GQA_TQ
        nq = T // tq
        qmap = omap = lambda b, i: (b * nq + i, 0)
        kv_specs += [
            pl.BlockSpec((T, LANES), lambda b, i: (b, 0)),
            pl.BlockSpec((T, LANES), lambda b, i: (b, 0)),
        ]
        args = (sink, qc, kc, vc, kc, vc)
    else:
        tq = CTX
        nq = 1
        qmap = lambda b, i: (ctx_blk + b, 0)
        omap = lambda b, i: (b, 0)
        args = (sink, qc, kc, vc)
    return pl.pallas_call(
        functools.partial(_gqa_kernel, latent=latent),
        grid=(B, nq),
        in_specs=[pl.BlockSpec(memory_space=pltpu.SMEM), pl.BlockSpec((tq, C_HEADS * C_DH), qmap)] + kv_specs,
        out_specs=pl.BlockSpec((tq, C_HEADS * C_DH), omap),
        out_shape=jax.ShapeDtypeStruct((NL if latent else NCX, C_HEADS * C_DH), bf16),
        compiler_params=_params(2),
        name="gqa_latent" if latent else "gqa_context",
    )(*args)


def _pad_cols(w, width):
    return jnp.pad(w, ((0, 0), (0, width - w.shape[1])))


def _lane_runs(lane_map):
    runs = []
    for src in lane_map:
        src = int(src)
        if runs and ((src < 0 and runs[-1][0] < 0) or (src >= 0 and runs[-1][0] >= 0 and src == sum(runs[-1]))):
            runs[-1] = (runs[-1][0], runs[-1][1] + 1)
        else:
            runs.append((src, 1))
    return runs


def _place(w, lane_map):
    parts = [jnp.zeros(w.shape[:-1] + (n,), w.dtype) if s < 0 else w[..., s:s + n] for s, n in _lane_runs(lane_map)]
    return jnp.concatenate(parts, axis=-1)


def _place_heads(w, heads, lane_map):
    r = w.shape[0]
    return _place(w.reshape(r, heads, -1), lane_map).reshape(r, heads * LANES)


def _reorder_c_heads(w, axis):
    heads = jnp.split(w, C_HEADS, axis=axis)
    return jnp.concatenate([heads[h] for h in C_HEAD_ORDER], axis=axis)


IN_SIZES = (A_HEADS * A_DK, A_HEADS * A_DK, A_HEADS * A_DV, A_HEADS * A_DV, N_GATES,
            B_Q_RANK, B_KV_RANK, B_ROPE, C_HEADS * C_DH, C_KV_HEADS * C_DH, C_KV_HEADS * C_DH)
IN_OFFSETS = tuple(int(v) for v in np.cumsum((0,) + IN_SIZES))


def _in_part(w, i):
    return w[..., IN_OFFSETS[i]:IN_OFFSETS[i + 1]]


def _arrange_w_in(w, map_b):
    part = lambda i: _in_part(w, i)
    kr = _place(part(7), np.where(map_b >= B_NOPE, map_b - B_NOPE, -1))
    return jnp.concatenate([
        part(0), part(2), part(3), _pad_cols(part(4), LANES),
        part(5), part(6), kr,
        _reorder_c_heads(part(8), 1), part(9), part(10), part(1),
    ], axis=1)


def _rope_tables(map_b, map_c):
    assert T == GRID_W * GRID_W
    pos = jnp.arange(GRID_W, dtype=f32)[:, None]
    small, by_col = [], []
    for lane_map, rope_start, half in ((map_b, B_NOPE, B_ROPE // 4), (map_c, 0, C_DH // 4)):
        rel = lane_map - rope_start
        in_rope = (lane_map >= 0) & (rel >= 0) & (rel < 4 * half)
        rel = np.where(in_rope, rel, 0)
        second = jnp.asarray((rel // half) % 2 == 1)[None, :]
        freq = ROPE_BASE ** (-jnp.asarray(rel % half, f32) / half)
        rot = jnp.asarray(in_rope)[None, :]
        ang = pos * freq[None, :]
        sin = jnp.sin(ang)
        small += [jnp.where(rot, jnp.cos(ang), 1.0), jnp.where(rot, jnp.where(second, sin, -sin), 0.0)]
        by_col += [rel >= 2 * half] * 2
    small = jnp.concatenate(small, axis=1)
    by_col = jnp.asarray(np.concatenate(by_col))[None, None, :]
    shape = (GRID_W, GRID_W, small.shape[1])
    full = jnp.where(by_col, jnp.broadcast_to(small[None], shape), jnp.broadcast_to(small[:, None], shape))
    return full.reshape(T, small.shape[1])


def kernel(x, c, ctx, c_ctx, ada_w, ada_b, norm_g, ffn1_wi, ffn1_wo, ffn2_wi, ffn2_wo, w_in, w_out,
           mlstm_gate_b, mlstm_out_norm, mla_cq_norm, mla_ckv_norm, mla_w_uq, mla_w_ukv, mla_q_norm, mla_k_norm,
           gqa_q_norm, gqa_k_norm, gqa_sink):
    map_b, map_c = _head_lane_map_b(), _head_lane_map_c()
    nope_map = np.where(map_b < B_NOPE, map_b, -1)
    q_scale_b, q_scale_c = B_DQK ** -0.5 * LOG2E, C_DH ** -0.5 * LOG2E

    def arrange(w_in_l, gate_b_l, w_uq_l, w_ukv_l, bq_l, bk_l, cq_l, ck_l):
        ukv = w_ukv_l.reshape(B_KV_RANK, B_HEADS, B_NOPE + B_DV)
        gbq, gbk = _place(bq_l[None], map_b) * q_scale_b, _place(bk_l[None], map_b)
        gcq, gck = _place(cq_l[None], map_c) * q_scale_c, _place(ck_l[None], map_c)
        return dict(
            win=_arrange_w_in(w_in_l, map_b).astype(bf16),
            bias=_pad_cols(jnp.pad(gate_b_l[None], ((0, 0), (COL_AG, 0))), WP),
            wuq=_place_heads(w_uq_l, B_HEADS, map_b).astype(bf16),
            wukv=jnp.concatenate([_place_heads(ukv[:, :, :B_NOPE].reshape(B_KV_RANK, -1), B_HEADS, nope_map),
                                  ukv[:, :, B_NOPE:].reshape(B_KV_RANK, -1)], axis=1).astype(bf16),
            hg=jnp.concatenate([jnp.concatenate(pair, axis=1)
                                for pair in ((gbq, gbq), (gbk, gbk), (gcq, gcq), (gcq, gck))]))

    pw = jax.vmap(arrange)(w_in, mlstm_gate_b, mla_w_uq, mla_w_ukv, mla_q_norm, mla_k_norm, gqa_q_norm, gqa_k_norm)
    wi1, wo1, wi2, wo2 = (w.astype(bf16) for w in (ffn1_wi, ffn1_wo, ffn2_wi, ffn2_wo))
    c_rows = A_HEADS * A_DV + B_HEADS * B_DV
    wout = jnp.concatenate([w_out[:, :c_rows], _reorder_c_heads(w_out[:, c_rows:], 1)], axis=1).astype(bf16)
    cqn, ckvn, onorm = mla_cq_norm[:, None, :], mla_ckv_norm[:, None, :], mlstm_out_norm[:, None, :]

    cc = jnp.concatenate([c, c_ctx[None, :], jnp.zeros((MOD_ROWS - B - 1, D), f32)], axis=0)
    mod = _ada(cc, ada_w, ada_b).reshape(DEPTH, MOD_ROWS, N_MOD, D)
    tab = _rope_tables(map_b, map_c)
    xs = (x.reshape(NL, D), ctx.reshape(NCX, D))

    for l in range(DEPTH):
        need_ctx = l < DEPTH - 1
        x1, h = _ffn1(l, xs, mod, norm_g, wi1, wo1)
        pa, kt, gt, qb, kb, vb, qc, kc, vc = _inproj(
            l, h, tab, pw["win"], pw["bias"], cqn, ckvn, pw["wuq"], pw["wukv"], pw["hg"])
        hf, hb = _mlstm(pa, kt, gt)
        y_lat = (_mla(qb, kb, vb, True), _gqa(gqa_sink[l], qc, kc, vc, True))
        y_ctx = (_mla(qb, kb, vb, False), _gqa(gqa_sink[l], qc, kc, vc, False)) if need_ctx else None
        xs = _outproj_ffn(l, x1, mod, norm_g, hf, hb, pa, y_lat, y_ctx, onorm, wout, wi2, wo2)
    return xs.reshape(B, T, D)
```

```python
import functools
import math

import jax
import jax.numpy as jnp
import numpy as np
from jax import lax
from jax.experimental import pallas as pl
from jax.experimental.pallas import tpu as pltpu

f32 = jnp.float32
bf16 = jnp.bfloat16

D = 1024
B = 4
T = 4096
CTX = 256
DEPTH = 2
GRID_W = 64
ROPE_BASE = 10000.0
EPS = 1e-6
HALF = 0.5
N_MOD = 9
D_FF = 2816
A_HEADS, A_DK, A_DV, A_CHUNK = 4, 32, 64, 64
B_HEADS, B_Q_RANK, B_KV_RANK, B_NOPE, B_ROPE, B_DV = 6, 256, 128, 64, 32, 64
B_DQK = B_NOPE + B_ROPE
C_HEADS, C_KV_HEADS, C_DH, WINDOW = 6, 2, 64, 128
C_GROUP = C_HEADS // C_KV_HEADS

NL = B * T
NCX = B * CTX
N = NL + NCX

LANES = 128
HALF_LANES = LANES // 2
MOD_ROWS = 8
VMEM_LIMIT = 56 * 1024 * 1024
LOG2E = math.log2(math.e)

TM_FFN = 512
FF_CHUNK = 256
TM_IN = 512
ADA_TN = 1152
MLSTM_R = 256
MLA_TQ = 1024
MLA_KC = 512
GQA_TQ = 256
GQA_BAND = GQA_TQ + 2 * WINDOW

COL_AQ, COL_AV, COL_AO, COL_AG = 0, 128, 384, 640
COL_BCQ, COL_BCKV, COL_BKR = 768, 1024, 1152
COL_CQ, COL_CK, COL_CV = 1280, 1664, 1792
COL_AK = 1920
WP = 2048
C_HEAD_ORDER = tuple(h for g in range(C_GROUP) for h in (g, C_GROUP + g))
PA_V, PA_O, PA_Q, PA_G = 0, 256, 512, 640
PA_W = PA_G + 4 * LANES
N_GATES = 4 * A_HEADS


def _head_lane_map_b():
    m = -np.ones(LANES, np.int64)
    m[0:8], m[8:16], m[16:64] = np.arange(64, 72), np.arange(80, 88), np.arange(0, 48)
    m[64:72], m[72:80], m[80:96] = np.arange(72, 80), np.arange(88, 96), np.arange(48, 64)
    return m


def _head_lane_map_c():
    return np.concatenate([np.arange(C_DH), np.arange(C_DH)])


def _sigmoid(x):
    return 1.0 / (1.0 + jnp.exp(-x))


def _log_sigmoid(x):
    return jnp.minimum(x, 0.0) - jnp.log(1.0 + jnp.exp(-jnp.abs(x)))


def _rms(x, g):
    ms = jnp.mean(x * x, axis=-1, keepdims=True)
    return x * lax.rsqrt(ms + EPS) * g


def _dot(a, b):
    return jnp.dot(a, b, preferred_element_type=f32)


def _dot_nt(a, b):
    return lax.dot_general(a, b, (((1,), (1,)), ((), ())), preferred_element_type=f32)


def _split3(x):
    hi = x.astype(bf16)
    r1 = x - hi.astype(f32)
    mid = r1.astype(bf16)
    return hi, mid, (r1 - mid.astype(f32)).astype(bf16)


def _layer(arr, l):
    rest = (0,) * (arr.ndim - 1)
    return pl.BlockSpec((None,) + arr.shape[1:], lambda *_: (l,) + rest, pipeline_mode=pl.Buffered(1))


def _mod_spec(l, tm):
    tpb = T // tm
    return pl.BlockSpec((None, 1, N_MOD, D), lambda i: (l, i // tpb, 0, 0))


def _params(n_axes):
    return pltpu.CompilerParams(dimension_semantics=("arbitrary",) * n_axes, vmem_limit_bytes=VMEM_LIMIT)


def _ada_kernel(c_ref, w_ref, b_ref, o_ref):
    c = c_ref[...]
    s = c * _sigmoid(c)
    pieces = _split3(s)
    s3 = jnp.concatenate([piece.astype(f32) for piece in pieces], axis=0).astype(bf16)
    w = w_ref[0]
    w_hi = w.astype(bf16)
    w_lo = (w - w_hi.astype(f32)).astype(bf16)
    r = _dot(s3, w_hi)
    o_ref[0] = (r[0:MOD_ROWS] + r[MOD_ROWS:2 * MOD_ROWS] + r[2 * MOD_ROWS:]
                + _dot(pieces[0], w_lo) + b_ref[0])


def _ada(cc, ada_w, ada_b):
    nt = (N_MOD * D) // ADA_TN
    return pl.pallas_call(
        _ada_kernel,
        grid=(DEPTH, nt),
        in_specs=[
            pl.BlockSpec((MOD_ROWS, D), lambda l, j: (0, 0)),
            pl.BlockSpec((1, D, ADA_TN), lambda l, j: (l, 0, j)),
            pl.BlockSpec((1, 1, ADA_TN), lambda l, j: (l, 0, j)),
        ],
        out_specs=pl.BlockSpec((1, MOD_ROWS, ADA_TN), lambda l, j: (l, 0, j)),
        out_shape=jax.ShapeDtypeStruct((DEPTH, MOD_ROWS, N_MOD * D), f32),
        compiler_params=_params(2),
        name="ada_mod",
    )(cc, ada_w, ada_b.reshape(DEPTH, 1, N_MOD * D))


def _ffn(x, g, shift, scale, gate, wi_ref, wo_ref):
    h = (_rms(x, g) * (1.0 + scale) + shift).astype(bf16)
    acc = None
    for c in range(D_FF // FF_CHUNK):
        lo, hi = c * FF_CHUNK, (c + 1) * FF_CHUNK
        gt = _dot(h, wi_ref[:, lo:hi].astype(bf16))
        up = _dot(h, wi_ref[:, D_FF + lo:D_FF + hi].astype(bf16))
        a = (gt * _sigmoid(gt) * up).astype(bf16)
        part = _dot(a, wo_ref[lo:hi, :].astype(bf16))
        acc = part if acc is None else acc + part
    return x + HALF * gate * acc


def _ffn1_kernel(*refs, split_input):
    if split_input:
        xl_ref, xc_ref, mod_ref, ng_ref, wi_ref, wo_ref, x1_ref, h_ref = refs
        x = jnp.where(pl.program_id(0) < NL // TM_FFN, xl_ref[...], xc_ref[...])
    else:
        x_ref, mod_ref, ng_ref, wi_ref, wo_ref, x1_ref, h_ref = refs
        x = x_ref[...]
    mod = mod_ref[0]
    x1 = _ffn(x, ng_ref[0:1, :], mod[0:1, :], mod[1:2, :], mod[2:3, :], wi_ref, wo_ref)
    x1_ref[...] = x1
    h_ref[...] = (_rms(x1, ng_ref[1:2, :]) * (1.0 + mod[4:5, :]) + mod[3:4, :]).astype(bf16)


def _ffn1(l, xs, mod, ng, wi, wo):
    split_input = isinstance(xs, tuple)
    nlt = NL // TM_FFN
    if split_input:
        assert NCX % TM_FFN == 0
        x_specs = [pl.BlockSpec((TM_FFN, D), lambda i: (jnp.minimum(i, nlt - 1), 0)),
                   pl.BlockSpec((TM_FFN, D), lambda i: (jnp.maximum(i - nlt, 0), 0))]
    else:
        xs = (xs,)
        x_specs = [pl.BlockSpec((TM_FFN, D), lambda i: (i, 0))]
    return pl.pallas_call(
        functools.partial(_ffn1_kernel, split_input=split_input),
        grid=(N // TM_FFN,),
        in_specs=x_specs + [_mod_spec(l, TM_FFN), _layer(ng, l), _layer(wi, l), _layer(wo, l)],
        out_specs=[pl.BlockSpec((TM_FFN, D), lambda i: (i, 0))] * 2,
        out_shape=[jax.ShapeDtypeStruct((N, D), f32), jax.ShapeDtypeStruct((N, D), bf16)],
        compiler_params=_params(1),
        name="ffn1",
    )(*xs, mod, ng, wi, wo)


def _outproj_ffn_kernel(*refs, with_ctx):
    if with_ctx:
        (x_ref, mod_ref, ng_ref, hf_ref, hb_ref, o_ref, ybl_ref, ycl_ref, ybc_ref, ycc_ref, on_ref, wout_ref,
         wi_ref, wo_ref, out_ref) = refs
        is_latent = pl.program_id(0) < NL // TM_FFN
        yb = jnp.where(is_latent, ybl_ref[...], ybc_ref[...])
        yc = jnp.where(is_latent, ycl_ref[...], ycc_ref[...])
    else:
        (x_ref, mod_ref, ng_ref, hf_ref, hb_ref, o_ref, yb_ref, yc_ref, on_ref, wout_ref,
         wi_ref, wo_ref, out_ref) = refs
        yb, yc = yb_ref[...], yc_ref[...]
    mod = mod_ref[0]
    hs = hf_ref[...] + hb_ref[...]
    sq = hs * hs
    head = lax.broadcasted_iota(jnp.int32, (1, A_HEADS * A_DV), 1) // A_DV
    ms = jnp.zeros_like(hs)
    for hh in range(A_HEADS):
        sel = head == hh
        ssh = jnp.sum(jnp.where(sel, sq, 0.0), axis=-1, keepdims=True) * (1.0 / A_DV)
        ms = jnp.where(sel, ssh, ms)
    ya = _sigmoid(o_ref[...]) * (hs * lax.rsqrt(ms + EPS) * on_ref[...])
    y = jnp.concatenate([ya.astype(bf16), yb, yc], axis=-1)
    x2 = x_ref[...] + mod[5:6, :] * _dot(y, wout_ref[...])
    out_ref[...] = _ffn(x2, ng_ref[2:3, :], mod[6:7, :], mod[7:8, :], mod[8:9, :], wi_ref, wo_ref)


def _outproj_ffn(l, x1, mod, ng, hf, hb, pa, y_lat, y_ctx, onorm, wout, wi, wo):
    with_ctx = y_ctx is not None
    rows = N if with_ctx else NL
    nlt = NL // TM_FFN
    row = lambda w, c=0: pl.BlockSpec((TM_FFN, w), lambda i: (i, c))
    lat = lambda w: pl.BlockSpec((TM_FFN, w), lambda i: (jnp.minimum(i, nlt - 1), 0))
    ctx = lambda w: pl.BlockSpec((TM_FFN, w), lambda i: (jnp.maximum(i - nlt, 0), 0))
    y_specs = [lat(B_HEADS * B_DV), lat(C_HEADS * C_DH)]
    if with_ctx:
        y_specs += [ctx(B_HEADS * B_DV), ctx(C_HEADS * C_DH)]
    return pl.pallas_call(
        functools.partial(_outproj_ffn_kernel, with_ctx=with_ctx),
        grid=(rows // TM_FFN,),
        in_specs=[
            row(D), _mod_spec(l, TM_FFN), _layer(ng, l),
            row(A_HEADS * A_DV), row(A_HEADS * A_DV), row(A_HEADS * A_DV, PA_O // (A_HEADS * A_DV)),
        ] + y_specs + [_layer(onorm, l), _layer(wout, l), _layer(wi, l), _layer(wo, l)],
        out_specs=row(D),
        out_shape=jax.ShapeDtypeStruct((rows, D), f32),
        compiler_params=_params(1),
        name="outproj_ffn2",
    )(x1, mod, ng, hf, hb, pa, *y_lat, *(y_ctx or ()), onorm, wout, wi, wo)


def _inproj_kernel(h_ref, win_ref, bias_ref, tab_ref, cqn_ref, ckvn_ref, wuq_ref, wukv_ref, hg_ref,
                   pa_ref, kt_ref, gt_ref, qb_ref, kb_ref, vb_ref, qc_ref, kc_ref, vc_ref):
    h = h_ref[...]
    p = _dot(h, win_ref[...]) + bias_ref[...]

    pa_ref[:, PA_V:PA_Q] = p[:, COL_AV:COL_AG]
    pa_ref[:, PA_Q:PA_G] = p[:, COL_AQ:COL_AV] * (A_DK ** -0.5)
    graw = p[:, COL_AG:COL_AG + LANES]
    lane = lax.broadcasted_iota(jnp.int32, (1, LANES), 1)
    for kk in range(4):
        gk = graw if kk == 0 else pltpu.roll(graw, LANES - A_HEADS * kk, 1)
        if kk % 2 == 1:
            gk = _log_sigmoid(gk)
        pa_ref[:, PA_G + LANES * kk:PA_G + LANES * (kk + 1)] = jnp.where(lane < A_HEADS, gk, 0.0)

    grow = lax.broadcasted_iota(jnp.int32, (2 * N_GATES, 1), 0)
    is_forget = ((grow // A_HEADS) % 2 == 1) == (grow < N_GATES)
    gate = grow % N_GATES
    source_lane = jnp.where(grow < N_GATES, gate, ((gate // A_HEADS) ^ 1) * A_HEADS + gate % A_HEADS)
    pick_gates = (lane == source_lane).astype(bf16)
    eye = (lax.broadcasted_iota(jnp.int32, (LANES, LANES), 0)
           == lax.broadcasted_iota(jnp.int32, (LANES, LANES), 1)).astype(bf16)
    keys = p[:, COL_AK:COL_AK + LANES].astype(bf16)
    gate_pieces = _split3(graw)
    for c in range(TM_IN // A_CHUNK):
        rows = slice(c * A_CHUNK, (c + 1) * A_CHUNK)
        kt_ref[c] = _dot_nt(eye, keys[rows, :])
        g = sum(_dot_nt(pick_gates, piece[rows, :]) for piece in gate_pieces)
        gt_ref[c] = jnp.where(is_forget, _log_sigmoid(g), g)

    is_latent = pl.program_id(0) < NL // TM_IN
    tab = tab_ref[...]
    tb = [jnp.where(is_latent, tab[:, LANES * i:LANES * (i + 1)], 1.0 - (i % 2)) for i in range(4)]
    hg = hg_ref[...]

    cq = _rms(p[:, COL_BCQ:COL_BCKV], cqn_ref[...]).astype(bf16)
    ckv = _rms(p[:, COL_BCKV:COL_BKR], ckvn_ref[...]).astype(bf16)
    kr = p[:, COL_BKR:COL_CQ]
    q = _dot(cq, wuq_ref[...])
    kv = _dot(ckv, wukv_ref[...])
    vb_ref[...] = kv[:, B_HEADS * LANES:].astype(bf16)
    vc_ref[...] = p[:, COL_CV:COL_CV + LANES].astype(bf16)

    two = lambda a: jnp.concatenate([a, a], axis=1)
    pair = lambda a, j, first=0: a[:, first + 2 * LANES * j:first + 2 * LANES * (j + 1)]
    cos_b, sin_b, cos_c, sin_c = (two(t) for t in tb)
    kr2 = two(kr)
    ri = lax.broadcasted_iota(jnp.int32, (2 * LANES, 2 * LANES), 0)
    ci = lax.broadcasted_iota(jnp.int32, (2 * LANES, 2 * LANES), 1)
    swap_b = ((ri // LANES == ci // LANES) & (ri % LANES == (ci + HALF_LANES) % LANES)).astype(bf16)
    half_c = C_DH // 2
    swap_c = ((ri // half_c == ci // half_c) & (ri % half_c == (ci + half_c // 2) % half_c)).astype(bf16)
    full = slice(0, 2 * LANES)
    jobs = []
    for j in range(B_HEADS // 2):
        dst = slice(2 * LANES * j, 2 * LANES * (j + 1))
        jobs.append((pair(q, j), hg[0:1, :], LANES, B_DQK, cos_b, sin_b, swap_b, [(qb_ref, dst, full)]))
        jobs.append((pair(kv, j) + kr2, hg[1:2, :], LANES, B_DQK, cos_b, sin_b, swap_b, [(kb_ref, dst, full)]))
    jobs.append((pair(p, 0, COL_CQ), hg[2:3, :], C_DH, C_DH, cos_c, sin_c, swap_c, [(qc_ref, full, full)]))
    jobs.append((pair(p, 1, COL_CQ), hg[3:4, :], C_DH, C_DH, cos_c, sin_c, swap_c,
                 [(qc_ref, slice(2 * LANES, 3 * LANES), slice(0, LANES)),
                  (kc_ref, slice(0, LANES), slice(LANES, 2 * LANES))]))
    gained = [x * gain for x, gain, *_ in jobs]
    rolled = [_dot(xg.astype(bf16), job[6]) for job, xg in zip(jobs, gained)]
    lane = lax.broadcasted_iota(jnp.int32, (1, LANES), 1)
    lane2 = lax.broadcasted_iota(jnp.int32, (1, 2 * LANES), 1)
    sums = []
    for x, _, width, *_ in jobs:
        sq = x * x
        parts = []
        for t in range(2):
            blk = sq[:, LANES * t:LANES * (t + 1)]
            if width == LANES:
                parts.append(jnp.sum(blk, axis=-1, keepdims=True))
            else:
                parts.append(jnp.sum(jnp.where(lane < width, blk, 0.0), axis=-1, keepdims=True))
                parts.append(jnp.sum(jnp.where(lane < width, 0.0, blk), axis=-1, keepdims=True))
        ss = parts[-1]
        for k in range(len(parts) - 2, -1, -1):
            ss = jnp.where(lane2 < (k + 1) * width, parts[k], ss)
        sums.append(ss)
    for (_, _, _, n_real, cos, sin, _, dests), xg, xr, ss in zip(jobs, gained, rolled, sums):
        out = ((xg * cos + xr * sin) * lax.rsqrt(ss * (1.0 / n_real) + EPS)).astype(bf16)
        for dst_ref, dst_cols, src_cols in dests:
            dst_ref[:, dst_cols] = out[:, src_cols]


def _inproj(l, h, tab, win, bias, cqn, ckvn, wuq, wukv, hg):
    tpb = T // TM_IN
    row = lambda w: pl.BlockSpec((TM_IN, w), lambda i: (i, 0))
    chunked = lambda r: pl.BlockSpec((TM_IN // A_CHUNK, r, A_CHUNK), lambda i: (i, 0, 0))
    out_w = [(B_HEADS * LANES, bf16), (B_HEADS * LANES, bf16), (B_HEADS * B_DV, bf16),
             (C_HEADS * C_DH, bf16), (C_KV_HEADS * C_DH, bf16), (C_KV_HEADS * C_DH, bf16)]
    return pl.pallas_call(
        _inproj_kernel,
        grid=(N // TM_IN,),
        in_specs=[
            row(D), _layer(win, l), _layer(bias, l),
            pl.BlockSpec((TM_IN, 4 * LANES), lambda i: (jnp.where(i < NL // TM_IN, i % tpb, 0), 0)),
            _layer(cqn, l), _layer(ckvn, l), _layer(wuq, l), _layer(wukv, l),
            _layer(hg, l),
        ],
        out_specs=[row(PA_W), chunked(LANES), chunked(2 * N_GATES)] + [row(w) for w, _ in out_w],
        out_shape=[jax.ShapeDtypeStruct((N, PA_W), f32),
                   jax.ShapeDtypeStruct((N // A_CHUNK, LANES, A_CHUNK), f32),
                   jax.ShapeDtypeStruct((N // A_CHUNK, 2 * N_GATES, A_CHUNK), f32)]
        + [jax.ShapeDtypeStruct((N, w), dt) for w, dt in out_w],
        compiler_params=_params(1),
        name="inproj",
    )(h, win, bias, tab, cqn, ckvn, wuq, wukv, hg)


def _cummax_rows(x, rev):
    n = x.shape[0]
    row = lax.broadcasted_iota(jnp.int32, (n, 1), 0)
    sh = 1
    while sh < n:
        if rev:
            x = jnp.maximum(x, jnp.where(row < n - sh, pltpu.roll(x, n - sh, 0), -jnp.inf))
        else:
            x = jnp.maximum(x, jnp.where(row >= sh, pltpu.roll(x, sh, 0), -jnp.inf))
        sh *= 2
    return x


def _stack_heads(pieces):
    return jnp.concatenate(pieces, axis=0)


def _mlstm_kernel(*refs):
    streams = [(refs[0:7] + refs[14:15], False), (refs[7:14] + refs[15:16], True)]
    s_ref, ml_ref, ms_ref = refs[16:19]

    @pl.when(pl.program_id(1) == 0)
    def _():
        s_ref[...] = jnp.zeros_like(s_ref)
        ml_ref[...] = jnp.zeros_like(ml_ref)
        ms_ref[...] = jnp.zeros_like(ms_ref)

    L = A_CHUNK
    n_chunks = MLSTM_R // L
    heads = range(A_HEADS)
    ti = lax.broadcasted_iota(jnp.int32, (L, L), 0)
    si = lax.broadcasted_iota(jnp.int32, (L, L), 1)
    lane = lax.broadcasted_iota(jnp.int32, (1, LANES), 1)
    row8 = lax.broadcasted_iota(jnp.int32, (2 * A_HEADS, 1), 0)
    in_head = [(lane >= hh * A_DK) & (lane < (hh + 1) * A_DK) for hh in heads]
    ones_blk = jnp.ones((L, LANES), bf16)

    items = []
    for sidx, (srefs, rev) in enumerate(streams):
        q_ref, kt_ref, v_ref, ig_ref, lf_ref, gt_ref, gts_ref, h_ref = srefs
        attend = (si >= ti) if rev else (si <= ti)
        attend4 = _stack_heads([attend] * A_HEADS)
        cum_cols = attend.astype(bf16)
        cum_rows = ((ti >= si) if rev else (ti <= si)).astype(bf16)
        m_lane = ml_ref[sidx, 0:1, :]
        m_sub = ms_ref[sidx, :, 0:1]
        for cc in (range(n_chunks - 1, -1, -1) if rev else range(n_chunks)):
            rows = slice(cc * L, (cc + 1) * L)
            ig = ig_ref[rows, :]
            lf = lf_ref[rows, :]
            gt = gt_ref[cc]
            gts = gts_ref[cc]
            b_col = sum(_dot(cum_cols, piece) for piece in _split3(lf))
            r_col = ig - b_col
            big_m = jnp.maximum(m_lane, _cummax_rows(r_col, rev))
            mt_col = b_col + big_m
            b_last_l = jnp.sum(lf, axis=0, keepdims=True)
            m_new_l = jnp.maximum(m_lane, jnp.max(r_col, axis=0, keepdims=True)) + b_last_l
            b_rows = sum(_dot(piece, cum_rows) for piece in _split3(gts))
            live = row8 < A_HEADS
            r8 = jnp.where(live, gt - b_rows, 0.0)
            b_last_s = jnp.where(live, jnp.sum(gts, axis=-1, keepdims=True), 0.0)
            r_max_s = jnp.max(r8, axis=-1, keepdims=True)
            wg8 = jnp.exp(r8 - r_max_s)
            m_new_s = jnp.maximum(m_sub, r_max_s) + b_last_s
            decay_s = jnp.exp(b_last_s + m_sub - m_new_s)
            scale_s = jnp.exp(b_last_s + r_max_s - m_new_s)
            expand = lambda a, n: _stack_heads([jnp.broadcast_to(a[hh:hh + 1, :], (n, a.shape[1])) for hh in heads])
            q = q_ref[rows, :]
            big_m_b = _stack_heads([jnp.broadcast_to(big_m[:, hh:hh + 1], (L, LANES)) for hh in heads])
            mt_b = _stack_heads([jnp.broadcast_to(mt_col[:, hh:hh + 1], (L, LANES)) for hh in heads])
            m_old_b = _stack_heads([jnp.broadcast_to(m_lane[:, hh:hh + 1], (L, LANES)) for hh in heads])
            q_stack = _stack_heads([jnp.where(in_head[hh], q, 0.0) for hh in heads])
            items.append(dict(
                sidx=sidx, rows=rows, h_ref=h_ref,
                qst=q_stack.astype(bf16),
                qa=(q_stack * jnp.exp(m_old_b - big_m_b)).astype(bf16),
                kt=kt_ref[cc].astype(bf16),
                kw=(kt_ref[cc] * expand(wg8, A_DK)).astype(bf16),
                vo=jnp.concatenate([v_ref[rows, :].astype(bf16), ones_blk], axis=1),
                w=jnp.exp(jnp.where(attend4, expand(r8, L) - big_m_b[:, 0:L], -jnp.inf)),
                floor=jnp.exp(-mt_b),
                decay=jnp.broadcast_to(expand(decay_s, A_DK), (LANES, LANES)),
                kv_scale=jnp.broadcast_to(expand(scale_s, A_DK), (LANES, LANES))))
            m_lane, m_sub = m_new_l, m_new_s
        ml_ref[sidx, 0:1, :] = m_lane
        ms_ref[sidx, :, 0:1] = m_sub

    for it in items:
        it["s"] = _dot(it["qst"], it["kt"])

    for it in items:
        it["kv"] = _dot(it["kw"], it["vo"])

    for it in items:
        it["p"] = (it["s"] * it["w"]).astype(bf16)

    state = [s_ref[0], s_ref[1]]
    tile3 = lambda a: jnp.concatenate([a] * 3, axis=1)
    for it in items:
        st = state[it["sidx"]]
        it["c_in"] = st.astype(bf16)
        state[it["sidx"]] = tile3(it["decay"]) * st + tile3(it["kv_scale"]) * it["kv"]
    s_ref[0] = state[0]
    s_ref[1] = state[1]

    nv = A_HEADS * A_DV
    for it in items:
        out = _dot(jnp.concatenate([it["qa"], it["p"]], axis=1),
                   jnp.concatenate([it["c_in"], it["vo"]], axis=0))
        den = jnp.maximum(jnp.abs(out[:, nv:]), it["floor"])
        for pair in range(A_HEADS // 2):
            sl = slice(LANES * pair, LANES * (pair + 1))
            even, odd = (slice(L * hh, L * (hh + 1)) for hh in (2 * pair, 2 * pair + 1))
            it["h_ref"][it["rows"], sl] = jnp.where(lane < A_DV, out[even, sl] / den[even, :], out[odd, sl] / den[odd, :])


def _mlstm(pa, kt, gt):
    nb = T // MLSTM_R
    nc = MLSTM_R // A_CHUNK
    assert CTX == MLSTM_R

    def rb(rev):
        def f(b, j):
            jj = j - 1
            return jnp.where(j == 0, NL // MLSTM_R + b, b * nb + (nb - 1 - jj if rev else jj))
        return f

    def stream_specs(rev):
        r = rb(rev)
        d = 2 if rev else 0
        col = lambda w, c: pl.BlockSpec((MLSTM_R, w), lambda b, j: (r(b, j), c))
        return [
            col(LANES, PA_Q // LANES),
            pl.BlockSpec((nc, LANES, A_CHUNK), lambda b, j: (r(b, j), 0, 0)),
            col(2 * LANES, PA_V // (2 * LANES)),
            col(LANES, PA_G // LANES + d), col(LANES, PA_G // LANES + d + 1),
            pl.BlockSpec((nc, 2 * A_HEADS, A_CHUNK), lambda b, j: (r(b, j), d // 2, 0)),
            pl.BlockSpec((nc, 2 * A_HEADS, A_CHUNK), lambda b, j: (r(b, j), 2 + d // 2, 0)),
        ]

    out_spec = lambda rev: pl.BlockSpec((MLSTM_R, A_HEADS * A_DV), lambda b, j: (rb(rev)(b, j), 0))
    return pl.pallas_call(
        _mlstm_kernel,
        grid=(B, nb + 1),
        in_specs=stream_specs(False) + stream_specs(True),
        out_specs=[out_spec(False), out_spec(True)],
        out_shape=[jax.ShapeDtypeStruct((N, A_HEADS * A_DV), f32)] * 2,
        scratch_shapes=[pltpu.VMEM((2, LANES, 3 * LANES), f32), pltpu.VMEM((2, 8, LANES), f32),
                        pltpu.VMEM((2, 8, LANES), f32)],
        compiler_params=_params(2),
        name="mlstm",
    )(*[pa, kt, pa, pa, pa, gt, gt] * 2)


def _tile_max(s, m128):
    for t in range(s.shape[1] // LANES):
        blk = s[:, LANES * t:LANES * (t + 1)]
        m128 = blk if m128 is None else jnp.maximum(m128, blk)
    return m128


def _mla_kernel(*refs, latent):
    if latent:
        q_ref, kc_ref, vc_ref, kl_ref, vl_ref, o_ref, s_ref = refs
        sources = [(kc_ref, vc_ref, 0, CTX)] + [(kl_ref, vl_ref, c, MLA_KC) for c in range(0, T, MLA_KC)]
    else:
        q_ref, kc_ref, vc_ref, o_ref, s_ref = refs
        sources = [(kc_ref, vc_ref, 0, CTX)]
    q = q_ref[...]
    lane = lax.broadcasted_iota(jnp.int32, (1, LANES), 1)
    row_max = []
    for hh in range(2):
        sl = slice(LANES * hh, LANES * (hh + 1))
        qh = q[:, sl]
        m128 = None
        off = 0
        for k_ref, _, r0, n in sources:
            s = _dot_nt(qh, k_ref[r0:r0 + n, sl])
            s_ref[hh, :, off:off + n] = s
            m128 = _tile_max(s, m128)
            off += n
        row_max.append(jnp.max(m128, axis=-1, keepdims=True))
    outs = []
    for hh in range(2):
        den_lane = B_DV if hh == 0 else 0
        acc = None
        off = 0
        for _, v_ref, r0, n in sources:
            p = jnp.exp2(s_ref[hh, :, off:off + n] - row_max[hh]).astype(bf16)
            vext = jnp.where(lane == den_lane, 1.0, v_ref[r0:r0 + n, :]).astype(bf16)
            part = _dot(p, vext)
            acc = part if acc is None else acc + part
            off += n
        outs.append(acc / acc[:, den_lane:den_lane + 1])
    o_ref[...] = jnp.where(lane < B_DV, outs[0], outs[1]).astype(o_ref.dtype)


def _mla(qb, kb, vb, latent):
    npair = B_HEADS // 2
    ctx_blk = NL // CTX
    kv_specs = [
        pl.BlockSpec((CTX, 2 * LANES), lambda b, p, i: (ctx_blk + b, p)),
        pl.BlockSpec((CTX, LANES), lambda b, p, i: (ctx_blk + b, p)),
    ]
    if latent:
        tq = MLA_TQ
        nq = T // tq
        qmap = omap = lambda b, p, i: (b * nq + i, p)
        kv_specs += [
            pl.BlockSpec((T, 2 * LANES), lambda b, p, i: (b, p)),
            pl.BlockSpec((T, LANES), lambda b, p, i: (b, p)),
        ]
        args = (qb, kb, vb, kb, vb)
        nkeys = CTX + T
    else:
        tq = CTX
        nq = 1
        qmap = lambda b, p, i: (ctx_blk + b, p)
        omap = lambda b, p, i: (b, p)
        args = (qb, kb, vb)
        nkeys = CTX
    return pl.pallas_call(
        functools.partial(_mla_kernel, latent=latent),
        grid=(B, npair, nq),
        in_specs=[pl.BlockSpec((tq, 2 * LANES), qmap)] + kv_specs,
        out_specs=pl.BlockSpec((tq, LANES), omap),
        out_shape=jax.ShapeDtypeStruct((NL if latent else NCX, B_HEADS * B_DV), bf16),
        scratch_shapes=[pltpu.VMEM((2, tq, nkeys), f32)],
        compiler_params=_params(3),
        name="mla_latent" if latent else "mla_context",
    )(*args)


def _gqa_kernel(sink_ref, *refs, latent):
    if latent:
        q_ref, kc_ref, vc_ref, kl_ref, vl_ref, o_ref = refs
    else:
        q_ref, kc_ref, vc_ref, o_ref = refs
    q = q_ref[...]
    tq = q.shape[0]
    lane = lax.broadcasted_iota(jnp.int32, (1, LANES), 1)
    keys = kc_ref[...]
    vals = vc_ref[...]
    valid = None
    if latent:
        n = pl.program_id(1)
        start = pl.multiple_of(jnp.clip(n * GQA_TQ - WINDOW, 0, T - GQA_BAND), WINDOW)
        keys = jnp.concatenate([keys, kl_ref[pl.ds(start, GQA_BAND), :]], axis=0)
        vals = jnp.concatenate([vals, vl_ref[pl.ds(start, GQA_BAND), :]], axis=0)
        qpos = n * GQA_TQ + lax.broadcasted_iota(jnp.int32, (tq, 1), 0)
        kidx = lax.broadcasted_iota(jnp.int32, (1, CTX + GQA_BAND), 1)
        valid = (kidx < CTX) | (jnp.abs(qpos - (start - CTX + kidx)) <= WINDOW)
    lo = lane < C_DH
    outs = [[], []]
    scores = []
    for kvh in range(C_KV_HEADS):
        mine = lo if kvh == 0 else ~lo
        qs = jnp.concatenate([jnp.where(mine, q[:, LANES * g:LANES * (g + 1)], 0) for g in range(C_GROUP)], axis=0)
        scores.append(_dot_nt(qs, keys))
    for kvh, s_all in enumerate(scores):
        den_lane = C_DH * (1 - kvh)
        vext = jnp.where(lane == den_lane, 1.0, vals).astype(bf16)
        for g in range(C_GROUP):
            s = s_all[g * tq:(g + 1) * tq, :]
            if latent:
                s = jnp.where(valid, s, -jnp.inf)
            sink = sink_ref[C_GROUP * kvh + g] * LOG2E
            m = jnp.maximum(sink, jnp.max(_tile_max(s, None), axis=-1, keepdims=True))
            acc = _dot(jnp.exp2(s - m).astype(bf16), vext)
            outs[kvh].append(acc / (jnp.exp2(sink - m) + acc[:, den_lane:den_lane + 1]))
    for g in range(C_GROUP):
        o_ref[:, LANES * g:LANES * (g + 1)] = jnp.where(lo, outs[0][g], outs[1][g]).astype(o_ref.dtype)


def _gqa(sink, qc, kc, vc, latent):
    ctx_blk = NL // CTX
    kv_specs = [
        pl.BlockSpec((CTX, LANES), lambda b, i: (ctx_blk + b, 0)),
        pl.BlockSpec((CTX, LANES), lambda b, i: (ctx_blk + b, 0)),
    ]
    if latent:
        tq = GQA_TQ
        nq = T // tq
        qmap = omap = lambda b, i: (b * nq + i, 0)
        kv_specs += [
            pl.BlockSpec((T, LANES), lambda b, i: (b, 0)),
            pl.BlockSpec((T, LANES), lambda b, i: (b, 0)),
        ]
        args = (sink, qc, kc, vc, kc, vc)
    else:
        tq = CTX
        nq = 1
        qmap = lambda b, i: (ctx_blk + b, 0)
        omap = lambda b, i: (b, 0)
        args = (sink, qc, kc, vc)
    return pl.pallas_call(
        functools.partial(_gqa_kernel, latent=latent),
        grid=(B, nq),
        in_specs=[pl.BlockSpec(memory_space=pltpu.SMEM), pl.BlockSpec((tq, C_HEADS * C_DH), qmap)] + kv_specs,
        out_specs=pl.BlockSpec((tq, C_HEADS * C_DH), omap),
        out_shape=jax.ShapeDtypeStruct((NL if latent else NCX, C_HEADS * C_DH), bf16),
        compiler_params=_params(2),
        name="gqa_latent" if latent else "gqa_context",
    )(*args)


def _pad_cols(w, width):
    return jnp.pad(w, ((0, 0), (0, width - w.shape[1])))


def _lane_runs(lane_map):
    runs = []
    for src in lane_map:
        src = int(src)
        if runs and ((src < 0 and runs[-1][0] < 0) or (src >= 0 and runs[-1][0] >= 0 and src == sum(runs[-1]))):
            runs[-1] = (runs[-1][0], runs[-1][1] + 1)
        else:
            runs.append((src, 1))
    return runs


def _place(w, lane_map):
    parts = [jnp.zeros(w.shape[:-1] + (n,), w.dtype) if s < 0 else w[..., s:s + n] for s, n in _lane_runs(lane_map)]
    return jnp.concatenate(parts, axis=-1)


def _place_heads(w, heads, lane_map):
    r = w.shape[0]
    return _place(w.reshape(r, heads, -1), lane_map).reshape(r, heads * LANES)


def _reorder_c_heads(w, axis):
    heads = jnp.split(w, C_HEADS, axis=axis)
    return jnp.concatenate([heads[h] for h in C_HEAD_ORDER], axis=axis)


IN_SIZES = (A_HEADS * A_DK, A_HEADS * A_DK, A_HEADS * A_DV, A_HEADS * A_DV, N_GATES,
            B_Q_RANK, B_KV_RANK, B_ROPE, C_HEADS * C_DH, C_KV_HEADS * C_DH, C_KV_HEADS * C_DH)
IN_OFFSETS = tuple(int(v) for v in np.cumsum((0,) + IN_SIZES))


def _in_part(w, i):
    return w[..., IN_OFFSETS[i]:IN_OFFSETS[i + 1]]


def _arrange_w_in(w, map_b):
    part = lambda i: _in_part(w, i)
    kr = _place(part(7), np.where(map_b >= B_NOPE, map_b - B_NOPE, -1))
    return jnp.concatenate([
        part(0), part(2), part(3), _pad_cols(part(4), LANES),
        part(5), part(6), kr,
        _reorder_c_heads(part(8), 1), part(9), part(10), part(1),
    ], axis=1)


def _rope_tables(map_b, map_c):
    assert T == GRID_W * GRID_W
    pos = jnp.arange(GRID_W, dtype=f32)[:, None]
    small, by_col = [], []
    for lane_map, rope_start, half in ((map_b, B_NOPE, B_ROPE // 4), (map_c, 0, C_DH // 4)):
        rel = lane_map - rope_start
        in_rope = (lane_map >= 0) & (rel >= 0) & (rel < 4 * half)
        rel = np.where(in_rope, rel, 0)
        second = jnp.asarray((rel // half) % 2 == 1)[None, :]
        freq = ROPE_BASE ** (-jnp.asarray(rel % half, f32) / half)
        rot = jnp.asarray(in_rope)[None, :]
        ang = pos * freq[None, :]
        sin = jnp.sin(ang)
        small += [jnp.where(rot, jnp.cos(ang), 1.0), jnp.where(rot, jnp.where(second, sin, -sin), 0.0)]
        by_col += [rel >= 2 * half] * 2
    small = jnp.concatenate(small, axis=1)
    by_col = jnp.asarray(np.concatenate(by_col))[None, None, :]
    shape = (GRID_W, GRID_W, small.shape[1])
    full = jnp.where(by_col, jnp.broadcast_to(small[None], shape), jnp.broadcast_to(small[:, None], shape))
    return full.reshape(T, small.shape[1])


def kernel(x, c, ctx, c_ctx, ada_w, ada_b, norm_g, ffn1_wi, ffn1_wo, ffn2_wi, ffn2_wo, w_in, w_out,
           mlstm_gate_b, mlstm_out_norm, mla_cq_norm, mla_ckv_norm, mla_w_uq, mla_w_ukv, mla_q_norm, mla_k_norm,
           gqa_q_norm, gqa_k_norm, gqa_sink):
    map_b, map_c = _head_lane_map_b(), _head_lane_map_c()
    nope_map = np.where(map_b < B_NOPE, map_b, -1)
    q_scale_b, q_scale_c = B_DQK ** -0.5 * LOG2E, C_DH ** -0.5 * LOG2E

    def arrange(w_in_l, gate_b_l, w_uq_l, w_ukv_l, bq_l, bk_l, cq_l, ck_l):
        ukv = w_ukv_l.reshape(B_KV_RANK, B_HEADS, B_NOPE + B_DV)
        gbq, gbk = _place(bq_l[None], map_b) * q_scale_b, _place(bk_l[None], map_b)
        gcq, gck = _place(cq_l[None], map_c) * q_scale_c, _place(ck_l[None], map_c)
        return dict(
            win=_arrange_w_in(w_in_l, map_b).astype(bf16),
            bias=_pad_cols(jnp.pad(gate_b_l[None], ((0, 0), (COL_AG, 0))), WP),
            wuq=_place_heads(w_uq_l, B_HEADS, map_b).astype(bf16),
            wukv=jnp.concatenate([_place_heads(ukv[:, :, :B_NOPE].reshape(B_KV_RANK, -1), B_HEADS, nope_map),
                                  ukv[:, :, B_NOPE:].reshape(B_KV_RANK, -1)], axis=1).astype(bf16),
            hg=jnp.concatenate([jnp.concatenate(pair, axis=1)
                                for pair in ((gbq, gbq), (gbk, gbk), (gcq, gcq), (gcq, gck))]))

    pw = jax.vmap(arrange)(w_in, mlstm_gate_b, mla_w_uq, mla_w_ukv, mla_q_norm, mla_k_norm, gqa_q_norm, gqa_k_norm)
    wi1, wo1, wi2, wo2 = ffn1_wi, ffn1_wo, ffn2_wi, ffn2_wo
    c_rows = A_HEADS * A_DV + B_HEADS * B_DV
    wout = jnp.concatenate([w_out[:, :c_rows], _reorder_c_heads(w_out[:, c_rows:], 1)], axis=1).astype(bf16)
    cqn, ckvn, onorm = mla_cq_norm[:, None, :], mla_ckv_norm[:, None, :], mlstm_out_norm[:, None, :]

    cc = jnp.concatenate([c, c_ctx[None, :], jnp.zeros((MOD_ROWS - B - 1, D), f32)], axis=0)
    mod = _ada(cc, ada_w, ada_b).reshape(DEPTH, MOD_ROWS, N_MOD, D)
    tab = _rope_tables(map_b, map_c)
    xs = (x.reshape(NL, D), ctx.reshape(NCX, D))

    for l in range(DEPTH):
        need_ctx = l < DEPTH - 1
        x1, h = _ffn1(l, xs, mod, norm_g, wi1, wo1)
        pa, kt, gt, qb, kb, vb, qc, kc, vc = _inproj(
            l, h, tab, pw["win"], pw["bias"], cqn, ckvn, pw["wuq"], pw["wukv"], pw["hg"])
        hf, hb = _mlstm(pa, kt, gt)
        y_lat = (_mla(qb, kb, vb, True), _gqa(gqa_sink[l], qc, kc, vc, True))
        y_ctx = (_mla(qb, kb, vb, False), _gqa(gqa_sink[l], qc, kc, vc, False)) if need_ctx else None
        xs = _outproj_ffn(l, x1, mod, norm_g, hf, hb, pa, y_lat, y_ctx, onorm, wout, wi2, wo2)
    return xs.reshape(B, T, D)
```
